```python
import math
import jax, jax.numpy as jnp
from jax import lax
import numpy as np

D_MODEL = 1024
BATCH = 2
SEQ = 8192
DEPTH = 4
DEC_BATCH = 32
DEC_SEQ = 4
PAST_LEN = 8192
PAGE_SIZE = 128

CONV_DIM = D_MODEL // 4
ATT_HEADS = 4
ATT_HDIM = D_MODEL // 16
ATT_DIM = ATT_HEADS * ATT_HDIM
HG_HEADS = 4
HG_DIM = D_MODEL - CONV_DIM - ATT_DIM
HG_KDIM = HG_DIM // HG_HEADS
HG_VDIM = HG_DIM // HG_HEADS
HG_CHUNK = 64
CONV_WIDTH = 31
DILATED_BRANCHES = ((128, 1), (512, 4), (2048, 16))
ATT_MAX_WIN = 2048
FFN_DIM = ((8 * D_MODEL // 3 + 255) // 256) * 256
EPS = 1e-6
NEG_BIG = -1e30
N_IN = 2 * CONV_DIM + 3 * ATT_DIM + 4 * HG_DIM
SPLITS = (2 * CONV_DIM,
          2 * CONV_DIM + ATT_DIM,
          2 * CONV_DIM + 2 * ATT_DIM,
          2 * CONV_DIM + 3 * ATT_DIM,
          2 * CONV_DIM + 3 * ATT_DIM + HG_DIM,
          2 * CONV_DIM + 3 * ATT_DIM + 2 * HG_DIM,
          2 * CONV_DIM + 3 * ATT_DIM + 3 * HG_DIM)

kernel_name = "hymba_conv_dilatedattn_hgrn2_decode_step"

F32 = jnp.float32


def _rmsnorm(x, g):
    x32 = x.astype(F32)
    y = x32 * lax.rsqrt(jnp.mean(x32 * x32, axis=-1, keepdims=True) + EPS)
    return (y * g.astype(F32)).astype(x.dtype)


def _layernorm(x, g, b):
    x32 = x.astype(F32)
    xc = x32 - jnp.mean(x32, axis=-1, keepdims=True)
    y = xc * lax.rsqrt(jnp.mean(xc * xc, axis=-1, keepdims=True) + EPS)
    return (y * g.astype(F32) + b.astype(F32)).astype(x.dtype)


def _swiglu(h, w_gate, w_up, w_down):
    return (jax.nn.silu(h @ w_gate) * (h @ w_up)) @ w_down


def _alibi_slopes():
    return jnp.asarray([2.0 ** (-8.0 * (h + 1) / ATT_HEADS) for h in range(ATT_HEADS)], dtype=F32)


def _conv_group(z, conv_state, dw_w, dw_b, ln_g, ln_b):
    a, gate = jnp.split(z, 2, axis=-1)
    u = a * jax.nn.sigmoid(gate)
    full = jnp.concatenate([conv_state.astype(u.dtype), u], axis=1)
    y = lax.conv_general_dilated(full, dw_w[:, None, :].astype(u.dtype), window_strides=(1,),
                                 padding="VALID", dimension_numbers=("NWC", "WIO", "NWC"),
                                 feature_group_count=CONV_DIM)
    y = jax.nn.silu(_layernorm(y + dw_b, ln_g, ln_b))
    return y, full[:, -(CONV_WIDTH - 1):]


def _dilated_branch_prompt(q, k, v, window, dil, slopes):
    B, S, H, Dh = q.shape
    blk = window // dil
    span = window
    Sp = -(-S // span) * span
    nb = Sp // span

    def to_blocks(t):
        t = jnp.pad(t.astype(F32), ((0, 0), (0, Sp - S), (0, 0), (0, 0)))
        t = t.reshape(B, Sp // dil, dil, H, Dh).transpose(0, 2, 3, 1, 4)
        return t.reshape(B, dil, H, nb, blk, Dh)

    def with_prev(t):
        prev = jnp.pad(t, ((0, 0), (0, 0), (0, 0), (1, 0), (0, 0), (0, 0)))[:, :, :, :-1]
        return jnp.concatenate([prev, t], axis=4)

    qb = to_blocks(q)
    kk = with_prev(to_blocks(k))
    vv = with_prev(to_blocks(v))
    s = jnp.einsum("brhnqd,brhnkd->brhnqk", qb, kk)
    qi = jnp.arange(blk)[:, None]
    ki = jnp.arange(2 * blk)[None, :]
    j = qi - ki + blk
    n = jnp.arange(nb)[:, None, None]
    valid = (j >= 0) & (j <= blk) & ((n > 0) | (ki >= blk))
    bias = -slopes[:, None, None, None] * (j * dil).astype(F32)
    s = jnp.where(valid, s + bias, NEG_BIG)
    m = jnp.max(s, axis=-1, keepdims=True)
    p = jnp.exp(s - m)
    l = jnp.sum(p, axis=-1)
    o = jnp.einsum("brhnqk,brhnkd->brhnqd", p, vv) / l[..., None]
    lse = m[..., 0] + jnp.log(l)
    o = o.reshape(B, dil, H, Sp // dil, Dh).transpose(0, 3, 1, 2, 4).reshape(B, Sp, H, Dh)[:, :S]
    lse = lse.reshape(B, dil, H, Sp // dil).transpose(0, 3, 1, 2).reshape(B, Sp, H)[:, :S]
    return o, lse


def _dilated_branch_sample(q, kf, vf, n_past, window, dil, slopes):
    T = q.shape[1]
    jj = jnp.arange(window // dil + 1)
    idx = n_past + jnp.arange(T)[:, None] - jj[None, :] * dil
    valid = idx >= 0
    idx = jnp.maximum(idx, 0)
    kg = kf[:, idx].astype(F32)
    vg = vf[:, idx].astype(F32)
    s = jnp.einsum("bthd,btjhd->bthj", q, kg)
    bias = -slopes[:, None] * (jj * dil).astype(F32)
    s = jnp.where(valid[:, None, :], s + bias, NEG_BIG)
    m = jnp.max(s, axis=-1, keepdims=True)
    p = jnp.exp(s - m)
    l = jnp.sum(p, axis=-1)
    o = jnp.einsum("bthj,btjhd->bthd", p, vg) / l[..., None]
    lse = m[..., 0] + jnp.log(l)
    return o, lse


def _att_group(q, k, v, k_buf, v_buf, slopes):
    qs = q.astype(F32) * (ATT_HDIM ** -0.5)
    outs, lses = [], []
    if k_buf is None:
        for window, dil in DILATED_BRANCHES:
            o, lse = _dilated_branch_prompt(qs, k, v, window, dil, slopes)
            outs.append(o)
            lses.append(lse)
    else:
        kf = jnp.concatenate([k_buf.astype(k.dtype), k], axis=1)
        vf = jnp.concatenate([v_buf.astype(v.dtype), v], axis=1)
        n_past = k_buf.shape[1]
        for window, dil in DILATED_BRANCHES:
            o, lse = _dilated_branch_sample(qs, kf, vf, n_past, window, dil, slopes)
            outs.append(o)
            lses.append(lse)
    w = jax.nn.softmax(jnp.stack(lses, axis=0), axis=0)
    o = jnp.sum(w[..., None] * jnp.stack(outs, axis=0), axis=0)
    return o.astype(q.dtype)


def _gated_recurrence(q, k, v, logf, s0, chunk):
    B, T, H, dk = q.shape
    dv = v.shape[-1]
    nc = T // chunk

    def to_chunks(t):
        return t.reshape(B, nc, chunk, H, t.shape[-1]).transpose(1, 0, 3, 2, 4)

    causal = jnp.tril(jnp.ones((chunk, chunk), dtype=bool))

    def step(S, inp):
        qc, kc, vc, gc = inp
        b = jnp.cumsum(gc, axis=2)
        diff = b[:, :, :, None, :] - b[:, :, None, :, :]
        decay = jnp.exp(jnp.where(causal[:, :, None], diff, NEG_BIG))
        a = jnp.einsum("bhtd,bhsd,bhtsd->bhts", qc, kc, decay)
        o = jnp.einsum("bhts,bhsv->bhtv", a, vc) + jnp.einsum("bhtd,bhdv->bhtv", qc * jnp.exp(b), S)
        b_last = b[:, :, -1:, :]
        S = jnp.exp(b_last[:, :, 0, :])[..., None] * S + jnp.einsum(
            "bhsd,bhsv->bhdv", kc * jnp.exp(b_last - b), vc)
        return S, o

    s_fin, o = lax.scan(step, s0, (to_chunks(q), to_chunks(k), to_chunks(v), to_chunks(logf)))
    return o.transpose(1, 0, 3, 2, 4).reshape(B, T, H, dv), s_fin


def _hgrn2_group(zq, zf, zi, zg, s0, lb, norm_g):
    B, T, _ = zq.shape
    q = jax.nn.silu(zq.astype(F32)).reshape(B, T, HG_HEADS, HG_KDIM)
    lbv = lb.astype(F32)
    f = lbv + (1.0 - lbv) * jax.nn.sigmoid(zf.astype(F32))
    logf = jnp.log(f).reshape(B, T, HG_HEADS, HG_KDIM)
    k = (1.0 - f).reshape(B, T, HG_HEADS, HG_KDIM)
    v = zi.astype(F32).reshape(B, T, HG_HEADS, HG_VDIM)
    chunk = HG_CHUNK if T % HG_CHUNK == 0 else T
    o, s_fin = _gated_recurrence(q, k, v, logf, s0.astype(F32), chunk)
    o = _rmsnorm(o, norm_g) * jax.nn.silu(zg.astype(F32)).reshape(B, T, HG_HEADS, HG_VDIM)
    return o.reshape(B, T, HG_DIM).astype(zq.dtype), s_fin


def _layer(x, conv_state, k_buf, v_buf, hg_state, lb, slopes,
           ln1, wg1, wu1, wd1, lnm, wi, dww, dwb, clg, clb, hgn, wo, ln2, wg2, wu2, wd2):
    B, T, _ = x.shape
    x = x + 0.5 * _swiglu(_rmsnorm(x, ln1), wg1, wu1, wd1)
    z = _rmsnorm(x, lnm) @ wi
    z_conv, zq, zk, zv, hq, hf, hi, hg = jnp.split(z, SPLITS, axis=-1)
    y_conv, conv_new = _conv_group(z_conv, conv_state, dww, dwb, clg, clb)
    q = zq.reshape(B, T, ATT_HEADS, ATT_HDIM)
    k = zk.reshape(B, T, ATT_HEADS, ATT_HDIM)
    v = zv.reshape(B, T, ATT_HEADS, ATT_HDIM)
    y_att = _att_group(q, k, v, k_buf, v_buf, slopes).reshape(B, T, ATT_DIM)
    y_hg, hg_new = _hgrn2_group(hq, hf, hi, hg, hg_state, lb, hgn)
    x = x + jnp.concatenate([y_conv, y_att, y_hg], axis=-1) @ wo
    x = x + 0.5 * _swiglu(_rmsnorm(x, ln2), wg2, wu2, wd2)
    if k_buf is None:
        keep = min(ATT_MAX_WIN, T)
        k_new, v_new = k[:, -keep:], v[:, -keep:]
    else:
        k_new, v_new = k, v
    return x, conv_new, k_new, v_new, hg_new.astype(x.dtype)


def setup_inputs(seed: int = 0) -> dict:
    key = jax.random.key(seed)
    ks = jax.random.split(key, 32)
    d = D_MODEL
    win = min(ATT_MAX_WIN, PAST_LEN)

    def nrm(k, shape, scale):
        return scale * jax.random.normal(k, shape, F32)

    return {
        "x_prompt": nrm(ks[0], (BATCH, SEQ, d), 1.0),
        "x_sample": nrm(ks[1], (DEC_BATCH, DEC_SEQ, d), 1.0),
        "state_conv": nrm(ks[2], (DEPTH, DEC_BATCH, CONV_WIDTH - 1, CONV_DIM), 0.5),
        "cache_k_win": nrm(ks[3], (DEPTH, DEC_BATCH, win, ATT_HEADS, ATT_HDIM), 1.0),
        "cache_v_win": nrm(ks[4], (DEPTH, DEC_BATCH, win, ATT_HEADS, ATT_HDIM), 1.0),
        "state_hgrn": nrm(ks[5], (DEPTH, DEC_BATCH, HG_HEADS, HG_KDIM, HG_VDIM), 0.5),
        "ln_ffn1": 1.0 + nrm(ks[6], (DEPTH, d), 0.01),
        "w_ffn1_gate": nrm(ks[7], (DEPTH, d, FFN_DIM), d ** -0.5),
        "w_ffn1_up": nrm(ks[8], (DEPTH, d, FFN_DIM), d ** -0.5),
        "w_ffn1_down": nrm(ks[9], (DEPTH, FFN_DIM, d), FFN_DIM ** -0.5),
        "ln_mix": 1.0 + nrm(ks[10], (DEPTH, d), 0.01),
        "w_in": nrm(ks[11], (DEPTH, d, N_IN), d ** -0.5),
        "conv_dw_w": nrm(ks[12], (DEPTH, CONV_WIDTH, CONV_DIM), CONV_WIDTH ** -0.5),
        "conv_dw_b": nrm(ks[13], (DEPTH, CONV_DIM), 0.01),
        "conv_ln_g": 1.0 + nrm(ks[14], (DEPTH, CONV_DIM), 0.01),
        "conv_ln_b": nrm(ks[15], (DEPTH, CONV_DIM), 0.01),
        "hg_lower_bounds": nrm(ks[16], (DEPTH, HG_DIM), 0.1),
        "hg_norm_g": 1.0 + nrm(ks[17], (DEPTH, HG_VDIM), 0.01),
        "w_out": nrm(ks[18], (DEPTH, d, d), d ** -0.5),
        "ln_ffn2": 1.0 + nrm(ks[19], (DEPTH, d), 0.01),
        "w_ffn2_gate": nrm(ks[20], (DEPTH, d, FFN_DIM), d ** -0.5),
        "w_ffn2_up": nrm(ks[21], (DEPTH, d, FFN_DIM), d ** -0.5),
        "w_ffn2_down": nrm(ks[22], (DEPTH, FFN_DIM, d), FFN_DIM ** -0.5),
        "ln_final": 1.0 + nrm(ks[23], (d,), 0.01),
    }


def reference(x_prompt, x_sample, state_conv, cache_k_win, cache_v_win, state_hgrn,
              ln_ffn1, w_ffn1_gate, w_ffn1_up, w_ffn1_down,
              ln_mix, w_in, conv_dw_w, conv_dw_b, conv_ln_g, conv_ln_b,
              hg_lower_bounds, hg_norm_g, w_out,
              ln_ffn2, w_ffn2_gate, w_ffn2_up, w_ffn2_down, ln_final):
    slopes = _alibi_slopes()
    sm = jax.nn.softmax(hg_lower_bounds.astype(F32), axis=0)
    lower = jnp.cumsum(sm, axis=0) - sm[0]
    Bp = x_prompt.shape[0]
    hp, hs = x_prompt, x_sample
    cp_l, cs_l, kp_l, vp_l, ks_l, vs_l, sp_l, ss_l = [], [], [], [], [], [], [], []
    for l in range(DEPTH):
        wl = (ln_ffn1[l], w_ffn1_gate[l], w_ffn1_up[l], w_ffn1_down[l],
              ln_mix[l], w_in[l], conv_dw_w[l], conv_dw_b[l], conv_ln_g[l], conv_ln_b[l],
              hg_norm_g[l], w_out[l], ln_ffn2[l], w_ffn2_gate[l], w_ffn2_up[l], w_ffn2_down[l])
        conv0 = jnp.zeros((Bp, CONV_WIDTH - 1, CONV_DIM), hp.dtype)
        hg0 = jnp.zeros((Bp, HG_HEADS, HG_KDIM, HG_VDIM), F32)
        hp, cp, kp, vp, sp = _layer(hp, conv0, None, None, hg0, lower[l], slopes, *wl)
        hs, cs, ksn, vsn, ss = _layer(hs, state_conv[l], cache_k_win[l], cache_v_win[l],
                                      state_hgrn[l], lower[l], slopes, *wl)
        cp_l.append(cp); cs_l.append(cs); kp_l.append(kp); vp_l.append(vp)
        ks_l.append(ksn); vs_l.append(vsn); sp_l.append(sp); ss_l.append(ss)
    y_prompt = _rmsnorm(hp, ln_final)
    y_sample = _rmsnorm(hs, ln_final)
    return (y_prompt, y_sample,
            jnp.stack(cp_l), jnp.stack(cs_l),
            jnp.stack(kp_l), jnp.stack(vp_l),
            jnp.stack(ks_l), jnp.stack(vs_l),
            jnp.stack(sp_l), jnp.stack(ss_l))
```

```python
import functools
import math

import numpy as np
import jax
import jax.numpy as jnp
from jax import lax
from jax.experimental import pallas as pl
from jax.experimental.pallas import tpu as pltpu

F32 = jnp.float32
BF16 = jnp.bfloat16

EPS = 1e-6
NEG_BIG = -1e30
CONV_WIDTH = 31
CONV_HALO = 32
ATT_HEADS = 4
ATT_HDIM = 64
ATT_BLK = 128
DILATIONS = (1, 4, 16)
ATT_SPAN = 2048
HG_HEADS = 4
LANES = 128
SUBLANES = 8
VMEM_LIMIT = 56 * 1024 * 1024


def _params(sem, vmem=VMEM_LIMIT):
    return pltpu.CompilerParams(dimension_semantics=sem, vmem_limit_bytes=vmem)


def _const_spec(shape, index):
    return pl.BlockSpec(shape, index, pipeline_mode=pl.Buffered(1))


def _rms(x, g):
    ms = jnp.mean(x * x, axis=-1, keepdims=True)
    return x * lax.rsqrt(ms + EPS) * g


def _silu(x):
    return x * jax.nn.sigmoid(x)


def _dot(a, b):
    return jnp.dot(a, b, preferred_element_type=F32)


def _dot_nt(a, b):
    return lax.dot_general(a, b, (((1,), (1,)), ((), ())), preferred_element_type=F32)


def _dot_tn(a, b):
    return lax.dot_general(a, b, (((0,), (0,)), ((), ())), preferred_element_type=F32)


def _swiglu(h, wg_ref, wu_ref, wd_ref, f_chunk):
    ffn = wg_ref.shape[1]
    acc = None
    for c0 in range(0, ffn, f_chunk):
        g = _dot(h, wg_ref[:, c0:c0 + f_chunk])
        u = _dot(h, wu_ref[:, c0:c0 + f_chunk])
        a = (_silu(g) * u).astype(BF16)
        y = _dot(a, wd_ref[c0:c0 + f_chunk, :])
        acc = y if acc is None else acc + y
    return acc


def _ffn_kernel(x_ref, ln_ref, wg_ref, wu_ref, wd_ref, o_ref, *, f_chunk):
    x = x_ref[...]
    h = _rms(x, ln_ref[...]).astype(BF16)
    o_ref[...] = x + 0.5 * _swiglu(h, wg_ref, wu_ref, wd_ref, f_chunk)


def _ffn_half_step(x, layer, ln, wg, wu, wd, *, tm, f_chunk):
    n, d = x.shape
    ffn = wg.shape[2]
    row = pl.BlockSpec((tm, d), lambda i: (i, 0))
    return pl.pallas_call(
        functools.partial(_ffn_kernel, f_chunk=f_chunk),
        grid=(n // tm,),
        in_specs=[row, pl.BlockSpec((None, 1, d), lambda i: (layer, 0, 0)),
                  _const_spec((None, d, ffn), lambda i: (layer, 0, 0)),
                  _const_spec((None, d, ffn), lambda i: (layer, 0, 0)),
                  _const_spec((None, ffn, d), lambda i: (layer, 0, 0))],
        out_specs=row,
        out_shape=jax.ShapeDtypeStruct((n, d), F32),
        compiler_params=_params(("parallel",)),
        name="ffn_half_step",
    )(x, ln, wg, wu, wd)


def _proj_in_kernel(x_ref, ln_ref, wi_ref, zc_ref, q_ref, k_ref, v_ref, zh_ref):
    h = _rms(x_ref[...], ln_ref[...]).astype(BF16)
    c0 = 0
    for ref in (zc_ref, q_ref, k_ref, v_ref, zh_ref):
        w = ref.shape[1]
        ref[...] = _dot(h, wi_ref[:, c0:c0 + w])
        c0 += w


def _proj_in(x, layer, ln, wi, *, tm, widths):
    n, d = x.shape
    n_in = wi.shape[2]
    row = lambda w: pl.BlockSpec((tm, w), lambda i: (i, 0))
    return pl.pallas_call(
        _proj_in_kernel,
        grid=(n // tm,),
        in_specs=[row(d), pl.BlockSpec((None, 1, d), lambda i: (layer, 0, 0)),
                  _const_spec((None, d, n_in), lambda i: (layer, 0, 0))],
        out_specs=[row(w) for w in widths],
        out_shape=[jax.ShapeDtypeStruct((n, w), F32) for w in widths],
        compiler_params=_params(("parallel",)),
        name="proj_in",
    )(x, ln, wi)


def _dense_in(x, layer, ln1, wg, wu, wd, lnm, wi, *, tm, f_chunk, widths):
    x1 = _ffn_half_step(x, layer, ln1, wg, wu, wd, tm=tm, f_chunk=f_chunk)
    return (x1,) + tuple(_proj_in(x1, layer, lnm, wi, tm=tm, widths=widths))


def _dense_out_kernel(x_ref, yc_ref, ya_ref, yh_ref, wo_ref, ln2_ref, wg_ref, wu_ref, wd_ref, lnf_ref,
                      o_ref, *, f_chunk, final):
    y = jnp.concatenate([yc_ref[...], ya_ref[...], yh_ref[...]], axis=-1)
    x2 = x_ref[...] + _dot(y, wo_ref[...])
    h = _rms(x2, ln2_ref[...]).astype(BF16)
    x3 = x2 + 0.5 * _swiglu(h, wg_ref, wu_ref, wd_ref, f_chunk)
    if final:
        x3 = _rms(x3, lnf_ref[...])
    o_ref[...] = x3


def _dense_out(x1, yc, ya, yh, layer, wo, ln2, wg, wu, wd, lnf, *, tm, f_chunk, final):
    n, d = x1.shape
    ffn = wg.shape[2]
    row = lambda w: pl.BlockSpec((tm, w), lambda i: (i, 0))
    vec = pl.BlockSpec((None, 1, d), lambda i: (layer, 0, 0))
    return pl.pallas_call(
        functools.partial(_dense_out_kernel, f_chunk=f_chunk, final=final),
        grid=(n // tm,),
        in_specs=[row(d), row(yc.shape[1]), row(ya.shape[1]), row(yh.shape[1]),
                  _const_spec((None, d, d), lambda i: (layer, 0, 0)),
                  vec,
                  _const_spec((None, d, ffn), lambda i: (layer, 0, 0)),
                  _const_spec((None, d, ffn), lambda i: (layer, 0, 0)),
                  _const_spec((None, ffn, d), lambda i: (layer, 0, 0)),
                  pl.BlockSpec((1, d), lambda i: (0, 0))],
        out_specs=row(d),
        out_shape=jax.ShapeDtypeStruct((n, d), F32),
        compiler_params=_params(("parallel",)),
        name="dense_out",
    )(x1, yc, ya, yh, wo, ln2, wg, wu, wd, lnf)


def _conv_kernel(zc_ref, halo_ref, dww_ref, dwb_ref, lng_ref, lnb_ref, y_ref, tail_ref, ext_ref, *, tt, tail):
    c = y_ref.shape[-1]

    @pl.when(pl.program_id(1) == 0)
    def _():
        ext_ref[0:CONV_HALO, :] = halo_ref[...]

    z = zc_ref[...]
    u = z[:, :c] * jax.nn.sigmoid(z[:, c:])
    ext_ref[CONV_HALO:CONV_HALO + tt, :] = u
    base = CONV_HALO - (CONV_WIDTH - 1)
    acc = None
    for w in range(CONV_WIDTH):
        term = ext_ref[base + w:base + w + tt, :] * dww_ref[w:w + 1, :]
        acc = term if acc is None else acc + term
    y = acc + dwb_ref[...]
    yc = y - jnp.mean(y, axis=-1, keepdims=True)
    yn = yc * lax.rsqrt(jnp.mean(yc * yc, axis=-1, keepdims=True) + EPS)
    y_ref[...] = _silu(yn * lng_ref[...] + lnb_ref[...]).astype(y_ref.dtype)
    tail_ref[...] = u[tt - tail:, :]
    carry = ext_ref[tt:tt + CONV_HALO, :]
    ext_ref[0:CONV_HALO, :] = carry


def _conv_group(zc, halo, layer, dww, dwb, lng, lnb, *, tt):
    b, t, c2 = zc.shape
    c = c2 // 2
    tail = min(CONV_HALO, tt)
    vec = pl.BlockSpec((None, 1, c), lambda i, j: (layer, 0, 0))
    return pl.pallas_call(
        functools.partial(_conv_kernel, tt=tt, tail=tail),
        grid=(b, t // tt),
        in_specs=[pl.BlockSpec((None, tt, c2), lambda i, j: (i, j, 0)),
                  pl.BlockSpec((None, CONV_HALO, c), lambda i, j: (i, 0, 0)),
                  pl.BlockSpec((None, CONV_HALO, c), lambda i, j: (layer, 0, 0)),
                  vec, vec, vec],
        out_specs=[pl.BlockSpec((None, tt, c), lambda i, j: (i, j, 0)),
                   pl.BlockSpec((None, tail, c), lambda i, j: (i, 0, 0))],
        out_shape=[jax.ShapeDtypeStruct((b, t, c), BF16),
                   jax.ShapeDtypeStruct((b, tail, c), F32)],
        scratch_shapes=[pltpu.VMEM((CONV_HALO + tt, c), F32)],
        compiler_params=_params(("parallel", "arbitrary")),
        name="conv_group",
    )(zc, halo, dww, dwb, lng, lnb)


def _alibi_slopes():
    return [2.0 ** (-8.0 * (h + 1) / ATT_HEADS) for h in range(ATT_HEADS)]


def _prompt_bias_tables():
    qi = np.arange(ATT_BLK)[:, None]
    ki = np.arange(2 * ATT_BLK)[None, :]
    j = qi - ki + ATT_BLK
    tabs = []
    for dil in DILATIONS:
        for first in (False, True):
            valid = (j >= 0) & (j <= ATT_BLK) & ((not first) | (ki >= ATT_BLK))
            rows = []
            for s in _alibi_slopes():
                bias = (-np.float32(s)) * (j * dil).astype(np.float32)
                rows.append(np.where(valid, bias, np.float32(NEG_BIG)).astype(np.float32))
            tabs.append(np.concatenate(rows, axis=0))
    return np.stack(tabs, axis=0)


def _head_pair_split(x, lo_mask):
    zero = jnp.zeros_like(x)
    return jnp.concatenate([jnp.where(lo_mask, x, zero), jnp.where(lo_mask, zero, x)], axis=0)


def _attn_prompt_kernel(bias_ref, q_ref, kp_ref, kc_ref, vp_ref, vc_ref, o_ref,
                        qs_ref, ks_ref, vs_ref, os_ref, ls_ref, *, span):
    j = pl.program_id(1)
    nslab = qs_ref.shape[0]
    scale = ATT_HDIM ** -0.5
    for sl in range(nslab):
        cols = slice(sl * LANES, (sl + 1) * LANES)
        qs_ref[sl] = q_ref[:, cols] * scale
        ks_ref[sl, 0:span, :] = kp_ref[:, cols]
        ks_ref[sl, span:2 * span, :] = kc_ref[:, cols]
        vs_ref[sl, 0:span, :] = vp_ref[:, cols]
        vs_ref[sl, span:2 * span, :] = vc_ref[:, cols]

    lane = lax.broadcasted_iota(jnp.int32, (ATT_BLK, LANES), 1)
    lo_mask = lane < ATT_HDIM
    n_blocks = span // ATT_BLK

    for br, dil in enumerate(DILATIONS):
        def body(i, carry, br=br, dil=dil):
            r = i % dil
            n = i // dil
            q0 = n * (ATT_BLK * dil) + r
            k0 = span + q0 - ATT_BLK * dil
            first = jnp.logical_and(j == 0, n == 0).astype(jnp.int32)
            bias = bias_ref[2 * br + first]

            def rows(ref, sl, start, size):
                if dil == 1:
                    return ref[sl, pl.ds(pl.multiple_of(start, ATT_BLK), size), :]
                return ref[sl, pl.ds(start, size, stride=dil), :]

            s_parts, v_parts = [], []
            for sl in range(nslab):
                ql = rows(qs_ref, sl, q0, ATT_BLK)
                kl = rows(ks_ref, sl, k0, 2 * ATT_BLK)
                v_parts.append(rows(vs_ref, sl, k0, 2 * ATT_BLK).astype(BF16))
                lhs = _head_pair_split(ql, lo_mask).astype(BF16)
                s_parts.append(_dot_nt(lhs, kl.astype(BF16)))
            s = jnp.concatenate(s_parts, axis=0)
            s = jnp.where(bias > 0.5 * NEG_BIG, s + bias, NEG_BIG)
            m = jnp.max(s, axis=-1, keepdims=True)
            p = jnp.exp(s - m)
            l = jnp.sum(p, axis=-1, keepdims=True)
            pv = _dot(p.astype(BF16), jnp.concatenate(v_parts, axis=1))
            o_all = pv / l
            lse_all = m + jnp.log(l)
            for sl in range(nslab):
                cols = slice(sl * LANES, (sl + 1) * LANES)
                ra = slice((2 * sl) * ATT_BLK, (2 * sl + 1) * ATT_BLK)
                rb = slice((2 * sl + 1) * ATT_BLK, (2 * sl + 2) * ATT_BLK)
                o_new = jnp.where(lo_mask, o_all[ra, cols], o_all[rb, cols])
                lse_new = jnp.where(lo_mask, jnp.broadcast_to(lse_all[ra], (ATT_BLK, LANES)),
                                    jnp.broadcast_to(lse_all[rb], (ATT_BLK, LANES)))
                if br == 0:
                    o_tot, lse_tot = o_new, lse_new
                else:
                    o_old = rows(os_ref, sl, q0, ATT_BLK)
                    lse_old = rows(ls_ref, sl, q0, ATT_BLK)
                    mx = jnp.maximum(lse_old, lse_new)
                    e_old = jnp.exp(lse_old - mx)
                    e_new = jnp.exp(lse_new - mx)
                    den = e_old + e_new
                    o_tot = (e_old * o_old + e_new * o_new) / den
                    lse_tot = mx + jnp.log(den)
                if dil == 1:
                    os_ref[sl, pl.ds(pl.multiple_of(q0, ATT_BLK), ATT_BLK), :] = o_tot
                    ls_ref[sl, pl.ds(pl.multiple_of(q0, ATT_BLK), ATT_BLK), :] = lse_tot
                else:
                    os_ref[sl, pl.ds(q0, ATT_BLK, stride=dil), :] = o_tot
                    ls_ref[sl, pl.ds(q0, ATT_BLK, stride=dil), :] = lse_tot
            return carry

        lax.fori_loop(0, n_blocks, body, 0)

    o_ref[...] = jnp.concatenate([os_ref[sl] for sl in range(nslab)], axis=-1).astype(o_ref.dtype)


def _attn_prompt(q, k, v, bias_tabs, *, span):
    b, t, w = q.shape
    nslab = w // LANES
    cur = pl.BlockSpec((None, span, w), lambda i, j: (i, j, 0))
    prev = pl.BlockSpec((None, span, w), lambda i, j: (i, jnp.maximum(j - 1, 0), 0))
    slab = lambda rows: pltpu.VMEM((nslab, rows, LANES), F32)
    return pl.pallas_call(
        functools.partial(_attn_prompt_kernel, span=span),
        grid=(b, t // span),
        in_specs=[_const_spec(bias_tabs.shape, lambda i, j: (0, 0, 0)), cur, prev, cur, prev, cur],
        out_specs=cur,
        out_shape=jax.ShapeDtypeStruct((b, t, w), BF16),
        scratch_shapes=[slab(span), slab(2 * span), slab(2 * span), slab(span), slab(span)],
        compiler_params=_params(("parallel", "arbitrary")),
        name="attn_prompt",
    )(bias_tabs, q, k, k, v, v)


def _sample_bias_tables(n_past, t_new, t_pad):
    slopes = _alibi_slopes()
    n_keys = n_past + t_pad
    tabs = np.full((len(DILATIONS), ATT_HEADS * t_pad, n_keys), NEG_BIG, np.float32)
    for br, dil in enumerate(DILATIONS):
        for h, s in enumerate(slopes):
            for t in range(t_pad):
                tq = min(t, t_new - 1)
                for jj in range(ATT_BLK + 1):
                    idx = n_past + tq - jj * dil
                    if idx < 0:
                        continue
                    col = idx if idx < n_past else n_past + (idx - n_past)
                    tabs[br, h * t_pad + t, col] = -np.float32(s) * np.float32(jj * dil)
    return tabs[:, :, :n_past], tabs[:, :, n_past:]


def _attn_sample_kernel(bc_ref, bn_ref, q_ref, kn_ref, vn_ref, kc_ref, vc_ref, o_ref):
    t_pad, w = q_ref.shape
    scale = ATT_HDIM ** -0.5
    q = q_ref[...] * scale
    lane = lax.broadcasted_iota(jnp.int32, (t_pad, w), 1)
    head_of_lane = lane // ATT_HDIM
    zero = jnp.zeros_like(q)
    lhs = jnp.concatenate([jnp.where(head_of_lane == h, q, zero) for h in range(ATT_HEADS)], axis=0).astype(BF16)
    s_c = _dot_nt(lhs, kc_ref[...].astype(BF16))
    s_n = _dot_nt(lhs, kn_ref[...].astype(BF16))
    vc = vc_ref[...].astype(BF16)
    vn = vn_ref[...].astype(BF16)
    outs, lses = [], []
    for br in range(len(DILATIONS)):
        bc = bc_ref[br]
        bn = bn_ref[br]
        sc = jnp.where(bc > 0.5 * NEG_BIG, s_c + bc, NEG_BIG)
        sn = jnp.where(bn > 0.5 * NEG_BIG, s_n + bn, NEG_BIG)
        m = jnp.maximum(jnp.max(sc, axis=-1, keepdims=True), jnp.max(sn, axis=-1, keepdims=True))
        pc = jnp.exp(sc - m)
        pn = jnp.exp(sn - m)
        l = jnp.sum(pc, axis=-1, keepdims=True) + jnp.sum(pn, axis=-1, keepdims=True)
        o = (_dot(pc.astype(BF16), vc) + _dot(pn.astype(BF16), vn)) / l
        outs.append(o)
        lses.append(m + jnp.log(l))
    mx = functools.reduce(jnp.maximum, lses)
    es = [jnp.exp(x - mx) for x in lses]
    den = functools.reduce(lambda a, b: a + b, es)
    o = functools.reduce(lambda a, b: a + b, [e * x for e, x in zip(es, outs)]) / den
    res = zero
    for h in range(ATT_HEADS):
        res = jnp.where(head_of_lane == h, o[h * t_pad:(h + 1) * t_pad, :], res)
    o_ref[...] = res.astype(o_ref.dtype)


def _attn_sample(q, k, v, cache_k, cache_v, layer, bias_c, bias_n):
    b, t_pad, w = q.shape
    n_past = cache_k.shape[2]
    new = pl.BlockSpec((None, t_pad, w), lambda i: (i, 0, 0))
    cache = pl.BlockSpec((None, None, n_past, w), lambda i: (layer, i, 0, 0))
    return pl.pallas_call(
        _attn_sample_kernel,
        grid=(b,),
        in_specs=[_const_spec(bias_c.shape, lambda i: (0, 0, 0)),
                  _const_spec(bias_n.shape, lambda i: (0, 0, 0)),
                  new, new, new, cache, cache],
        out_specs=new,
        out_shape=jax.ShapeDtypeStruct((b, t_pad, w), BF16),
        compiler_params=_params(("parallel",)),
        name="attn_sample",
    )(bias_c, bias_n, q, k, v, cache_k, cache_v)


def _hgrn_tables(chunk):
    r = np.arange(chunk)[:, None]
    c = np.arange(chunk)[None, :]
    x = np.bitwise_xor(r, c)
    hb = np.zeros_like(x)
    for bit in range(1, 16):
        hb = np.where(x >> bit > 0, bit, hb)
    lev = np.where(c > r, -1, np.where(r // SUBLANES == c // SUBLANES, 0, hb)).astype(np.int32)
    sel = np.zeros((SUBLANES * LANES, chunk), np.float32)
    for s in range(SUBLANES):
        sel[s * LANES:(s + 1) * LANES, s::SUBLANES] = 1.0
    tri = (c <= r).astype(np.float32)
    return lev, sel, tri


def _split3(x):
    hi = x.astype(BF16)
    r1 = x - hi.astype(F32)
    mid = r1.astype(BF16)
    lo = (r1 - mid.astype(F32)).astype(BF16)
    return hi, mid, lo


def _group_row(x, s):
    c, l = x.shape
    x3 = x.reshape(c // SUBLANES, SUBLANES, l)
    return jnp.broadcast_to(x3[:, s:s + 1, :], x3.shape).reshape(c, l)


def _block_last(x, m):
    c, l = x.shape
    x3 = x.reshape(c // m, m, l)
    return jnp.broadcast_to(x3[:, m - 1:m, :], x3.shape).reshape(c, l)


def _hgrn_kernel(*refs, chunk, n_chunks, layer, t_valid, has_state):
    if has_state:
        (zq_ref, zf_ref, zi_ref, zg_ref, lb_ref, ng_ref, lev_ref, sel_ref, tri_ref, s0_ref,
         o_ref, sf_ref, st_ref) = refs
    else:
        (zq_ref, zf_ref, zi_ref, zg_ref, lb_ref, ng_ref, lev_ref, sel_ref, tri_ref,
         o_ref, sf_ref, st_ref) = refs
        s0_ref = None
    tstep = pl.program_id(2)

    @pl.when(tstep == 0)
    def _():
        if has_state:
            st_ref[...] = s0_ref[...].T
        else:
            st_ref[...] = jnp.zeros_like(st_ref)

    raw = lb_ref[...]
    e = jnp.exp(raw - jnp.max(raw, axis=0, keepdims=True))
    sm = e / jnp.sum(e, axis=0, keepdims=True)
    lb = jnp.zeros_like(sm[0:1])
    for i in range(1, layer + 1):
        lb = lb + sm[i:i + 1]

    lev = lev_ref[...]
    tri = tri_ref[...]
    n_levels = int(math.log2(chunk))

    def one_chunk(ci, carry):
        c0 = pl.multiple_of(ci * chunk, chunk)
        rows = pl.ds(c0, chunk)
        q = _silu(zq_ref[rows, :])
        f = lb + (1.0 - lb) * jax.nn.sigmoid(zf_ref[rows, :])
        g = jnp.log(f)
        k = 1.0 - f
        v = zi_ref[rows, :]
        if t_valid is not None:
            live = lax.broadcasted_iota(jnp.int32, g.shape, 0) + c0 < t_valid
            g = jnp.where(live, g, 0.0)
            k = jnp.where(live, k, 0.0)
        b = _dot(tri, jnp.concatenate(_split3(g), axis=1))
        dk = g.shape[1]
        b = b[:, :dk] + b[:, dk:2 * dk] + b[:, 2 * dk:]

        vals = []
        for s in range(SUBLANES):
            decay = jnp.exp(jnp.minimum(b - _group_row(b, s), 0.0))
            vals.append((q * decay * _group_row(k, s)).astype(BF16))
        a = _dot(jnp.concatenate(vals, axis=1), sel_ref[...])
        a = jnp.where(lev == 0, a, 0.0)
        for bit in range(3, n_levels):
            m = 1 << bit
            b_end = _block_last(b, m)
            b_prev = jnp.concatenate([jnp.zeros((m, dk), F32), b_end[:chunk - m]], axis=0)
            qd = (q * jnp.exp(jnp.minimum(b - b_prev, 0.0))).astype(BF16)
            kd = (k * jnp.exp(b_end - b)).astype(BF16)
            a = jnp.where(lev == bit, _dot_nt(qd, kd), a)

        st = st_ref[...]
        o = _dot(a.astype(BF16), v.astype(BF16)) + _dot_nt((q * jnp.exp(b)).astype(BF16), st.astype(BF16))
        b_last = b[chunk - 1:chunk, :]
        kd = (k * jnp.exp(b_last - b)).astype(BF16)
        st_ref[...] = st * jnp.exp(b_last) + _dot_tn(v.astype(BF16), kd)

        on = o * lax.rsqrt(jnp.mean(o * o, axis=-1, keepdims=True) + EPS) * ng_ref[...]
        o_ref[rows, :] = (on * _silu(zg_ref[rows, :])).astype(o_ref.dtype)
        return carry

    lax.fori_loop(0, n_chunks, one_chunk, 0)

    @pl.when(tstep == pl.num_programs(2) - 1)
    def _():
        sf_ref[...] = st_ref[...].T


def _hgrn_group(zh, lb_raw, norm_g, s0, layer, tables, *, tblk, chunk, t_valid=None):
    b, t, w4 = zh.shape
    hd = w4 // 4
    dk = hd // HG_HEADS
    depth = lb_raw.shape[0]
    lev, sel, tri = tables
    col = lambda part: pl.BlockSpec((None, tblk, dk), lambda i, h, j: (i, j, part * HG_HEADS + h))
    in_specs = [col(0), col(1), col(2), col(3),
                pl.BlockSpec((depth, dk), lambda i, h, j: (0, h)),
                pl.BlockSpec((None, 1, dk), lambda i, h, j: (layer, 0, 0)),
                _const_spec(lev.shape, lambda i, h, j: (0, 0)),
                _const_spec(sel.shape, lambda i, h, j: (0, 0)),
                _const_spec(tri.shape, lambda i, h, j: (0, 0))]
    args = [zh, zh, zh, zh, lb_raw, norm_g, lev, sel, tri]
    if s0 is not None:
        in_specs.append(pl.BlockSpec((None, None, None, dk, dk), lambda i, h, j: (layer, i, h, 0, 0)))
        args.append(s0)
    return pl.pallas_call(
        functools.partial(_hgrn_kernel, chunk=chunk, n_chunks=tblk // chunk, layer=layer,
                          t_valid=t_valid, has_state=s0 is not None),
        grid=(b, HG_HEADS, t // tblk),
        in_specs=in_specs,
        out_specs=[pl.BlockSpec((None, tblk, dk), lambda i, h, j: (i, j, h)),
                   pl.BlockSpec((None, None, dk, dk), lambda i, h, j: (i, h, 0, 0))],
        out_shape=[jax.ShapeDtypeStruct((b, t, hd), BF16),
                   jax.ShapeDtypeStruct((b, HG_HEADS, dk, dk), F32)],
        scratch_shapes=[pltpu.VMEM((dk, dk), F32)],
        compiler_params=_params(("parallel", "parallel", "arbitrary")),
        name="hgrn_group",
    )(*args)


def _pad_rows(x, rows):
    return jnp.pad(x, ((0, 0), (0, rows - x.shape[1]), (0, 0)))


def kernel(x_prompt, x_sample, state_conv, cache_k_win, cache_v_win, state_hgrn, ln_ffn1, w_ffn1_gate, w_ffn1_up, w_ffn1_down, ln_mix, w_in, conv_dw_w, conv_dw_b, conv_ln_g, conv_ln_b, hg_lower_bounds, hg_norm_g, w_out, ln_ffn2, w_ffn2_gate, w_ffn2_up, w_ffn2_down, ln_final):
    bp, seq, d = x_prompt.shape
    bs, t_new, _ = x_sample.shape
    depth = w_in.shape[0]
    conv_dim = conv_dw_w.shape[2]
    att_dim = ATT_HEADS * ATT_HDIM
    hg_dim = hg_lower_bounds.shape[1]
    widths = (2 * conv_dim, att_dim, att_dim, att_dim, 4 * hg_dim)
    n_past = cache_k_win.shape[2]
    t_pad = SUBLANES
    keep = min(ATT_SPAN, seq)

    wg1, wu1, wd1 = (w.astype(BF16) for w in (w_ffn1_gate, w_ffn1_up, w_ffn1_down))
    wg2, wu2, wd2 = (w.astype(BF16) for w in (w_ffn2_gate, w_ffn2_up, w_ffn2_down))
    wi, wo = w_in.astype(BF16), w_out.astype(BF16)
    vec3 = lambda a: a.reshape(depth, 1, a.shape[-1])
    ln1, lnm, ln2 = vec3(ln_ffn1), vec3(ln_mix), vec3(ln_ffn2)
    dwb, clg, clb, hgn = vec3(conv_dw_b), vec3(conv_ln_g), vec3(conv_ln_b), vec3(hg_norm_g)
    lnf = ln_final.reshape(1, d)
    dww = jnp.pad(conv_dw_w, ((0, 0), (0, CONV_HALO - CONV_WIDTH), (0, 0)))
    cache_k = cache_k_win.reshape(depth, bs, n_past, att_dim)
    cache_v = cache_v_win.reshape(depth, bs, n_past, att_dim)
    halo_s = jnp.pad(state_conv, ((0, 0), (0, 0), (CONV_HALO - (CONV_WIDTH - 1), 0), (0, 0)))
    halo_p = jnp.zeros((bp, CONV_HALO, conv_dim), F32)

    bias_p = jnp.asarray(_prompt_bias_tables())
    bias_c, bias_n = (jnp.asarray(a) for a in _sample_bias_tables(n_past, t_new, t_pad))
    chunk_p = 256
    tabs_p = tuple(jnp.asarray(a, dt) for a, dt in zip(_hgrn_tables(chunk_p), (jnp.int32, BF16, BF16)))
    tabs_s = tuple(jnp.asarray(a, dt) for a, dt in zip(_hgrn_tables(t_pad), (jnp.int32, BF16, BF16)))

    hp = x_prompt.reshape(bp * seq, d)
    hs = x_sample.reshape(bs * t_new, d)
    outs = [[] for _ in range(8)]
    for l in range(depth):
        final = l == depth - 1
        x1, zc, q, k, v, zh = _dense_in(hp, l, ln1, wg1, wu1, wd1, lnm, wi, tm=512, f_chunk=1408, widths=widths)
        seq3 = lambda a: a.reshape(bp, seq, a.shape[-1])
        yc, tail = _conv_group(seq3(zc), halo_p, l, dww, dwb, clg, clb, tt=512)
        ya = _attn_prompt(seq3(q), seq3(k), seq3(v), bias_p, span=ATT_SPAN)
        yh, sp = _hgrn_group(seq3(zh), hg_lower_bounds, hgn, None, l, tabs_p, tblk=1024, chunk=chunk_p)
        flat = lambda a: a.reshape(bp * seq, a.shape[-1])
        hp = _dense_out(x1, flat(yc), flat(ya), flat(yh), l, wo, ln2, wg2, wu2, wd2, lnf,
                        tm=512, f_chunk=1408, final=final)
        outs[0].append(tail[:, -(CONV_WIDTH - 1):])
        outs[2].append(seq3(k)[:, -keep:].reshape(bp, keep, ATT_HEADS, ATT_HDIM))
        outs[3].append(seq3(v)[:, -keep:].reshape(bp, keep, ATT_HEADS, ATT_HDIM))
        outs[6].append(sp)

        x1s, zcs, qs, ks, vs, zhs = _dense_in(hs, l, ln1, wg1, wu1, wd1, lnm, wi,
                                              tm=bs * t_new, f_chunk=1408, widths=widths)
        new3 = lambda a: _pad_rows(a.reshape(bs, t_new, a.shape[-1]), t_pad)
        ycs, us = _conv_group(new3(zcs), halo_s[l], l, dww, dwb, clg, clb, tt=t_pad)
        yas = _attn_sample(new3(qs), new3(ks), new3(vs), cache_k, cache_v, l, bias_c, bias_n)
        yhs, ss = _hgrn_group(new3(zhs), hg_lower_bounds, hgn, state_hgrn, l, tabs_s,
                              tblk=t_pad, chunk=t_pad, t_valid=t_new)
        unpad = lambda a: a[:, :t_new].reshape(bs * t_new, a.shape[-1])
        hs = _dense_out(x1s, unpad(ycs), unpad(yas), unpad(yhs), l, wo, ln2, wg2, wu2, wd2, lnf,
                        tm=bs * t_new, f_chunk=1408, final=final)
        outs[1].append(jnp.concatenate([state_conv[l][:, t_new:], us[:, :t_new]], axis=1))
        outs[4].append(ks.reshape(bs, t_new, ATT_HEADS, ATT_HDIM))
        outs[5].append(vs.reshape(bs, t_new, ATT_HEADS, ATT_HDIM))
        outs[7].append(ss)

    y_prompt = hp.reshape(bp, seq, d)
    y_sample = hs.reshape(bs, t_new, d)
    st = [jnp.stack(o) for o in outs]
    return (y_prompt, y_sample, st[0], st[1], st[2], st[3], st[4], st[5], st[6], st[7])
```

```python
import functools
import math

import numpy as np
import jax
import jax.numpy as jnp
from jax import lax
from jax.experimental import pallas as pl
from jax.experimental.pallas import tpu as pltpu

F32 = jnp.float32
BF16 = jnp.bfloat16

EPS = 1e-6
NEG_BIG = -1e30
CONV_WIDTH = 31
CONV_HALO = 32
ATT_HEADS = 4
ATT_HDIM = 64
ATT_BLK = 128
DILATIONS = (1, 4, 16)
ATT_SPAN = 2048
HG_HEADS = 4
LANES = 128
SUBLANES = 8
VMEM_LIMIT = 56 * 1024 * 1024


def _params(sem, vmem=VMEM_LIMIT):
    return pltpu.CompilerParams(dimension_semantics=sem, vmem_limit_bytes=vmem)


def _const_spec(shape, index):
    return pl.BlockSpec(shape, index, pipeline_mode=pl.Buffered(1))


def _rms(x, g):
    ms = jnp.mean(x * x, axis=-1, keepdims=True)
    return x * lax.rsqrt(ms + EPS) * g


def _silu(x):
    return x * jax.nn.sigmoid(x)


def _dot(a, b):
    return jnp.dot(a, b, preferred_element_type=F32)


def _dot_nt(a, b):
    return lax.dot_general(a, b, (((1,), (1,)), ((), ())), preferred_element_type=F32)


def _dot_tn(a, b):
    return lax.dot_general(a, b, (((0,), (0,)), ((), ())), preferred_element_type=F32)


def _swiglu(h, wg_ref, wu_ref, wd_ref, f_chunk):
    ffn = wg_ref.shape[1]
    acc = None
    for c0 in range(0, ffn, f_chunk):
        g = _dot(h, wg_ref[:, c0:c0 + f_chunk])
        u = _dot(h, wu_ref[:, c0:c0 + f_chunk])
        a = (_silu(g) * u).astype(BF16)
        y = _dot(a, wd_ref[c0:c0 + f_chunk, :])
        acc = y if acc is None else acc + y
    return acc


def _ffn_kernel(x_ref, ln_ref, wg_ref, wu_ref, wd_ref, o_ref, *, f_chunk):
    x = x_ref[...]
    h = _rms(x, ln_ref[...]).astype(BF16)
    o_ref[...] = x + 0.5 * _swiglu(h, wg_ref, wu_ref, wd_ref, f_chunk)


def _ffn_half_step(x, layer, ln, wg, wu, wd, *, tm, f_chunk):
    n, d = x.shape
    ffn = wg.shape[2]
    row = pl.BlockSpec((tm, d), lambda i: (i, 0))
    return pl.pallas_call(
        functools.partial(_ffn_kernel, f_chunk=f_chunk),
        grid=(n // tm,),
        in_specs=[row, pl.BlockSpec((None, 1, d), lambda i: (layer, 0, 0)),
                  _const_spec((None, d, ffn), lambda i: (layer, 0, 0)),
                  _const_spec((None, d, ffn), lambda i: (layer, 0, 0)),
                  _const_spec((None, ffn, d), lambda i: (layer, 0, 0))],
        out_specs=row,
        out_shape=jax.ShapeDtypeStruct((n, d), F32),
        compiler_params=_params(("parallel",)),
        name="ffn_half_step",
    )(x, ln, wg, wu, wd)


def _proj_in_kernel(x_ref, ln_ref, wi_ref, zc_ref, q_ref, k_ref, v_ref, zh_ref):
    h = _rms(x_ref[...], ln_ref[...]).astype(BF16)
    c0 = 0
    for ref in (zc_ref, q_ref, k_ref, v_ref, zh_ref):
        w = ref.shape[1]
        ref[...] = _dot(h, wi_ref[:, c0:c0 + w])
        c0 += w


def _proj_in(x, layer, ln, wi, *, tm, widths):
    n, d = x.shape
    n_in = wi.shape[2]
    row = lambda w: pl.BlockSpec((tm, w), lambda i: (i, 0))
    return pl.pallas_call(
        _proj_in_kernel,
        grid=(n // tm,),
        in_specs=[row(d), pl.BlockSpec((None, 1, d), lambda i: (layer, 0, 0)),
                  _const_spec((None, d, n_in), lambda i: (layer, 0, 0))],
        out_specs=[row(w) for w in widths],
        out_shape=[jax.ShapeDtypeStruct((n, w), F32) for w in widths],
        compiler_params=_params(("parallel",)),
        name="proj_in",
    )(x, ln, wi)


def _dense_in(x, layer, ln1, wg, wu, wd, lnm, wi, *, tm, f_chunk, widths):
    x1 = _ffn_half_step(x, layer, ln1, wg, wu, wd, tm=tm, f_chunk=f_chunk)
    return (x1,) + tuple(_proj_in(x1, layer, lnm, wi, tm=tm, widths=widths))


def _dense_out_kernel(x_ref, yc_ref, ya_ref, yh_ref, wo_ref, ln2_ref, wg_ref, wu_ref, wd_ref, lnf_ref,
                      o_ref, *, f_chunk, final):
    y = jnp.concatenate([r[...].astype(BF16) for r in (yc_ref, ya_ref, yh_ref)], axis=-1)
    x2 = x_ref[...] + _dot(y, wo_ref[...])
    h = _rms(x2, ln2_ref[...]).astype(BF16)
    x3 = x2 + 0.5 * _swiglu(h, wg_ref, wu_ref, wd_ref, f_chunk)
    if final:
        x3 = _rms(x3, lnf_ref[...])
    o_ref[...] = x3


def _dense_out(x1, yc, ya, yh, layer, wo, ln2, wg, wu, wd, lnf, *, tm, f_chunk, final):
    n, d = x1.shape
    ffn = wg.shape[2]
    row = lambda w: pl.BlockSpec((tm, w), lambda i: (i, 0))
    vec = pl.BlockSpec((None, 1, d), lambda i: (layer, 0, 0))
    return pl.pallas_call(
        functools.partial(_dense_out_kernel, f_chunk=f_chunk, final=final),
        grid=(n // tm,),
        in_specs=[row(d), row(yc.shape[1]), row(ya.shape[1]), row(yh.shape[1]),
                  _const_spec((None, d, d), lambda i: (layer, 0, 0)),
                  vec,
                  _const_spec((None, d, ffn), lambda i: (layer, 0, 0)),
                  _const_spec((None, d, ffn), lambda i: (layer, 0, 0)),
                  _const_spec((None, ffn, d), lambda i: (layer, 0, 0)),
                  pl.BlockSpec((1, d), lambda i: (0, 0))],
        out_specs=row(d),
        out_shape=jax.ShapeDtypeStruct((n, d), F32),
        compiler_params=_params(("parallel",)),
        name="dense_out",
    )(x1, yc, ya, yh, wo, ln2, wg, wu, wd, lnf)


def _conv_kernel(zc_ref, halo_ref, dww_ref, dwb_ref, lng_ref, lnb_ref, y_ref, tail_ref, ext_ref, *, nb, tt, tail):
    c = y_ref.shape[-1]
    lo = CONV_HALO - (CONV_WIDTH - 1)

    def one_batch(b):
        @pl.when(pl.program_id(1) == 0)
        def _():
            ext_ref[b, 0:CONV_HALO, :] = halo_ref[b]
            ext_ref[b, CONV_HALO + tt:CONV_HALO + tt + SUBLANES, :] = jnp.zeros((SUBLANES, c), F32)

        z = zc_ref[b]
        u = z[:, :c] * jax.nn.sigmoid(z[:, c:])
        ext_ref[b, CONV_HALO:CONV_HALO + tt, :] = u
        y = None
        for r in range(SUBLANES):
            z_r = None
            for o in range(r, CONV_HALO + 1, SUBLANES):
                if o < lo:
                    continue
                term = ext_ref[b, o - r:o - r + tt + SUBLANES, :] * dww_ref[o - lo:o - lo + 1, :]
                z_r = term if z_r is None else z_r + term
            part = z_r[r:r + tt, :]
            y = part if y is None else y + part
        y = y + dwb_ref[...]
        yc = y - jnp.mean(y, axis=-1, keepdims=True)
        yn = yc * lax.rsqrt(jnp.mean(yc * yc, axis=-1, keepdims=True) + EPS)
        y_ref[b] = _silu(yn * lng_ref[...] + lnb_ref[...]).astype(y_ref.dtype)
        tail_ref[b] = u[tt - tail:, :]
        carry = ext_ref[b, tt:tt + CONV_HALO, :]
        ext_ref[b, 0:CONV_HALO, :] = carry

    if nb == 1:
        one_batch(0)
    else:
        def body(b, carry):
            one_batch(b)
            return carry
        lax.fori_loop(0, nb, body, 0)


def _conv_group(zc, halo, layer, dww, dwb, lng, lnb, *, nb, tt, out_dtype):
    b, t, c2 = zc.shape
    c = c2 // 2
    tail = min(CONV_HALO, tt)
    vec = pl.BlockSpec((None, 1, c), lambda i, j: (layer, 0, 0))
    return pl.pallas_call(
        functools.partial(_conv_kernel, nb=nb, tt=tt, tail=tail),
        grid=(b // nb, t // tt),
        in_specs=[pl.BlockSpec((nb, tt, c2), lambda i, j: (i, j, 0)),
                  pl.BlockSpec((nb, CONV_HALO, c), lambda i, j: (i, 0, 0)),
                  pl.BlockSpec((None, CONV_HALO, c), lambda i, j: (layer, 0, 0)),
                  vec, vec, vec],
        out_specs=[pl.BlockSpec((nb, tt, c), lambda i, j: (i, j, 0)),
                   pl.BlockSpec((nb, tail, c), lambda i, j: (i, 0, 0))],
        out_shape=[jax.ShapeDtypeStruct((b, t, c), out_dtype),
                   jax.ShapeDtypeStruct((b, tail, c), F32)],
        scratch_shapes=[pltpu.VMEM((nb, CONV_HALO + tt + SUBLANES, c), F32)],
        compiler_params=_params(("parallel", "arbitrary")),
        name="conv_group",
    )(zc, halo, dww, dwb, lng, lnb)


def _alibi_slopes():
    return [2.0 ** (-8.0 * (h + 1) / ATT_HEADS) for h in range(ATT_HEADS)]


def _prompt_bias_tables():
    qi = np.arange(ATT_BLK)[:, None]
    ki = np.arange(2 * ATT_BLK)[None, :]
    j = qi - ki + ATT_BLK
    tabs = []
    for dil in DILATIONS:
        for first in (False, True):
            valid = (j >= 0) & (j <= ATT_BLK) & ((not first) | (ki >= ATT_BLK))
            rows = []
            for s in _alibi_slopes():
                bias = (-np.float32(s)) * (j * dil).astype(np.float32)
                rows.append(np.where(valid, bias, np.float32(NEG_BIG)).astype(np.float32))
            tabs.append(np.concatenate(rows, axis=0))
    return np.stack(tabs, axis=0)


def _head_pair_split(x, lo_mask):
    zero = jnp.zeros_like(x)
    return jnp.concatenate([jnp.where(lo_mask, x, zero), jnp.where(lo_mask, zero, x)], axis=0)


def _attn_prompt_kernel(bias_ref, q_ref, kp_ref, kc_ref, vp_ref, vc_ref, o_ref,
                        qs_ref, ks_ref, vs_ref, os_ref, ls_ref, *, span):
    j = pl.program_id(1)
    nslab = qs_ref.shape[0]
    scale = ATT_HDIM ** -0.5
    for sl in range(nslab):
        cols = slice(sl * LANES, (sl + 1) * LANES)
        qs_ref[sl] = q_ref[:, cols] * scale
        ks_ref[sl, 0:span, :] = kp_ref[:, cols]
        ks_ref[sl, span:2 * span, :] = kc_ref[:, cols]
        vs_ref[sl, 0:span, :] = vp_ref[:, cols]
        vs_ref[sl, span:2 * span, :] = vc_ref[:, cols]

    lane = lax.broadcasted_iota(jnp.int32, (ATT_BLK, LANES), 1)
    lo_mask = lane < ATT_HDIM
    n_blocks = span // ATT_BLK

    for br, dil in enumerate(DILATIONS):
        def body(i, carry, br=br, dil=dil):
            r = i % dil
            n = i // dil
            q0 = n * (ATT_BLK * dil) + r
            k0 = span + q0 - ATT_BLK * dil
            first = jnp.logical_and(j == 0, n == 0).astype(jnp.int32)
            bias = bias_ref[2 * br + first]

            def rows(ref, sl, start, size):
                if dil == 1:
                    return ref[sl, pl.ds(pl.multiple_of(start, ATT_BLK), size), :]
                return ref[sl, pl.ds(start, size, stride=dil), :]

            s_parts, v_parts = [], []
            for sl in range(nslab):
                ql = rows(qs_ref, sl, q0, ATT_BLK)
                kl = rows(ks_ref, sl, k0, 2 * ATT_BLK)
                v_parts.append(rows(vs_ref, sl, k0, 2 * ATT_BLK).astype(BF16))
                lhs = _head_pair_split(ql, lo_mask).astype(BF16)
                s_parts.append(_dot_nt(lhs, kl.astype(BF16)))
            s = jnp.concatenate(s_parts, axis=0)
            s = jnp.where(bias > 0.5 * NEG_BIG, s + bias, NEG_BIG)
            m = jnp.max(s, axis=-1, keepdims=True)
            p = jnp.exp(s - m)
            l = jnp.sum(p, axis=-1, keepdims=True)
            pv = _dot(p.astype(BF16), jnp.concatenate(v_parts, axis=1))
            lse_all = m + jnp.log(l)
            for sl in range(nslab):
                cols = slice(sl * LANES, (sl + 1) * LANES)
                ra = slice((2 * sl) * ATT_BLK, (2 * sl + 1) * ATT_BLK)
                rb = slice((2 * sl + 1) * ATT_BLK, (2 * sl + 2) * ATT_BLK)
                pair = lambda x: jnp.where(lo_mask, jnp.broadcast_to(x[ra], (ATT_BLK, LANES)),
                                           jnp.broadcast_to(x[rb], (ATT_BLK, LANES)))
                o_new = jnp.where(lo_mask, pv[ra, cols], pv[rb, cols]) / pair(l)
                lse_new = pair(lse_all)
                if br == 0:
                    o_tot, lse_tot = o_new, lse_new
                else:
                    o_old = rows(os_ref, sl, q0, ATT_BLK)
                    lse_old = rows(ls_ref, sl, q0, ATT_BLK)
                    mx = jnp.maximum(lse_old, lse_new)
                    e_old = jnp.exp(lse_old - mx)
                    e_new = jnp.exp(lse_new - mx)
                    den = e_old + e_new
                    o_tot = (e_old * o_old + e_new * o_new) / den
                    lse_tot = mx + jnp.log(den)
                if dil == 1:
                    os_ref[sl, pl.ds(pl.multiple_of(q0, ATT_BLK), ATT_BLK), :] = o_tot
                    ls_ref[sl, pl.ds(pl.multiple_of(q0, ATT_BLK), ATT_BLK), :] = lse_tot
                else:
                    os_ref[sl, pl.ds(q0, ATT_BLK, stride=dil), :] = o_tot
                    ls_ref[sl, pl.ds(q0, ATT_BLK, stride=dil), :] = lse_tot
            return carry

        lax.fori_loop(0, n_blocks, body, 0, unroll=2)

    o_ref[...] = jnp.concatenate([os_ref[sl] for sl in range(nslab)], axis=-1).astype(o_ref.dtype)


def _attn_prompt(q, k, v, bias_tabs, *, span):
    b, t, w = q.shape
    nslab = w // LANES
    cur = pl.BlockSpec((None, span, w), lambda i, j: (i, j, 0))
    prev = pl.BlockSpec((None, span, w), lambda i, j: (i, jnp.maximum(j - 1, 0), 0))
    slab = lambda rows: pltpu.VMEM((nslab, rows, LANES), F32)
    return pl.pallas_call(
        functools.partial(_attn_prompt_kernel, span=span),
        grid=(b, t // span),
        in_specs=[_const_spec(bias_tabs.shape, lambda i, j: (0, 0, 0)), cur, prev, cur, prev, cur],
        out_specs=cur,
        out_shape=jax.ShapeDtypeStruct((b, t, w), BF16),
        scratch_shapes=[slab(span), slab(2 * span), slab(2 * span), slab(span), slab(span)],
        compiler_params=_params(("parallel", "arbitrary")),
        name="attn_prompt",
    )(bias_tabs, q, k, k, v, v)


def _sample_bias_tables(n_past, t_new, t_pad):
    slopes = _alibi_slopes()
    n_keys = n_past + t_pad
    tabs = np.full((len(DILATIONS), ATT_HEADS * t_pad, n_keys), NEG_BIG, np.float32)
    for br, dil in enumerate(DILATIONS):
        for h, s in enumerate(slopes):
            for t in range(t_pad):
                tq = min(t, t_new - 1)
                for jj in range(ATT_BLK + 1):
                    idx = n_past + tq - jj * dil
                    if idx < 0:
                        continue
                    tabs[br, h * t_pad + t, idx] = -np.float32(s) * np.float32(jj * dil)
    return tabs[:, :, :n_past], tabs[:, :, n_past:]


def _attn_sample_kernel(bc_ref, bn_ref, q_ref, kn_ref, vn_ref, kc_ref, vc_ref, o_ref):
    t_pad, w = q_ref.shape
    scale = ATT_HDIM ** -0.5
    q = q_ref[...] * scale
    lane = lax.broadcasted_iota(jnp.int32, (t_pad, w), 1)
    head_of_lane = lane // ATT_HDIM
    zero = jnp.zeros_like(q)
    lhs = jnp.concatenate([jnp.where(head_of_lane == h, q, zero) for h in range(ATT_HEADS)], axis=0).astype(BF16)
    s_c = _dot(lhs, kc_ref[...].astype(BF16))
    s_n = _dot_nt(lhs, kn_ref[...].astype(BF16))
    vc = vc_ref[...].astype(BF16)
    vn = vn_ref[...].astype(BF16)
    outs, lses = [], []
    for br in range(len(DILATIONS)):
        bc = bc_ref[br]
        bn = bn_ref[br]
        sc = jnp.where(bc > 0.5 * NEG_BIG, s_c + bc, NEG_BIG)
        sn = jnp.where(bn > 0.5 * NEG_BIG, s_n + bn, NEG_BIG)
        m = jnp.maximum(jnp.max(sc, axis=-1, keepdims=True), jnp.max(sn, axis=-1, keepdims=True))
        pc = jnp.exp(sc - m)
        pn = jnp.exp(sn - m)
        l = jnp.sum(pc, axis=-1, keepdims=True) + jnp.sum(pn, axis=-1, keepdims=True)
        o = (_dot_nt(pc.astype(BF16), vc) + _dot(pn.astype(BF16), vn)) / l
        outs.append(o)
        lses.append(m + jnp.log(l))
    mx = functools.reduce(jnp.maximum, lses)
    es = [jnp.exp(x - mx) for x in lses]
    den = functools.reduce(lambda a, b: a + b, es)
    o = functools.reduce(lambda a, b: a + b, [e * x for e, x in zip(es, outs)]) / den
    res = zero
    for h in range(ATT_HEADS):
        res = jnp.where(head_of_lane == h, o[h * t_pad:(h + 1) * t_pad, :], res)
    o_ref[...] = res.astype(o_ref.dtype)


def _attn_sample(q, k, v, cache_k, cache_v, layer, bias_c, bias_n):
    b, t_pad, w = q.shape
    n_past = cache_k.shape[3]
    new = pl.BlockSpec((None, t_pad, w), lambda i: (i, 0, 0))
    cache = pl.BlockSpec((None, None, w, n_past), lambda i: (layer, i, 0, 0))
    return pl.pallas_call(
        _attn_sample_kernel,
        grid=(b,),
        in_specs=[_const_spec(bias_c.shape, lambda i: (0, 0, 0)),
                  _const_spec(bias_n.shape, lambda i: (0, 0, 0)),
                  new, new, new, cache, cache],
        out_specs=new,
        out_shape=jax.ShapeDtypeStruct((b, t_pad, w), F32),
        compiler_params=_params(("parallel",)),
        name="attn_sample",
    )(bias_c, bias_n, q, k, v, cache_k, cache_v)


def _hgrn_tables(chunk):
    sub = min(chunk, LANES)
    r = np.arange(sub)[:, None]
    c = np.arange(sub)[None, :]
    x = np.bitwise_xor(r, c)
    hb = np.zeros_like(x)
    for bit in range(1, 16):
        hb = np.where(x >> bit > 0, bit, hb)
    lev = np.where(c > r, -1, np.where(r // SUBLANES == c // SUBLANES, 0, hb)).astype(np.int32)
    lev = np.tile(lev, (chunk // sub, 1))
    sel = np.zeros((SUBLANES * LANES, sub), np.float32)
    for s in range(SUBLANES):
        sel[s * LANES:(s + 1) * LANES, s::SUBLANES] = 1.0
    rr = np.arange(chunk)
    tri = (rr[None, :] <= rr[:, None]).astype(np.float32)
    return lev, sel, tri


def _split3(x):
    hi = x.astype(BF16)
    r1 = x - hi.astype(F32)
    mid = r1.astype(BF16)
    lo = (r1 - mid.astype(F32)).astype(BF16)
    return hi, mid, lo


def _group_row(x, s):
    c, l = x.shape
    x3 = x.reshape(c // SUBLANES, SUBLANES, l)
    return jnp.broadcast_to(x3[:, s:s + 1, :], x3.shape).reshape(c, l)


def _block_last(x, m):
    c, l = x.shape
    x3 = x.reshape(c // m, m, l)
    return jnp.broadcast_to(x3[:, m - 1:m, :], x3.shape).reshape(c, l)


def _hgrn_lower_bound(raw, layer):
    e = jnp.exp(raw - jnp.max(raw, axis=0, keepdims=True))
    sm = e / jnp.sum(e, axis=0, keepdims=True)
    lb = jnp.zeros_like(sm[0:1])
    for i in range(1, layer + 1):
        lb = lb + sm[i:i + 1]
    return lb


def _hgrn_chunk(zq, zf, zi, zg, lb, ng, st, lev, sel, tri, live):
    chunk, dk = zq.shape
    sub = lev.shape[1]
    q = _silu(zq)
    f = lb + (1.0 - lb) * jax.nn.sigmoid(zf)
    g = jnp.log2(f)
    k = 1.0 - f
    if live is not None:
        g = jnp.where(live, g, 0.0)
        k = jnp.where(live, k, 0.0)
    vb = zi.astype(BF16)
    b3 = _dot(tri, jnp.concatenate(_split3(g), axis=1))
    b = b3[:, :dk] + b3[:, dk:2 * dk] + b3[:, 2 * dk:]

    vals = []
    for s in range(SUBLANES):
        decay = jnp.exp2(jnp.minimum(b - _group_row(b, s), 0.0))
        vals.append((q * decay * _group_row(k, s)).astype(BF16))
    a = jnp.where(lev == 0, _dot(jnp.concatenate(vals, axis=1), sel), 0.0)
    subs = [slice(r0, r0 + sub) for r0 in range(0, chunk, sub)]
    for bit in range(3, int(math.log2(sub))):
        m = 1 << bit
        b_end = _block_last(b, m)
        b_prev = jnp.concatenate([jnp.zeros((m, dk), F32), b_end[:chunk - m]], axis=0)
        qd = (q * jnp.exp2(b - b_prev)).astype(BF16)
        kd = (k * jnp.exp2(b_end - b)).astype(BF16)
        a_l = jnp.concatenate([_dot_nt(qd[rs], kd[rs]) for rs in subs], axis=0)
        a = jnp.where(lev == bit, a_l, a)
    ab = a.astype(BF16)
    o_parts = []
    for i, rs in enumerate(subs):
        if i == 0:
            o_parts.append(_dot(ab[rs], vb[rs]))
            continue
        b_piv = b[rs.start - 1:rs.start, :]
        qd = (q[rs] * jnp.exp2(b[rs] - b_piv)).astype(BF16)
        kd = (k[:rs.start] * jnp.exp2(b_piv - b[:rs.start])).astype(BF16)
        a_row = jnp.concatenate([_dot_nt(qd, kd).astype(BF16), ab[rs]], axis=1)
        o_parts.append(_dot(a_row, vb[:rs.stop]))
    o = jnp.concatenate(o_parts, axis=0) if len(o_parts) > 1 else o_parts[0]
    o = o + _dot_nt((q * jnp.exp2(b)).astype(BF16), st.astype(BF16))
    b_last = b[chunk - 1:chunk, :]
    kd = (k * jnp.exp2(b_last - b)).astype(BF16)
    st_new = st * jnp.exp2(b_last) + _dot_tn(vb, kd)
    on = o * lax.rsqrt(jnp.mean(o * o, axis=-1, keepdims=True) + EPS) * ng
    return on * _silu(zg), st_new


def _hgrn_prompt_kernel(zq_ref, zf_ref, zi_ref, zg_ref, lb_ref, ng_ref, lev_ref, sel_ref, tri_ref,
                        o_ref, sf_ref, st_ref, *, chunk, n_chunks, layer):
    tstep = pl.program_id(2)

    @pl.when(tstep == 0)
    def _():
        st_ref[...] = jnp.zeros_like(st_ref)

    lb = _hgrn_lower_bound(lb_ref[...], layer)
    lev = lev_ref[...]
    sel = sel_ref[...]
    tri = tri_ref[...]
    ng = ng_ref[...]

    def one_chunk(ci, carry):
        rows = pl.ds(pl.multiple_of(ci * chunk, chunk), chunk)
        o, st_new = _hgrn_chunk(zq_ref[rows, :], zf_ref[rows, :], zi_ref[rows, :], zg_ref[rows, :],
                                lb, ng, st_ref[...], lev, sel, tri, None)
        st_ref[...] = st_new
        o_ref[rows, :] = o.astype(o_ref.dtype)
        return carry

    lax.fori_loop(0, n_chunks, one_chunk, 0, unroll=2)

    @pl.when(tstep == pl.num_programs(2) - 1)
    def _():
        sf_ref[...] = st_ref[...].T


def _hgrn_prompt(zh, lb_raw, norm_g, layer, tables, *, tblk, chunk):
    b, t, w4 = zh.shape
    hd = w4 // 4
    dk = hd // HG_HEADS
    depth = lb_raw.shape[0]
    lev, sel, tri = tables
    col = lambda part: pl.BlockSpec((None, tblk, dk), lambda i, h, j: (i, j, part * HG_HEADS + h))
    return pl.pallas_call(
        functools.partial(_hgrn_prompt_kernel, chunk=chunk, n_chunks=tblk // chunk, layer=layer),
        grid=(b, HG_HEADS, t // tblk),
        in_specs=[col(0), col(1), col(2), col(3),
                  pl.BlockSpec((depth, dk), lambda i, h, j: (0, h)),
                  pl.BlockSpec((None, 1, dk), lambda i, h, j: (layer, 0, 0)),
                  _const_spec(lev.shape, lambda i, h, j: (0, 0)),
                  _const_spec(sel.shape, lambda i, h, j: (0, 0)),
                  _const_spec(tri.shape, lambda i, h, j: (0, 0))],
        out_specs=[pl.BlockSpec((None, tblk, dk), lambda i, h, j: (i, j, h)),
                   pl.BlockSpec((None, None, dk, dk), lambda i, h, j: (i, h, 0, 0))],
        out_shape=[jax.ShapeDtypeStruct((b, t, hd), BF16),
                   jax.ShapeDtypeStruct((b, HG_HEADS, dk, dk), F32)],
        scratch_shapes=[pltpu.VMEM((dk, dk), F32)],
        compiler_params=_params(("parallel", "parallel", "arbitrary")),
        name="hgrn_prompt",
    )(zh, zh, zh, zh, lb_raw, norm_g, lev, sel, tri)


def _hgrn_sample_kernel(zh_ref, lb_ref, ng_ref, lev_ref, sel_ref, tri_ref, s0_ref, o_ref, sf_ref,
                        *, nb, layer, t_valid):
    t_pad = zh_ref.shape[1]
    hd = o_ref.shape[2]
    dk = hd // HG_HEADS
    lev = lev_ref[...]
    sel = sel_ref[...]
    tri = tri_ref[...]
    ng = ng_ref[...]
    live = lax.broadcasted_iota(jnp.int32, (t_pad, dk), 0) < t_valid
    lbs = [_hgrn_lower_bound(lb_ref[:, h * dk:(h + 1) * dk], layer) for h in range(HG_HEADS)]

    def one_batch(bi, carry):
        for h in range(HG_HEADS):
            part = lambda p: zh_ref[bi, :, (p * HG_HEADS + h) * dk:(p * HG_HEADS + h + 1) * dk]
            o, st_new = _hgrn_chunk(part(0), part(1), part(2), part(3), lbs[h], ng,
                                    s0_ref[bi, h].T, lev, sel, tri, live)
            o_ref[bi, :, h * dk:(h + 1) * dk] = o
            sf_ref[bi, h] = st_new.T
        return carry

    lax.fori_loop(0, nb, one_batch, 0, unroll=2)


def _hgrn_sample(zh, lb_raw, norm_g, s0, layer, tables, *, nb, t_valid):
    b, t_pad, w4 = zh.shape
    hd = w4 // 4
    dk = hd // HG_HEADS
    depth = lb_raw.shape[0]
    lev, sel, tri = tables
    return pl.pallas_call(
        functools.partial(_hgrn_sample_kernel, nb=nb, layer=layer, t_valid=t_valid),
        grid=(b // nb,),
        in_specs=[pl.BlockSpec((nb, t_pad, w4), lambda i: (i, 0, 0)),
                  pl.BlockSpec((depth, hd), lambda i: (0, 0)),
                  pl.BlockSpec((None, 1, dk), lambda i: (layer, 0, 0)),
                  _const_spec(lev.shape, lambda i: (0, 0)),
                  _const_spec(sel.shape, lambda i: (0, 0)),
                  _const_spec(tri.shape, lambda i: (0, 0)),
                  pl.BlockSpec((None, nb, HG_HEADS, dk, dk), lambda i: (layer, i, 0, 0, 0))],
        out_specs=[pl.BlockSpec((nb, t_pad, hd), lambda i: (i, 0, 0)),
                   pl.BlockSpec((nb, HG_HEADS, dk, dk), lambda i: (i, 0, 0, 0))],
        out_shape=[jax.ShapeDtypeStruct((b, t_pad, hd), F32),
                   jax.ShapeDtypeStruct((b, HG_HEADS, dk, dk), F32)],
        compiler_params=_params(("parallel",)),
        name="hgrn_sample",
    )(zh, lb_raw, norm_g, lev, sel, tri, s0)


def _pad_rows(x, rows):
    return jnp.pad(x, ((0, 0), (0, rows - x.shape[1]), (0, 0)))


def _key_minor(cache):
    depth, b, n_past, h, hd = cache.shape
    return jnp.transpose(cache, (0, 1, 3, 4, 2)).reshape(depth, b, h * hd, n_past)


def kernel(x_prompt, x_sample, state_conv, cache_k_win, cache_v_win, state_hgrn, ln_ffn1, w_ffn1_gate, w_ffn1_up, w_ffn1_down, ln_mix, w_in, conv_dw_w, conv_dw_b, conv_ln_g, conv_ln_b, hg_lower_bounds, hg_norm_g, w_out, ln_ffn2, w_ffn2_gate, w_ffn2_up, w_ffn2_down, ln_final):
    bp, seq, d = x_prompt.shape
    bs, t_new, _ = x_sample.shape
    depth = w_in.shape[0]
    conv_dim = conv_dw_w.shape[2]
    att_dim = ATT_HEADS * ATT_HDIM
    hg_dim = hg_lower_bounds.shape[1]
    widths = (2 * conv_dim, att_dim, att_dim, att_dim, 4 * hg_dim)
    n_past = cache_k_win.shape[2]
    t_pad = SUBLANES
    keep = min(ATT_SPAN, seq)

    wg1, wu1, wd1 = (w.astype(BF16) for w in (w_ffn1_gate, w_ffn1_up, w_ffn1_down))
    wg2, wu2, wd2 = (w.astype(BF16) for w in (w_ffn2_gate, w_ffn2_up, w_ffn2_down))
    wi, wo = w_in.astype(BF16), w_out.astype(BF16)
    vec3 = lambda a: a.reshape(depth, 1, a.shape[-1])
    ln1, lnm, ln2 = vec3(ln_ffn1), vec3(ln_mix), vec3(ln_ffn2)
    dwb, clg, clb, hgn = vec3(conv_dw_b), vec3(conv_ln_g), vec3(conv_ln_b), vec3(hg_norm_g)
    lnf = ln_final.reshape(1, d)
    dww = jnp.pad(conv_dw_w, ((0, 0), (0, CONV_HALO - CONV_WIDTH), (0, 0)))
    cache_k, cache_v = _key_minor(cache_k_win), _key_minor(cache_v_win)
    halo_s = jnp.pad(state_conv, ((0, 0), (0, 0), (CONV_HALO - (CONV_WIDTH - 1), 0), (0, 0)))
    halo_p = jnp.zeros((bp, CONV_HALO, conv_dim), F32)

    bias_p = jnp.asarray(_prompt_bias_tables())
    bias_c, bias_n = (jnp.asarray(a) for a in _sample_bias_tables(n_past, t_new, t_pad))
    chunk_p = 256
    tabs_p = tuple(jnp.asarray(a, dt) for a, dt in zip(_hgrn_tables(chunk_p), (jnp.int32, BF16, BF16)))
    tabs_s = tuple(jnp.asarray(a, dt) for a, dt in zip(_hgrn_tables(t_pad), (jnp.int32, BF16, BF16)))

    hp = x_prompt.reshape(bp * seq, d)
    hs = x_sample.reshape(bs * t_new, d)
    outs = [[] for _ in range(8)]
    for l in range(depth):
        final = l == depth - 1
        x1, zc, q, k, v, zh = _dense_in(hp, l, ln1, wg1, wu1, wd1, lnm, wi, tm=512, f_chunk=1408, widths=widths)
        seq3 = lambda a: a.reshape(bp, seq, a.shape[-1])
        yc, tail = _conv_group(seq3(zc), halo_p, l, dww, dwb, clg, clb, nb=1, tt=512, out_dtype=BF16)
        ya = _attn_prompt(seq3(q), seq3(k), seq3(v), bias_p, span=ATT_SPAN)
        yh, sp = _hgrn_prompt(seq3(zh), hg_lower_bounds, hgn, l, tabs_p, tblk=1024, chunk=chunk_p)
        flat = lambda a: a.reshape(bp * seq, a.shape[-1])
        hp = _dense_out(x1, flat(yc), flat(ya), flat(yh), l, wo, ln2, wg2, wu2, wd2, lnf,
                        tm=512, f_chunk=1408, final=final)
        outs[0].append(tail[:, -(CONV_WIDTH - 1):])
        outs[2].append(seq3(k)[:, -keep:].reshape(bp, keep, ATT_HEADS, ATT_HDIM))
        outs[3].append(seq3(v)[:, -keep:].reshape(bp, keep, ATT_HEADS, ATT_HDIM))
        outs[6].append(sp)

        x1s, zcs, qs, ks, vs, zhs = _dense_in(hs, l, ln1, wg1, wu1, wd1, lnm, wi,
                                              tm=bs * t_new, f_chunk=1408, widths=widths)
        new3 = lambda a: _pad_rows(a.reshape(bs, t_new, a.shape[-1]), t_pad)
        ycs, us = _conv_group(new3(zcs), halo_s[l], l, dww, dwb, clg, clb, nb=bs, tt=t_pad, out_dtype=F32)
        yas = _attn_sample(new3(qs), new3(ks), new3(vs), cache_k, cache_v, l, bias_c, bias_n)
        yhs, ss = _hgrn_sample(new3(zhs), hg_lower_bounds, hgn, state_hgrn, l, tabs_s, nb=8, t_valid=t_new)
        unpad = lambda a: a[:, :t_new].reshape(bs * t_new, a.shape[-1])
        hs = _dense_out(x1s, unpad(ycs), unpad(yas), unpad(yhs), l, wo, ln2, wg2, wu2, wd2, lnf,
                        tm=bs * t_new, f_chunk=1408, final=final)
        outs[1].append(jnp.concatenate([state_conv[l][:, t_new:], us[:, :t_new]], axis=1))
        outs[4].append(ks.reshape(bs, t_new, ATT_HEADS, ATT_HDIM))
        outs[5].append(vs.reshape(bs, t_new, ATT_HEADS, ATT_HDIM))
        outs[7].append(ss)

    y_prompt = hp.reshape(bp, seq, d)
    y_sample = hs.reshape(bs, t_new, d)
    st = [jnp.stack(o) for o in outs]
    return (y_prompt, y_sample, st[0], st[1], st[2], st[3], st[4], st[5], st[6], st[7])
```

```python
import functools
import math

import numpy as np
import jax
import jax.numpy as jnp
from jax import lax
from jax.experimental import pallas as pl
from jax.experimental.pallas import tpu as pltpu

F32 = jnp.float32
BF16 = jnp.bfloat16

EPS = 1e-6
NEG_BIG = -1e30
CONV_WIDTH = 31
CONV_HALO = 32
ATT_HEADS = 4
ATT_HDIM = 64
ATT_BLK = 128
DILATIONS = (1, 4, 16)
ATT_SPAN = 2048
HG_HEADS = 4
LANES = 128
SUBLANES = 8
VMEM_LIMIT = 56 * 1024 * 1024


def _params(sem, vmem=VMEM_LIMIT):
    return pltpu.CompilerParams(dimension_semantics=sem, vmem_limit_bytes=vmem)


def _const_spec(shape, index):
    return pl.BlockSpec(shape, index, pipeline_mode=pl.Buffered(1))


def _rms(x, g):
    ms = jnp.mean(x * x, axis=-1, keepdims=True)
    return x * lax.rsqrt(ms + EPS) * g


def _silu(x):
    return x * jax.nn.sigmoid(x)


def _dot(a, b):
    return jnp.dot(a, b, preferred_element_type=F32)


def _dot_nt(a, b):
    return lax.dot_general(a, b, (((1,), (1,)), ((), ())), preferred_element_type=F32)


def _dot_tn(a, b):
    return lax.dot_general(a, b, (((0,), (0,)), ((), ())), preferred_element_type=F32)


def _swiglu(h, wg_ref, wu_ref, wd_ref, f_chunk):
    ffn = wg_ref.shape[1]
    acc = None
    for c0 in range(0, ffn, f_chunk):
        g = _dot(h, wg_ref[:, c0:c0 + f_chunk])
        u = _dot(h, wu_ref[:, c0:c0 + f_chunk])
        a = (_silu(g) * u).astype(BF16)
        y = _dot(a, wd_ref[c0:c0 + f_chunk, :])
        acc = y if acc is None else acc + y
    return acc


def _ffn_kernel(x_ref, ln_ref, wg_ref, wu_ref, wd_ref, o_ref, *, f_chunk):
    x = x_ref[...]
    h = _rms(x, ln_ref[...]).astype(BF16)
    o_ref[...] = x + 0.5 * _swiglu(h, wg_ref, wu_ref, wd_ref, f_chunk)


def _ffn_half_step(x, layer, ln, wg, wu, wd, *, tm, f_chunk):
    n, d = x.shape
    ffn = wg.shape[2]
    row = pl.BlockSpec((tm, d), lambda i: (i, 0))
    return pl.pallas_call(
        functools.partial(_ffn_kernel, f_chunk=f_chunk),
        grid=(n // tm,),
        in_specs=[row, pl.BlockSpec((None, 1, d), lambda i: (layer, 0, 0)),
                  _const_spec((None, d, ffn), lambda i: (layer, 0, 0)),
                  _const_spec((None, d, ffn), lambda i: (layer, 0, 0)),
                  _const_spec((None, ffn, d), lambda i: (layer, 0, 0))],
        out_specs=row,
        out_shape=jax.ShapeDtypeStruct((n, d), F32),
        compiler_params=_params(("parallel",)),
        name="ffn_half_step",
    )(x, ln, wg, wu, wd)


def _proj_in_kernel(x_ref, ln_ref, wi_ref, zc_ref, q_ref, k_ref, v_ref, zh_ref):
    h = _rms(x_ref[...], ln_ref[...]).astype(BF16)
    c0 = 0
    for ref in (zc_ref, q_ref, k_ref, v_ref, zh_ref):
        w = ref.shape[1]
        ref[...] = _dot(h, wi_ref[:, c0:c0 + w])
        c0 += w


def _proj_in(x, layer, ln, wi, *, tm, widths):
    n, d = x.shape
    n_in = wi.shape[2]
    row = lambda w: pl.BlockSpec((tm, w), lambda i: (i, 0))
    return pl.pallas_call(
        _proj_in_kernel,
        grid=(n // tm,),
        in_specs=[row(d), pl.BlockSpec((None, 1, d), lambda i: (layer, 0, 0)),
                  _const_spec((None, d, n_in), lambda i: (layer, 0, 0))],
        out_specs=[row(w) for w in widths],
        out_shape=[jax.ShapeDtypeStruct((n, w), F32) for w in widths],
        compiler_params=_params(("parallel",)),
        name="proj_in",
    )(x, ln, wi)


def _dense_in(x, layer, ln1, wg, wu, wd, lnm, wi, *, tm, f_chunk, widths):
    x1 = _ffn_half_step(x, layer, ln1, wg, wu, wd, tm=tm, f_chunk=f_chunk)
    return (x1,) + tuple(_proj_in(x1, layer, lnm, wi, tm=tm, widths=widths))


def _dense_out_kernel(x_ref, yc_ref, ya_ref, yh_ref, wo_ref, ln2_ref, wg_ref, wu_ref, wd_ref, lnf_ref,
                      o_ref, *, f_chunk, final):
    y = jnp.concatenate([r[...].astype(BF16) for r in (yc_ref, ya_ref, yh_ref)], axis=-1)
    x2 = x_ref[...] + _dot(y, wo_ref[...])
    h = _rms(x2, ln2_ref[...]).astype(BF16)
    x3 = x2 + 0.5 * _swiglu(h, wg_ref, wu_ref, wd_ref, f_chunk)
    if final:
        x3 = _rms(x3, lnf_ref[...])
    o_ref[...] = x3


def _dense_out(x1, yc, ya, yh, layer, wo, ln2, wg, wu, wd, lnf, *, tm, f_chunk, final):
    n, d = x1.shape
    ffn = wg.shape[2]
    row = lambda w: pl.BlockSpec((tm, w), lambda i: (i, 0))
    vec = pl.BlockSpec((None, 1, d), lambda i: (layer, 0, 0))
    return pl.pallas_call(
        functools.partial(_dense_out_kernel, f_chunk=f_chunk, final=final),
        grid=(n // tm,),
        in_specs=[row(d), row(yc.shape[1]), row(ya.shape[1]), row(yh.shape[1]),
                  _const_spec((None, d, d), lambda i: (layer, 0, 0)),
                  vec,
                  _const_spec((None, d, ffn), lambda i: (layer, 0, 0)),
                  _const_spec((None, d, ffn), lambda i: (layer, 0, 0)),
                  _const_spec((None, ffn, d), lambda i: (layer, 0, 0)),
                  pl.BlockSpec((1, d), lambda i: (0, 0))],
        out_specs=row(d),
        out_shape=jax.ShapeDtypeStruct((n, d), F32),
        compiler_params=_params(("parallel",)),
        name="dense_out",
    )(x1, yc, ya, yh, wo, ln2, wg, wu, wd, lnf)


def _conv_kernel(zc_ref, halo_ref, dww_ref, dwb_ref, lng_ref, lnb_ref, y_ref, tail_ref, ext_ref, *, nb, tt, tail):
    c = y_ref.shape[-1]
    lo = CONV_HALO - (CONV_WIDTH - 1)

    def one_batch(b):
        @pl.when(pl.program_id(1) == 0)
        def _():
            ext_ref[b, 0:CONV_HALO, :] = halo_ref[b]
            ext_ref[b, CONV_HALO + tt:CONV_HALO + tt + SUBLANES, :] = jnp.zeros((SUBLANES, c), F32)

        z = zc_ref[b]
        u = z[:, :c] * jax.nn.sigmoid(z[:, c:])
        ext_ref[b, CONV_HALO:CONV_HALO + tt, :] = u
        y = None
        for r in range(SUBLANES):
            z_r = None
            for o in range(r, CONV_HALO + 1, SUBLANES):
                if o < lo:
                    continue
                term = ext_ref[b, o - r:o - r + tt + SUBLANES, :] * dww_ref[o - lo:o - lo + 1, :]
                z_r = term if z_r is None else z_r + term
            part = z_r[r:r + tt, :]
            y = part if y is None else y + part
        y = y + dwb_ref[...]
        yc = y - jnp.mean(y, axis=-1, keepdims=True)
        yn = yc * lax.rsqrt(jnp.mean(yc * yc, axis=-1, keepdims=True) + EPS)
        y_ref[b] = _silu(yn * lng_ref[...] + lnb_ref[...]).astype(y_ref.dtype)
        tail_ref[b] = u[tt - tail:, :]
        carry = ext_ref[b, tt:tt + CONV_HALO, :]
        ext_ref[b, 0:CONV_HALO, :] = carry

    if nb == 1:
        one_batch(0)
    else:
        def body(b, carry):
            one_batch(b)
            return carry
        lax.fori_loop(0, nb, body, 0)


def _conv_group(zc, halo, layer, dww, dwb, lng, lnb, *, nb, tt, out_dtype):
    b, t, c2 = zc.shape
    c = c2 // 2
    tail = min(CONV_HALO, tt)
    vec = pl.BlockSpec((None, 1, c), lambda i, j: (layer, 0, 0))
    return pl.pallas_call(
        functools.partial(_conv_kernel, nb=nb, tt=tt, tail=tail),
        grid=(b // nb, t // tt),
        in_specs=[pl.BlockSpec((nb, tt, c2), lambda i, j: (i, j, 0)),
                  pl.BlockSpec((nb, CONV_HALO, c), lambda i, j: (i, 0, 0)),
                  pl.BlockSpec((None, CONV_HALO, c), lambda i, j: (layer, 0, 0)),
                  vec, vec, vec],
        out_specs=[pl.BlockSpec((nb, tt, c), lambda i, j: (i, j, 0)),
                   pl.BlockSpec((nb, tail, c), lambda i, j: (i, 0, 0))],
        out_shape=[jax.ShapeDtypeStruct((b, t, c), out_dtype),
                   jax.ShapeDtypeStruct((b, tail, c), F32)],
        scratch_shapes=[pltpu.VMEM((nb, CONV_HALO + tt + SUBLANES, c), F32)],
        compiler_params=_params(("parallel", "arbitrary")),
        name="conv_group",
    )(zc, halo, dww, dwb, lng, lnb)


def _alibi_slopes():
    return [2.0 ** (-8.0 * (h + 1) / ATT_HEADS) for h in range(ATT_HEADS)]


def _prompt_bias_tables():
    qi = np.arange(ATT_BLK)[:, None]
    ki = np.arange(2 * ATT_BLK)[None, :]
    j = qi - ki + ATT_BLK
    tabs = []
    for dil in DILATIONS:
        for first in (False, True):
            valid = (j >= 0) & (j <= ATT_BLK) & ((not first) | (ki >= ATT_BLK))
            rows = []
            for s in _alibi_slopes():
                bias = (-np.float32(s)) * (j * dil).astype(np.float32)
                rows.append(np.where(valid, bias, np.float32(NEG_BIG)).astype(np.float32))
            tabs.append(np.concatenate(rows, axis=0))
    return np.stack(tabs, axis=0)


def _head_pair_split(x, lo_mask):
    zero = jnp.zeros_like(x)
    return jnp.concatenate([jnp.where(lo_mask, x, zero), jnp.where(lo_mask, zero, x)], axis=0)


def _attn_prompt_kernel(bias_ref, q_ref, kp_ref, kc_ref, vp_ref, vc_ref, o_ref,
                        qs_ref, ks_ref, vs_ref, os_ref, ls_ref, *, span):
    j = pl.program_id(1)
    nslab = qs_ref.shape[0]
    scale = ATT_HDIM ** -0.5
    for sl in range(nslab):
        cols = slice(sl * LANES, (sl + 1) * LANES)
        qs_ref[sl] = q_ref[:, cols] * scale
        ks_ref[sl, 0:span, :] = kp_ref[:, cols]
        ks_ref[sl, span:2 * span, :] = kc_ref[:, cols]
        vs_ref[sl, 0:span, :] = vp_ref[:, cols]
        vs_ref[sl, span:2 * span, :] = vc_ref[:, cols]

    lane = lax.broadcasted_iota(jnp.int32, (ATT_BLK, LANES), 1)
    lo_mask = lane < ATT_HDIM
    n_blocks = span // ATT_BLK

    for br, dil in enumerate(DILATIONS):
        def body(i, carry, br=br, dil=dil):
            r = i % dil
            n = i // dil
            q0 = n * (ATT_BLK * dil) + r
            k0 = span + q0 - ATT_BLK * dil
            first = jnp.logical_and(j == 0, n == 0).astype(jnp.int32)
            bias = bias_ref[2 * br + first]

            def rows(ref, sl, start, size):
                if dil == 1:
                    return ref[sl, pl.ds(pl.multiple_of(start, ATT_BLK), size), :]
                return ref[sl, pl.ds(start, size, stride=dil), :]

            s_parts, v_parts = [], []
            for sl in range(nslab):
                ql = rows(qs_ref, sl, q0, ATT_BLK)
                kl = rows(ks_ref, sl, k0, 2 * ATT_BLK)
                v_parts.append(rows(vs_ref, sl, k0, 2 * ATT_BLK).astype(BF16))
                lhs = _head_pair_split(ql, lo_mask).astype(BF16)
                s_parts.append(_dot_nt(lhs, kl.astype(BF16)))
            s = jnp.concatenate(s_parts, axis=0)
            s = jnp.where(bias > 0.5 * NEG_BIG, s + bias, NEG_BIG)
            m = jnp.max(s, axis=-1, keepdims=True)
            p = jnp.exp(s - m)
            l = jnp.sum(p, axis=-1, keepdims=True)
            pv = _dot(p.astype(BF16), jnp.concatenate(v_parts, axis=1))
            lse_all = m + jnp.log(l)
            for sl in range(nslab):
                cols = slice(sl * LANES, (sl + 1) * LANES)
                ra = slice((2 * sl) * ATT_BLK, (2 * sl + 1) * ATT_BLK)
                rb = slice((2 * sl + 1) * ATT_BLK, (2 * sl + 2) * ATT_BLK)
                pair = lambda x: jnp.where(lo_mask, jnp.broadcast_to(x[ra], (ATT_BLK, LANES)),
                                           jnp.broadcast_to(x[rb], (ATT_BLK, LANES)))
                o_new = jnp.where(lo_mask, pv[ra, cols], pv[rb, cols]) / pair(l)
                lse_new = pair(lse_all)
                if br == 0:
                    o_tot, lse_tot = o_new, lse_new
                else:
                    o_old = rows(os_ref, sl, q0, ATT_BLK)
                    lse_old = rows(ls_ref, sl, q0, ATT_BLK)
                    mx = jnp.maximum(lse_old, lse_new)
                    e_old = jnp.exp(lse_old - mx)
                    e_new = jnp.exp(lse_new - mx)
                    den = e_old + e_new
                    o_tot = (e_old * o_old + e_new * o_new) / den
                    lse_tot = mx + jnp.log(den)
                if dil == 1:
                    os_ref[sl, pl.ds(pl.multiple_of(q0, ATT_BLK), ATT_BLK), :] = o_tot
                    ls_ref[sl, pl.ds(pl.multiple_of(q0, ATT_BLK), ATT_BLK), :] = lse_tot
                else:
                    os_ref[sl, pl.ds(q0, ATT_BLK, stride=dil), :] = o_tot
                    ls_ref[sl, pl.ds(q0, ATT_BLK, stride=dil), :] = lse_tot
            return carry

        lax.fori_loop(0, n_blocks, body, 0, unroll=2)

    o_ref[...] = jnp.concatenate([os_ref[sl] for sl in range(nslab)], axis=-1).astype(o_ref.dtype)


def _attn_prompt(q, k, v, bias_tabs, *, span):
    b, t, w = q.shape
    nslab = w // LANES
    cur = pl.BlockSpec((None, span, w), lambda i, j: (i, j, 0))
    prev = pl.BlockSpec((None, span, w), lambda i, j: (i, jnp.maximum(j - 1, 0), 0))
    slab = lambda rows: pltpu.VMEM((nslab, rows, LANES), F32)
    return pl.pallas_call(
        functools.partial(_attn_prompt_kernel, span=span),
        grid=(b, t // span),
        in_specs=[_const_spec(bias_tabs.shape, lambda i, j: (0, 0, 0)), cur, prev, cur, prev, cur],
        out_specs=cur,
        out_shape=jax.ShapeDtypeStruct((b, t, w), BF16),
        scratch_shapes=[slab(span), slab(2 * span), slab(2 * span), slab(span), slab(span)],
        compiler_params=_params(("parallel", "arbitrary")),
        name="attn_prompt",
    )(bias_tabs, q, k, k, v, v)


def _sample_bias_tables(n_past, t_new, t_pad):
    slopes = _alibi_slopes()
    n_keys = n_past + t_pad
    tabs = np.full((len(DILATIONS), ATT_HEADS * t_pad, n_keys), NEG_BIG, np.float32)
    for br, dil in enumerate(DILATIONS):
        for h, s in enumerate(slopes):
            for t in range(t_pad):
                tq = min(t, t_new - 1)
                for jj in range(ATT_BLK + 1):
                    idx = n_past + tq - jj * dil
                    if idx < 0:
                        continue
                    tabs[br, h * t_pad + t, idx] = -np.float32(s) * np.float32(jj * dil)
    return tabs[:, :, :n_past], tabs[:, :, n_past:]


def _attn_sample_kernel(bc_ref, bn_ref, q_ref, kn_ref, vn_ref, kc_ref, vc_ref, o_ref):
    t_pad, w = q_ref.shape
    scale = ATT_HDIM ** -0.5
    q = q_ref[...] * scale
    lane = lax.broadcasted_iota(jnp.int32, (t_pad, w), 1)
    head_of_lane = lane // ATT_HDIM
    zero = jnp.zeros_like(q)
    lhs = jnp.concatenate([jnp.where(head_of_lane == h, q, zero) for h in range(ATT_HEADS)], axis=0).astype(BF16)
    s_c = _dot(lhs, kc_ref[...].astype(BF16))
    s_n = _dot_nt(lhs, kn_ref[...].astype(BF16))
    vc = vc_ref[...].astype(BF16)
    vn = vn_ref[...].astype(BF16)
    pcs, pns, ls, lses = [], [], [], []
    for br in range(len(DILATIONS)):
        bc = bc_ref[br]
        bn = bn_ref[br]
        sc = jnp.where(bc > 0.5 * NEG_BIG, s_c + bc, NEG_BIG)
        sn = jnp.where(bn > 0.5 * NEG_BIG, s_n + bn, NEG_BIG)
        m = jnp.maximum(jnp.max(sc, axis=-1, keepdims=True), jnp.max(sn, axis=-1, keepdims=True))
        pc = jnp.exp(sc - m)
        pn = jnp.exp(sn - m)
        l = jnp.sum(pc, axis=-1, keepdims=True) + jnp.sum(pn, axis=-1, keepdims=True)
        pcs.append(pc.astype(BF16))
        pns.append(pn.astype(BF16))
        ls.append(l)
        lses.append(m + jnp.log(l))
    nq = ATT_HEADS * t_pad
    pv = _dot_nt(jnp.concatenate(pcs, axis=0), vc) + _dot(jnp.concatenate(pns, axis=0), vn)
    outs = [pv[br * nq:(br + 1) * nq] / ls[br] for br in range(len(DILATIONS))]
    mx = functools.reduce(jnp.maximum, lses)
    es = [jnp.exp(x - mx) for x in lses]
    den = functools.reduce(lambda a, b: a + b, es)
    o = functools.reduce(lambda a, b: a + b, [e * x for e, x in zip(es, outs)]) / den
    res = zero
    for h in range(ATT_HEADS):
        res = jnp.where(head_of_lane == h, o[h * t_pad:(h + 1) * t_pad, :], res)
    o_ref[...] = res.astype(o_ref.dtype)


def _attn_sample(q, k, v, cache_k, cache_v, layer, bias_c, bias_n):
    b, t_pad, w = q.shape
    n_past = cache_k.shape[3]
    new = pl.BlockSpec((None, t_pad, w), lambda i: (i, 0, 0))
    cache = pl.BlockSpec((None, None, w, n_past), lambda i: (layer, i, 0, 0))
    return pl.pallas_call(
        _attn_sample_kernel,
        grid=(b,),
        in_specs=[_const_spec(bias_c.shape, lambda i: (0, 0, 0)),
                  _const_spec(bias_n.shape, lambda i: (0, 0, 0)),
                  new, new, new, cache, cache],
        out_specs=new,
        out_shape=jax.ShapeDtypeStruct((b, t_pad, w), F32),
        compiler_params=_params(("parallel",)),
        name="attn_sample",
    )(bias_c, bias_n, q, k, v, cache_k, cache_v)


def _hgrn_tables(chunk):
    sub = min(chunk, LANES)
    r = np.arange(sub)[:, None]
    c = np.arange(sub)[None, :]
    x = np.bitwise_xor(r, c)
    hb = np.zeros_like(x)
    for bit in range(1, 16):
        hb = np.where(x >> bit > 0, bit, hb)
    lev = np.where(c > r, -1, np.where(r // SUBLANES == c // SUBLANES, 0, hb)).astype(np.int32)
    lev = np.tile(lev, (chunk // sub, 1))
    sel = np.zeros((SUBLANES * LANES, sub), np.float32)
    for s in range(SUBLANES):
        sel[s * LANES:(s + 1) * LANES, s::SUBLANES] = 1.0
    rr = np.arange(chunk)
    tri = (rr[None, :] <= rr[:, None]).astype(np.float32)
    return lev, sel, tri


def _split3(x):
    hi = x.astype(BF16)
    r1 = x - hi.astype(F32)
    mid = r1.astype(BF16)
    lo = (r1 - mid.astype(F32)).astype(BF16)
    return hi, mid, lo


def _group_row(x, s):
    c, l = x.shape
    x3 = x.reshape(c // SUBLANES, SUBLANES, l)
    return jnp.broadcast_to(x3[:, s:s + 1, :], x3.shape).reshape(c, l)


def _block_last(x, m):
    c, l = x.shape
    x3 = x.reshape(c // m, m, l)
    return jnp.broadcast_to(x3[:, m - 1:m, :], x3.shape).reshape(c, l)


def _hgrn_lower_bound(raw, layer):
    e = jnp.exp(raw - jnp.max(raw, axis=0, keepdims=True))
    sm = e / jnp.sum(e, axis=0, keepdims=True)
    lb = jnp.zeros_like(sm[0:1])
    for i in range(1, layer + 1):
        lb = lb + sm[i:i + 1]
    return lb


def _hgrn_chunk_gates(zq, zf, zi, lb, tri, live):
    dk = zq.shape[1]
    q = _silu(zq)
    f = lb + (1.0 - lb) * jax.nn.sigmoid(zf)
    g = jnp.log2(f)
    k = 1.0 - f
    if live is not None:
        g = jnp.where(live, g, 0.0)
        k = jnp.where(live, k, 0.0)
    b3 = _dot(tri, jnp.concatenate(_split3(g), axis=1))
    b = b3[:, :dk] + b3[:, dk:2 * dk] + b3[:, 2 * dk:]
    return q, k, b, zi.astype(BF16)


def _hgrn_chunk_local(q, k, b, vb, lev, sel):
    chunk, dk = q.shape
    sub = lev.shape[1]

    vals = []
    for s in range(SUBLANES):
        decay = jnp.exp2(jnp.minimum(b - _group_row(b, s), 0.0))
        vals.append((q * decay * _group_row(k, s)).astype(BF16))
    a = jnp.where(lev == 0, _dot(jnp.concatenate(vals, axis=1), sel), 0.0)
    subs = [slice(r0, r0 + sub) for r0 in range(0, chunk, sub)]
    for bit in range(3, int(math.log2(sub))):
        m = 1 << bit
        b_end = _block_last(b, m)
        b_prev = jnp.concatenate([jnp.zeros((m, dk), F32), b_end[:chunk - m]], axis=0)
        qd = (q * jnp.exp2(b - b_prev)).astype(BF16)
        kd = (k * jnp.exp2(b_end - b)).astype(BF16)
        a_l = jnp.concatenate([_dot_nt(qd[rs], kd[rs]) for rs in subs], axis=0)
        a = jnp.where(lev == bit, a_l, a)
    ab = a.astype(BF16)
    o_parts = []
    for i, rs in enumerate(subs):
        if i == 0:
            o_parts.append(_dot(ab[rs], vb[rs]))
            continue
        b_piv = b[rs.start - 1:rs.start, :]
        qd = (q[rs] * jnp.exp2(b[rs] - b_piv)).astype(BF16)
        kd = (k[:rs.start] * jnp.exp2(b_piv - b[:rs.start])).astype(BF16)
        a_row = jnp.concatenate([_dot_nt(qd, kd).astype(BF16), ab[rs]], axis=1)
        o_parts.append(_dot(a_row, vb[:rs.stop]))
    o = jnp.concatenate(o_parts, axis=0) if len(o_parts) > 1 else o_parts[0]
    q_in = (q * jnp.exp2(b)).astype(BF16)
    b_last = b[chunk - 1:chunk, :]
    kd = (k * jnp.exp2(b_last - b)).astype(BF16)
    return o, q_in, _dot_tn(vb, kd), jnp.exp2(b_last)


def _hgrn_chunk_finish(o_local, q_in, st, ng, zg):
    o = o_local + _dot_nt(q_in, st.astype(BF16))
    on = o * lax.rsqrt(jnp.mean(o * o, axis=-1, keepdims=True) + EPS) * ng
    return on * _silu(zg)


def _hgrn_prompt_kernel(zq_ref, zf_ref, zi_ref, zg_ref, lb_ref, ng_ref, lev_ref, sel_ref, tri_ref,
                        o_ref, sf_ref, st_ref, *, chunk, n_chunks, layer):
    tstep = pl.program_id(2)

    @pl.when(tstep == 0)
    def _():
        st_ref[...] = jnp.zeros_like(st_ref)

    lb = _hgrn_lower_bound(lb_ref[...], layer)
    lev = lev_ref[...]
    sel = sel_ref[...]
    tri = tri_ref[...]
    ng = ng_ref[...]

    def gates(ci):
        rows = slice(ci * chunk, (ci + 1) * chunk)
        return _hgrn_chunk_gates(zq_ref[rows, :], zf_ref[rows, :], zi_ref[rows, :], lb, tri, None)

    st = st_ref[...]
    ahead = gates(0)
    for ci in range(n_chunks):
        cur = ahead
        if ci + 1 < n_chunks:
            ahead = gates(ci + 1)
        o_local, q_in, s_own, d_all = _hgrn_chunk_local(*cur, lev, sel)
        rows = slice(ci * chunk, (ci + 1) * chunk)
        o_ref[rows, :] = _hgrn_chunk_finish(o_local, q_in, st, ng, zg_ref[rows, :]).astype(o_ref.dtype)
        st = st * d_all + s_own
    st_ref[...] = st

    @pl.when(tstep == pl.num_programs(2) - 1)
    def _():
        sf_ref[...] = st_ref[...].T


def _hgrn_prompt(zh, lb_raw, norm_g, layer, tables, *, tblk, chunk):
    b, t, w4 = zh.shape
    hd = w4 // 4
    dk = hd // HG_HEADS
    depth = lb_raw.shape[0]
    lev, sel, tri = tables
    col = lambda part: pl.BlockSpec((None, tblk, dk), lambda i, h, j: (i, j, part * HG_HEADS + h))
    return pl.pallas_call(
        functools.partial(_hgrn_prompt_kernel, chunk=chunk, n_chunks=tblk // chunk, layer=layer),
        grid=(b, HG_HEADS, t // tblk),
        in_specs=[col(0), col(1), col(2), col(3),
                  pl.BlockSpec((depth, dk), lambda i, h, j: (0, h)),
                  pl.BlockSpec((None, 1, dk), lambda i, h, j: (layer, 0, 0)),
                  _const_spec(lev.shape, lambda i, h, j: (0, 0)),
                  _const_spec(sel.shape, lambda i, h, j: (0, 0)),
                  _const_spec(tri.shape, lambda i, h, j: (0, 0))],
        out_specs=[pl.BlockSpec((None, tblk, dk), lambda i, h, j: (i, j, h)),
                   pl.BlockSpec((None, None, dk, dk), lambda i, h, j: (i, h, 0, 0))],
        out_shape=[jax.ShapeDtypeStruct((b, t, hd), BF16),
                   jax.ShapeDtypeStruct((b, HG_HEADS, dk, dk), F32)],
        scratch_shapes=[pltpu.VMEM((dk, dk), F32)],
        compiler_params=_params(("parallel", "parallel", "arbitrary")),
        name="hgrn_prompt",
    )(zh, zh, zh, zh, lb_raw, norm_g, lev, sel, tri)


def _hgrn_sample_kernel(zh_ref, lb_ref, ng_ref, lev_ref, sel_ref, tri_ref, s0_ref, o_ref, sf_ref,
                        *, nb, layer, t_valid):
    t_pad = zh_ref.shape[1]
    hd = o_ref.shape[2]
    dk = hd // HG_HEADS
    lev = lev_ref[...]
    sel = sel_ref[...]
    tri = tri_ref[...]
    ng = ng_ref[...]
    live = lax.broadcasted_iota(jnp.int32, (t_pad, dk), 0) < t_valid
    lbs = [_hgrn_lower_bound(lb_ref[:, h * dk:(h + 1) * dk], layer) for h in range(HG_HEADS)]

    def one_batch(bi, carry):
        for h in range(HG_HEADS):
            part = lambda p: zh_ref[bi, :, (p * HG_HEADS + h) * dk:(p * HG_HEADS + h + 1) * dk]
            gates = _hgrn_chunk_gates(part(0), part(1), part(2), lbs[h], tri, live)
            o_local, q_in, s_own, d_all = _hgrn_chunk_local(*gates, lev, sel)
            st = s0_ref[bi, h].T
            o_ref[bi, :, h * dk:(h + 1) * dk] = _hgrn_chunk_finish(o_local, q_in, st, ng, part(3))
            sf_ref[bi, h] = (st * d_all + s_own).T
        return carry

    lax.fori_loop(0, nb, one_batch, 0, unroll=2)


def _hgrn_sample(zh, lb_raw, norm_g, s0, layer, tables, *, nb, t_valid):
    b, t_pad, w4 = zh.shape
    hd = w4 // 4
    dk = hd // HG_HEADS
    depth = lb_raw.shape[0]
    lev, sel, tri = tables
    return pl.pallas_call(
        functools.partial(_hgrn_sample_kernel, nb=nb, layer=layer, t_valid=t_valid),
        grid=(b // nb,),
        in_specs=[pl.BlockSpec((nb, t_pad, w4), lambda i: (i, 0, 0)),
                  pl.BlockSpec((depth, hd), lambda i: (0, 0)),
                  pl.BlockSpec((None, 1, dk), lambda i: (layer, 0, 0)),
                  _const_spec(lev.shape, lambda i: (0, 0)),
                  _const_spec(sel.shape, lambda i: (0, 0)),
                  _const_spec(tri.shape, lambda i: (0, 0)),
                  pl.BlockSpec((None, nb, HG_HEADS, dk, dk), lambda i: (layer, i, 0, 0, 0))],
        out_specs=[pl.BlockSpec((nb, t_pad, hd), lambda i: (i, 0, 0)),
                   pl.BlockSpec((nb, HG_HEADS, dk, dk), lambda i: (i, 0, 0, 0))],
        out_shape=[jax.ShapeDtypeStruct((b, t_pad, hd), F32),
                   jax.ShapeDtypeStruct((b, HG_HEADS, dk, dk), F32)],
        compiler_params=_params(("parallel",)),
        name="hgrn_sample",
    )(zh, lb_raw, norm_g, lev, sel, tri, s0)


def _pad_rows(x, rows):
    return jnp.pad(x, ((0, 0), (0, rows - x.shape[1]), (0, 0)))


def _key_minor(cache):
    depth, b, n_past, h, hd = cache.shape
    return jnp.transpose(cache, (0, 1, 3, 4, 2)).reshape(depth, b, h * hd, n_past)


def kernel(x_prompt, x_sample, state_conv, cache_k_win, cache_v_win, state_hgrn, ln_ffn1, w_ffn1_gate, w_ffn1_up, w_ffn1_down, ln_mix, w_in, conv_dw_w, conv_dw_b, conv_ln_g, conv_ln_b, hg_lower_bounds, hg_norm_g, w_out, ln_ffn2, w_ffn2_gate, w_ffn2_up, w_ffn2_down, ln_final):
    bp, seq, d = x_prompt.shape
    bs, t_new, _ = x_sample.shape
    depth = w_in.shape[0]
    conv_dim = conv_dw_w.shape[2]
    att_dim = ATT_HEADS * ATT_HDIM
    hg_dim = hg_lower_bounds.shape[1]
    widths = (2 * conv_dim, att_dim, att_dim, att_dim, 4 * hg_dim)
    n_past = cache_k_win.shape[2]
    t_pad = SUBLANES
    keep = min(ATT_SPAN, seq)

    wg1, wu1, wd1 = (w.astype(BF16) for w in (w_ffn1_gate, w_ffn1_up, w_ffn1_down))
    wg2, wu2, wd2 = (w.astype(BF16) for w in (w_ffn2_gate, w_ffn2_up, w_ffn2_down))
    wi, wo = w_in.astype(BF16), w_out.astype(BF16)
    vec3 = lambda a: a.reshape(depth, 1, a.shape[-1])
    ln1, lnm, ln2 = vec3(ln_ffn1), vec3(ln_mix), vec3(ln_ffn2)
    dwb, clg, clb, hgn = vec3(conv_dw_b), vec3(conv_ln_g), vec3(conv_ln_b), vec3(hg_norm_g)
    lnf = ln_final.reshape(1, d)
    dww = jnp.pad(conv_dw_w, ((0, 0), (0, CONV_HALO - CONV_WIDTH), (0, 0)))
    cache_k, cache_v = _key_minor(cache_k_win), _key_minor(cache_v_win)
    halo_s = jnp.pad(state_conv, ((0, 0), (0, 0), (CONV_HALO - (CONV_WIDTH - 1), 0), (0, 0)))
    halo_p = jnp.zeros((bp, CONV_HALO, conv_dim), F32)

    bias_p = jnp.asarray(_prompt_bias_tables())
    bias_c, bias_n = (jnp.asarray(a) for a in _sample_bias_tables(n_past, t_new, t_pad))
    chunk_p = 128
    tabs_p = tuple(jnp.asarray(a, dt) for a, dt in zip(_hgrn_tables(chunk_p), (jnp.int32, BF16, BF16)))
    tabs_s = tuple(jnp.asarray(a, dt) for a, dt in zip(_hgrn_tables(t_pad), (jnp.int32, BF16, BF16)))

    hp = x_prompt.reshape(bp * seq, d)
    hs = x_sample.reshape(bs * t_new, d)
    outs = [[] for _ in range(8)]
    for l in range(depth):
        final = l == depth - 1
        x1, zc, q, k, v, zh = _dense_in(hp, l, ln1, wg1, wu1, wd1, lnm, wi, tm=512, f_chunk=1408, widths=widths)
        seq3 = lambda a: a.reshape(bp, seq, a.shape[-1])
        yc, tail = _conv_group(seq3(zc), halo_p, l, dww, dwb, clg, clb, nb=1, tt=512, out_dtype=BF16)
        ya = _attn_prompt(seq3(q), seq3(k), seq3(v), bias_p, span=ATT_SPAN)
        yh, sp = _hgrn_prompt(seq3(zh), hg_lower_bounds, hgn, l, tabs_p, tblk=1024, chunk=chunk_p)
        flat = lambda a: a.reshape(bp * seq, a.shape[-1])
        hp = _dense_out(x1, flat(yc), flat(ya), flat(yh), l, wo, ln2, wg2, wu2, wd2, lnf,
                        tm=512, f_chunk=1408, final=final)
        outs[0].append(tail[:, -(CONV_WIDTH - 1):])
        outs[2].append(seq3(k)[:, -keep:].reshape(bp, keep, ATT_HEADS, ATT_HDIM))
        outs[3].append(seq3(v)[:, -keep:].reshape(bp, keep, ATT_HEADS, ATT_HDIM))
        outs[6].append(sp)

        x1s, zcs, qs, ks, vs, zhs = _dense_in(hs, l, ln1, wg1, wu1, wd1, lnm, wi,
                                              tm=bs * t_new, f_chunk=1408, widths=widths)
        new3 = lambda a: _pad_rows(a.reshape(bs, t_new, a.shape[-1]), t_pad)
        ycs, us = _conv_group(new3(zcs), halo_s[l], l, dww, dwb, clg, clb, nb=bs, tt=t_pad, out_dtype=F32)
        yas = _attn_sample(new3(qs), new3(ks), new3(vs), cache_k, cache_v, l, bias_c, bias_n)
        yhs, ss = _hgrn_sample(new3(zhs), hg_lower_bounds, hgn, state_hgrn, l, tabs_s, nb=8, t_valid=t_new)
        unpad = lambda a: a[:, :t_new].reshape(bs * t_new, a.shape[-1])
        hs = _dense_out(x1s, unpad(ycs), unpad(yas), unpad(yhs), l, wo, ln2, wg2, wu2, wd2, lnf,
                        tm=bs * t_new, f_chunk=1408, final=final)
        outs[1].append(jnp.concatenate([state_conv[l][:, t_new:], us[:, :t_new]], axis=1))
        outs[4].append(ks.reshape(bs, t_new, ATT_HEADS, ATT_HDIM))
        outs[5].append(vs.reshape(bs, t_new, ATT_HEADS, ATT_HDIM))
        outs[7].append(ss)

    y_prompt = hp.reshape(bp, seq, d)
    y_sample = hs.reshape(bs, t_new, d)
    st = [jnp.stack(o) for o in outs]
    return (y_prompt, y_sample, st[0], st[1], st[2], st[3], st[4], st[5], st[6], st[7])
```

```python
import functools
import math

import numpy as np
import jax
import jax.numpy as jnp
from jax import lax
from jax.experimental import pallas as pl
from jax.experimental.pallas import tpu as pltpu

F32 = jnp.float32
BF16 = jnp.bfloat16

EPS = 1e-6
NEG_BIG = -1e30
CONV_WIDTH = 31
CONV_HALO = 32
ATT_HEADS = 4
ATT_HDIM = 64
ATT_BLK = 128
DILATIONS = (1, 4, 16)
ATT_SPAN = 2048
ATT_UNROLL = 8
HG_HEADS = 4
LANES = 128
SUBLANES = 8
VMEM_LIMIT = 56 * 1024 * 1024


def _params(sem, vmem=VMEM_LIMIT):
    return pltpu.CompilerParams(dimension_semantics=sem, vmem_limit_bytes=vmem)


def _const_spec(shape, index):
    return pl.BlockSpec(shape, index, pipeline_mode=pl.Buffered(1))


def _rms(x, g):
    ms = jnp.mean(x * x, axis=-1, keepdims=True)
    return x * lax.rsqrt(ms + EPS) * g


def _silu(x):
    return x * jax.nn.sigmoid(x)


def _dot(a, b):
    return jnp.dot(a, b, preferred_element_type=F32)


def _dot_nt(a, b):
    return lax.dot_general(a, b, (((1,), (1,)), ((), ())), preferred_element_type=F32)


def _dot_tn(a, b):
    return lax.dot_general(a, b, (((0,), (0,)), ((), ())), preferred_element_type=F32)


def _swiglu(h, wg_ref, wu_ref, wd_ref, f_chunk):
    ffn = wg_ref.shape[1]
    acc = None
    for c0 in range(0, ffn, f_chunk):
        g = _dot(h, wg_ref[:, c0:c0 + f_chunk])
        u = _dot(h, wu_ref[:, c0:c0 + f_chunk])
        a = (_silu(g) * u).astype(BF16)
        y = _dot(a, wd_ref[c0:c0 + f_chunk, :])
        acc = y if acc is None else acc + y
    return acc


def _ffn_kernel(x_ref, ln_ref, wg_ref, wu_ref, wd_ref, o_ref, *, f_chunk):
    x = x_ref[...]
    h = _rms(x, ln_ref[...]).astype(BF16)
    o_ref[...] = x + 0.5 * _swiglu(h, wg_ref, wu_ref, wd_ref, f_chunk)


def _ffn_half_step(x, layer, ln, wg, wu, wd, *, tm, f_chunk):
    n, d = x.shape
    ffn = wg.shape[2]
    row = pl.BlockSpec((tm, d), lambda i: (i, 0))
    return pl.pallas_call(
        functools.partial(_ffn_kernel, f_chunk=f_chunk),
        grid=(n // tm,),
        in_specs=[row, pl.BlockSpec((None, 1, d), lambda i: (layer, 0, 0)),
                  _const_spec((None, d, ffn), lambda i: (layer, 0, 0)),
                  _const_spec((None, d, ffn), lambda i: (layer, 0, 0)),
                  _const_spec((None, ffn, d), lambda i: (layer, 0, 0))],
        out_specs=row,
        out_shape=jax.ShapeDtypeStruct((n, d), F32),
        compiler_params=_params(("parallel",)),
        name="ffn_half_step",
    )(x, ln, wg, wu, wd)


def _proj_in_kernel(x_ref, ln_ref, wi_ref, zc_ref, q_ref, k_ref, v_ref, zh_ref):
    h = _rms(x_ref[...], ln_ref[...]).astype(BF16)
    c0 = 0
    for ref in (zc_ref, q_ref, k_ref, v_ref, zh_ref):
        w = ref.shape[1]
        ref[...] = _dot(h, wi_ref[:, c0:c0 + w])
        c0 += w


def _proj_in(x, layer, ln, wi, *, tm, widths):
    n, d = x.shape
    n_in = wi.shape[2]
    row = lambda w: pl.BlockSpec((tm, w), lambda i: (i, 0))
    return pl.pallas_call(
        _proj_in_kernel,
        grid=(n // tm,),
        in_specs=[row(d), pl.BlockSpec((None, 1, d), lambda i: (layer, 0, 0)),
                  _const_spec((None, d, n_in), lambda i: (layer, 0, 0))],
        out_specs=[row(w) for w in widths],
        out_shape=[jax.ShapeDtypeStruct((n, w), F32) for w in widths],
        compiler_params=_params(("parallel",)),
        name="proj_in",
    )(x, ln, wi)


def _dense_in(x, layer, ln1, wg, wu, wd, lnm, wi, *, tm, f_chunk, widths):
    x1 = _ffn_half_step(x, layer, ln1, wg, wu, wd, tm=tm, f_chunk=f_chunk)
    return (x1,) + tuple(_proj_in(x1, layer, lnm, wi, tm=tm, widths=widths))


def _dense_out_kernel(x_ref, yc_ref, ya_ref, yh_ref, wo_ref, ln2_ref, wg_ref, wu_ref, wd_ref, lnf_ref,
                      o_ref, *, f_chunk, final):
    y = jnp.concatenate([r[...].astype(BF16) for r in (yc_ref, ya_ref, yh_ref)], axis=-1)
    x2 = x_ref[...] + _dot(y, wo_ref[...])
    h = _rms(x2, ln2_ref[...]).astype(BF16)
    x3 = x2 + 0.5 * _swiglu(h, wg_ref, wu_ref, wd_ref, f_chunk)
    if final:
        x3 = _rms(x3, lnf_ref[...])
    o_ref[...] = x3


def _dense_out(x1, yc, ya, yh, layer, wo, ln2, wg, wu, wd, lnf, *, tm, f_chunk, final):
    n, d = x1.shape
    ffn = wg.shape[2]
    row = lambda w: pl.BlockSpec((tm, w), lambda i: (i, 0))
    vec = pl.BlockSpec((None, 1, d), lambda i: (layer, 0, 0))
    return pl.pallas_call(
        functools.partial(_dense_out_kernel, f_chunk=f_chunk, final=final),
        grid=(n // tm,),
        in_specs=[row(d), row(yc.shape[1]), row(ya.shape[1]), row(yh.shape[1]),
                  _const_spec((None, d, d), lambda i: (layer, 0, 0)),
                  vec,
                  _const_spec((None, d, ffn), lambda i: (layer, 0, 0)),
                  _const_spec((None, d, ffn), lambda i: (layer, 0, 0)),
                  _const_spec((None, ffn, d), lambda i: (layer, 0, 0)),
                  pl.BlockSpec((1, d), lambda i: (0, 0))],
        out_specs=row(d),
        out_shape=jax.ShapeDtypeStruct((n, d), F32),
        compiler_params=_params(("parallel",)),
        name="dense_out",
    )(x1, yc, ya, yh, wo, ln2, wg, wu, wd, lnf)


def _conv_kernel(zc_ref, halo_ref, dww_ref, dwb_ref, lng_ref, lnb_ref, y_ref, tail_ref, ext_ref, *, nb, tt, tail):
    c = y_ref.shape[-1]
    lo = CONV_HALO - (CONV_WIDTH - 1)

    def one_batch(b):
        @pl.when(pl.program_id(1) == 0)
        def _():
            ext_ref[b, 0:CONV_HALO, :] = halo_ref[b]
            ext_ref[b, CONV_HALO + tt:CONV_HALO + tt + SUBLANES, :] = jnp.zeros((SUBLANES, c), F32)

        z = zc_ref[b]
        u = z[:, :c] * jax.nn.sigmoid(z[:, c:])
        ext_ref[b, CONV_HALO:CONV_HALO + tt, :] = u
        y = None
        for r in range(SUBLANES):
            z_r = None
            for o in range(r, CONV_HALO + 1, SUBLANES):
                if o < lo:
                    continue
                term = ext_ref[b, o - r:o - r + tt + SUBLANES, :] * dww_ref[o - lo:o - lo + 1, :]
                z_r = term if z_r is None else z_r + term
            part = z_r[r:r + tt, :]
            y = part if y is None else y + part
        y = y + dwb_ref[...]
        yc = y - jnp.mean(y, axis=-1, keepdims=True)
        yn = yc * lax.rsqrt(jnp.mean(yc * yc, axis=-1, keepdims=True) + EPS)
        y_ref[b] = _silu(yn * lng_ref[...] + lnb_ref[...]).astype(y_ref.dtype)
        tail_ref[b] = u[tt - tail:, :]
        carry = ext_ref[b, tt:tt + CONV_HALO, :]
        ext_ref[b, 0:CONV_HALO, :] = carry

    if nb == 1:
        one_batch(0)
    else:
        def body(b, carry):
            one_batch(b)
            return carry
        lax.fori_loop(0, nb, body, 0)


def _conv_group(zc, halo, layer, dww, dwb, lng, lnb, *, nb, tt, out_dtype):
    b, t, c2 = zc.shape
    c = c2 // 2
    tail = min(CONV_HALO, tt)
    vec = pl.BlockSpec((None, 1, c), lambda i, j: (layer, 0, 0))
    return pl.pallas_call(
        functools.partial(_conv_kernel, nb=nb, tt=tt, tail=tail),
        grid=(b // nb, t // tt),
        in_specs=[pl.BlockSpec((nb, tt, c2), lambda i, j: (i, j, 0)),
                  pl.BlockSpec((nb, CONV_HALO, c), lambda i, j: (i, 0, 0)),
                  pl.BlockSpec((None, CONV_HALO, c), lambda i, j: (layer, 0, 0)),
                  vec, vec, vec],
        out_specs=[pl.BlockSpec((nb, tt, c), lambda i, j: (i, j, 0)),
                   pl.BlockSpec((nb, tail, c), lambda i, j: (i, 0, 0))],
        out_shape=[jax.ShapeDtypeStruct((b, t, c), out_dtype),
                   jax.ShapeDtypeStruct((b, tail, c), F32)],
        scratch_shapes=[pltpu.VMEM((nb, CONV_HALO + tt + SUBLANES, c), F32)],
        compiler_params=_params(("parallel", "arbitrary")),
        name="conv_group",
    )(zc, halo, dww, dwb, lng, lnb)


def _alibi_slopes():
    return [2.0 ** (-8.0 * (h + 1) / ATT_HEADS) for h in range(ATT_HEADS)]


def _prompt_bias_tables():
    qi = np.arange(ATT_BLK)[:, None]
    ki = np.arange(2 * ATT_BLK)[None, :]
    j = qi - ki + ATT_BLK
    tabs = []
    for dil in DILATIONS:
        for first in (False, True):
            valid = (j >= 0) & (j <= ATT_BLK) & ((not first) | (ki >= ATT_BLK))
            rows = []
            for s in _alibi_slopes():
                bias = (-np.float32(s)) * (j * dil).astype(np.float32)
                rows.append(np.where(valid, bias, np.float32(NEG_BIG)).astype(np.float32))
            tabs.append(np.concatenate(rows, axis=0))
    return np.stack(tabs, axis=0)


def _head_pair_split(x, lo_mask):
    zero = jnp.zeros_like(x)
    return jnp.concatenate([jnp.where(lo_mask, x, zero), jnp.where(lo_mask, zero, x)], axis=0)


def _attn_prompt_kernel(bias_ref, q_ref, kp_ref, kc_ref, vp_ref, vc_ref, o_ref,
                        qs_ref, ks_ref, vs_ref, os_ref, ls_ref, *, span):
    j = pl.program_id(1)
    nslab = qs_ref.shape[0]
    scale = ATT_HDIM ** -0.5
    for sl in range(nslab):
        cols = slice(sl * LANES, (sl + 1) * LANES)
        qs_ref[sl] = q_ref[:, cols] * scale
        ks_ref[sl, 0:span, :] = kp_ref[:, cols]
        ks_ref[sl, span:2 * span, :] = kc_ref[:, cols]
        vs_ref[sl, 0:span, :] = vp_ref[:, cols]
        vs_ref[sl, span:2 * span, :] = vc_ref[:, cols]

    lane = lax.broadcasted_iota(jnp.int32, (ATT_BLK, LANES), 1)
    lo_mask = lane < ATT_HDIM
    n_blocks = span // ATT_BLK

    order = sorted(range(len(DILATIONS)), key=lambda b: -DILATIONS[b])
    for pos, br in enumerate(order):
        dil = DILATIONS[br]

        def rows(start, size, dil=dil):
            if dil == 1:
                return pl.ds(pl.multiple_of(start, ATT_BLK), size)
            return pl.ds(start, size, stride=dil)

        def scores(i, dil=dil, rows=rows):
            q0 = (i // dil) * (ATT_BLK * dil) + i % dil
            k0 = span + q0 - ATT_BLK * dil
            s_parts, v_parts = [], []
            for sl in range(nslab):
                ql = qs_ref[sl, rows(q0, ATT_BLK), :]
                kl = ks_ref[sl, rows(k0, 2 * ATT_BLK), :]
                v_parts.append(vs_ref[sl, rows(k0, 2 * ATT_BLK), :].astype(BF16))
                lhs = _head_pair_split(ql, lo_mask).astype(BF16)
                s_parts.append(_dot_nt(lhs, kl.astype(BF16)))
            return jnp.concatenate(s_parts, axis=0), jnp.concatenate(v_parts, axis=1)

        def finish(i, s, v, br=br, dil=dil, pos=pos, rows=rows):
            q0 = (i // dil) * (ATT_BLK * dil) + i % dil
            first = jnp.logical_and(j == 0, i // dil == 0).astype(jnp.int32)
            bias = bias_ref[2 * br + first]
            s = jnp.where(bias > 0.5 * NEG_BIG, s + bias, NEG_BIG)
            m = jnp.max(s, axis=-1, keepdims=True)
            p = jnp.exp(s - m)
            l = jnp.sum(p, axis=-1, keepdims=True)
            pv = _dot(p.astype(BF16), v)
            lse_all = m + jnp.log(l)
            for sl in range(nslab):
                cols = slice(sl * LANES, (sl + 1) * LANES)
                ra = slice((2 * sl) * ATT_BLK, (2 * sl + 1) * ATT_BLK)
                rb = slice((2 * sl + 1) * ATT_BLK, (2 * sl + 2) * ATT_BLK)
                pair = lambda x: jnp.where(lo_mask, jnp.broadcast_to(x[ra], (ATT_BLK, LANES)),
                                           jnp.broadcast_to(x[rb], (ATT_BLK, LANES)))
                o_new = jnp.where(lo_mask, pv[ra, cols], pv[rb, cols]) / pair(l)
                lse_new = pair(lse_all)
                out_rows = rows(q0, ATT_BLK)
                if pos == 0:
                    o_tot, lse_tot = o_new, lse_new
                else:
                    o_old = os_ref[sl, out_rows, :]
                    lse_old = ls_ref[sl, out_rows, :]
                    mx = jnp.maximum(lse_old, lse_new)
                    e_old = jnp.exp(lse_old - mx)
                    e_new = jnp.exp(lse_new - mx)
                    den = e_old + e_new
                    o_tot = (e_old * o_old + e_new * o_new) / den
                    lse_tot = mx + jnp.log(den)
                os_ref[sl, out_rows, :] = o_tot
                ls_ref[sl, out_rows, :] = lse_tot

        def trip(t, carry, scores=scores, finish=finish):
            base = t * ATT_UNROLL
            ahead = scores(base)
            for u in range(ATT_UNROLL):
                s, v = ahead
                if u + 1 < ATT_UNROLL:
                    ahead = scores(base + u + 1)
                finish(base + u, s, v)
            return carry

        lax.fori_loop(0, n_blocks // ATT_UNROLL, trip, 0)

    o_ref[...] = jnp.concatenate([os_ref[sl] for sl in range(nslab)], axis=-1).astype(o_ref.dtype)


def _attn_prompt(q, k, v, bias_tabs, *, span):
    b, t, w = q.shape
    nslab = w // LANES
    cur = pl.BlockSpec((None, span, w), lambda i, j: (i, j, 0))
    prev = pl.BlockSpec((None, span, w), lambda i, j: (i, jnp.maximum(j - 1, 0), 0))
    slab = lambda rows: pltpu.VMEM((nslab, rows, LANES), F32)
    return pl.pallas_call(
        functools.partial(_attn_prompt_kernel, span=span),
        grid=(b, t // span),
        in_specs=[_const_spec(bias_tabs.shape, lambda i, j: (0, 0, 0)), cur, prev, cur, prev, cur],
        out_specs=cur,
        out_shape=jax.ShapeDtypeStruct((b, t, w), BF16),
        scratch_shapes=[slab(span), slab(2 * span), slab(2 * span), slab(span), slab(span)],
        compiler_params=_params(("parallel", "arbitrary")),
        name="attn_prompt",
    )(bias_tabs, q, k, k, v, v)


def _sample_bias_tables(n_past, t_new, t_pad):
    slopes = _alibi_slopes()
    n_keys = n_past + t_pad
    tabs = np.full((len(DILATIONS), ATT_HEADS * t_pad, n_keys), NEG_BIG, np.float32)
    for br, dil in enumerate(DILATIONS):
        for h, s in enumerate(slopes):
            for t in range(t_pad):
                tq = min(t, t_new - 1)
                for jj in range(ATT_BLK + 1):
                    idx = n_past + tq - jj * dil
                    if idx < 0:
                        continue
                    tabs[br, h * t_pad + t, idx] = -np.float32(s) * np.float32(jj * dil)
    return tabs[:, :, :n_past], tabs[:, :, n_past:]


def _attn_sample_kernel(bc_ref, bn_ref, q_ref, kn_ref, vn_ref, kc_ref, vc_ref, o_ref):
    t_pad, w = q_ref.shape
    scale = ATT_HDIM ** -0.5
    q = q_ref[...] * scale
    lane = lax.broadcasted_iota(jnp.int32, (t_pad, w), 1)
    head_of_lane = lane // ATT_HDIM
    zero = jnp.zeros_like(q)
    lhs = jnp.concatenate([jnp.where(head_of_lane == h, q, zero) for h in range(ATT_HEADS)], axis=0).astype(BF16)
    s_c = _dot(lhs, kc_ref[...].astype(BF16))
    s_n = _dot_nt(lhs, kn_ref[...].astype(BF16))
    vc = vc_ref[...].astype(BF16)
    vn = vn_ref[...].astype(BF16)
    pcs, pns, ls, lses = [], [], [], []
    for br in range(len(DILATIONS)):
        bc = bc_ref[br]
        bn = bn_ref[br]
        sc = jnp.where(bc > 0.5 * NEG_BIG, s_c + bc, NEG_BIG)
        sn = jnp.where(bn > 0.5 * NEG_BIG, s_n + bn, NEG_BIG)
        m = jnp.maximum(jnp.max(sc, axis=-1, keepdims=True), jnp.max(sn, axis=-1, keepdims=True))
        pc = jnp.exp(sc - m)
        pn = jnp.exp(sn - m)
        l = jnp.sum(pc, axis=-1, keepdims=True) + jnp.sum(pn, axis=-1, keepdims=True)
        pcs.append(pc.astype(BF16))
        pns.append(pn.astype(BF16))
        ls.append(l)
        lses.append(m + jnp.log(l))
    nq = ATT_HEADS * t_pad
    pv = _dot_nt(jnp.concatenate(pcs, axis=0), vc) + _dot(jnp.concatenate(pns, axis=0), vn)
    outs = [pv[br * nq:(br + 1) * nq] / ls[br] for br in range(len(DILATIONS))]
    mx = functools.reduce(jnp.maximum, lses)
    es = [jnp.exp(x - mx) for x in lses]
    den = functools.reduce(lambda a, b: a + b, es)
    o = functools.reduce(lambda a, b: a + b, [e * x for e, x in zip(es, outs)]) / den
    res = zero
    for h in range(ATT_HEADS):
        res = jnp.where(head_of_lane == h, o[h * t_pad:(h + 1) * t_pad, :], res)
    o_ref[...] = res.astype(o_ref.dtype)


def _attn_sample(q, k, v, cache_k, cache_v, layer, bias_c, bias_n):
    b, t_pad, w = q.shape
    n_past = cache_k.shape[3]
    new = pl.BlockSpec((None, t_pad, w), lambda i: (i, 0, 0))
    cache = pl.BlockSpec((None, None, w, n_past), lambda i: (layer, i, 0, 0))
    return pl.pallas_call(
        _attn_sample_kernel,
        grid=(b,),
        in_specs=[_const_spec(bias_c.shape, lambda i: (0, 0, 0)),
                  _const_spec(bias_n.shape, lambda i: (0, 0, 0)),
                  new, new, new, cache, cache],
        out_specs=new,
        out_shape=jax.ShapeDtypeStruct((b, t_pad, w), F32),
        compiler_params=_params(("parallel",)),
        name="attn_sample",
    )(bias_c, bias_n, q, k, v, cache_k, cache_v)


def _hgrn_tables(chunk):
    sub = min(chunk, LANES)
    r = np.arange(sub)[:, None]
    c = np.arange(sub)[None, :]
    x = np.bitwise_xor(r, c)
    hb = np.zeros_like(x)
    for bit in range(1, 16):
        hb = np.where(x >> bit > 0, bit, hb)
    lev = np.where(c > r, -1, np.where(r // SUBLANES == c // SUBLANES, 0, hb)).astype(np.int32)
    lev = np.tile(lev, (chunk // sub, 1))
    sel = np.zeros((SUBLANES * LANES, sub), np.float32)
    for s in range(SUBLANES):
        sel[s * LANES:(s + 1) * LANES, s::SUBLANES] = 1.0
    rr = np.arange(chunk)
    tri = (rr[None, :] <= rr[:, None]).astype(np.float32)
    return lev, sel, tri


def _split3(x):
    hi = x.astype(BF16)
    r1 = x - hi.astype(F32)
    mid = r1.astype(BF16)
    lo = (r1 - mid.astype(F32)).astype(BF16)
    return hi, mid, lo


def _group_row(x, s):
    c, l = x.shape
    x3 = x.reshape(c // SUBLANES, SUBLANES, l)
    return jnp.broadcast_to(x3[:, s:s + 1, :], x3.shape).reshape(c, l)


def _block_last(x, m):
    c, l = x.shape
    x3 = x.reshape(c // m, m, l)
    return jnp.broadcast_to(x3[:, m - 1:m, :], x3.shape).reshape(c, l)


def _hgrn_lower_bound(raw, layer):
    e = jnp.exp(raw - jnp.max(raw, axis=0, keepdims=True))
    sm = e / jnp.sum(e, axis=0, keepdims=True)
    lb = jnp.zeros_like(sm[0:1])
    for i in range(1, layer + 1):
        lb = lb + sm[i:i + 1]
    return lb


def _hgrn_chunk_gates(zq, zf, zi, lb, tri, live):
    dk = zq.shape[1]
    q = _silu(zq)
    f = lb + (1.0 - lb) * jax.nn.sigmoid(zf)
    g = jnp.log2(f)
    k = 1.0 - f
    if live is not None:
        g = jnp.where(live, g, 0.0)
        k = jnp.where(live, k, 0.0)
    b3 = _dot(tri, jnp.concatenate(_split3(g), axis=1))
    b = b3[:, :dk] + b3[:, dk:2 * dk] + b3[:, 2 * dk:]
    return q, k, b, zi.astype(BF16)


def _hgrn_chunk_local(q, k, b, vb, lev, sel):
    chunk, dk = q.shape
    sub = lev.shape[1]

    vals = []
    for s in range(SUBLANES):
        decay = jnp.exp2(jnp.minimum(b - _group_row(b, s), 0.0))
        vals.append((q * decay * _group_row(k, s)).astype(BF16))
    a = jnp.where(lev == 0, _dot(jnp.concatenate(vals, axis=1), sel), 0.0)
    subs = [slice(r0, r0 + sub) for r0 in range(0, chunk, sub)]
    for bit in range(3, int(math.log2(sub))):
        m = 1 << bit
        b_end = _block_last(b, m)
        b_prev = jnp.concatenate([jnp.zeros((m, dk), F32), b_end[:chunk - m]], axis=0)
        qd = (q * jnp.exp2(b - b_prev)).astype(BF16)
        kd = (k * jnp.exp2(b_end - b)).astype(BF16)
        a_l = jnp.concatenate([_dot_nt(qd[rs], kd[rs]) for rs in subs], axis=0)
        a = jnp.where(lev == bit, a_l, a)
    ab = a.astype(BF16)
    o_parts = []
    for i, rs in enumerate(subs):
        if i == 0:
            o_parts.append(_dot(ab[rs], vb[rs]))
            continue
        b_piv = b[rs.start - 1:rs.start, :]
        qd = (q[rs] * jnp.exp2(b[rs] - b_piv)).astype(BF16)
        kd = (k[:rs.start] * jnp.exp2(b_piv - b[:rs.start])).astype(BF16)
        a_row = jnp.concatenate([_dot_nt(qd, kd).astype(BF16), ab[rs]], axis=1)
        o_parts.append(_dot(a_row, vb[:rs.stop]))
    o = jnp.concatenate(o_parts, axis=0) if len(o_parts) > 1 else o_parts[0]
    q_in = (q * jnp.exp2(b)).astype(BF16)
    b_last = b[chunk - 1:chunk, :]
    kd = (k * jnp.exp2(b_last - b)).astype(BF16)
    return o, q_in, _dot_tn(vb, kd), jnp.exp2(b_last)


def _hgrn_chunk_finish(o_local, q_in, st, ng, zg):
    o = o_local + _dot_nt(q_in, st.astype(BF16))
    on = o * lax.rsqrt(jnp.mean(o * o, axis=-1, keepdims=True) + EPS) * ng
    return on * _silu(zg)


def _hgrn_prompt_kernel(zq_ref, zf_ref, zi_ref, zg_ref, lb_ref, ng_ref, lev_ref, sel_ref, tri_ref,
                        o_ref, sf_ref, st_ref, *, chunk, n_chunks, layer):
    tstep = pl.program_id(2)

    @pl.when(tstep == 0)
    def _():
        st_ref[...] = jnp.zeros_like(st_ref)

    lb = _hgrn_lower_bound(lb_ref[...], layer)
    lev = lev_ref[...]
    sel = sel_ref[...]
    tri = tri_ref[...]
    ng = ng_ref[...]

    def gates(ci):
        rows = slice(ci * chunk, (ci + 1) * chunk)
        return _hgrn_chunk_gates(zq_ref[rows, :], zf_ref[rows, :], zi_ref[rows, :], lb, tri, None)

    st = st_ref[...]
    ahead = gates(0)
    for ci in range(n_chunks):
        cur = ahead
        if ci + 1 < n_chunks:
            ahead = gates(ci + 1)
        o_local, q_in, s_own, d_all = _hgrn_chunk_local(*cur, lev, sel)
        rows = slice(ci * chunk, (ci + 1) * chunk)
        o_ref[rows, :] = _hgrn_chunk_finish(o_local, q_in, st, ng, zg_ref[rows, :]).astype(o_ref.dtype)
        st = st * d_all + s_own
    st_ref[...] = st

    @pl.when(tstep == pl.num_programs(2) - 1)
    def _():
        sf_ref[...] = st_ref[...].T


def _hgrn_prompt(zh, lb_raw, norm_g, layer, tables, *, tblk, chunk):
    b, t, w4 = zh.shape
    hd = w4 // 4
    dk = hd // HG_HEADS
    depth = lb_raw.shape[0]
    lev, sel, tri = tables
    col = lambda part: pl.BlockSpec((None, tblk, dk), lambda i, h, j: (i, j, part * HG_HEADS + h))
    return pl.pallas_call(
        functools.partial(_hgrn_prompt_kernel, chunk=chunk, n_chunks=tblk // chunk, layer=layer),
        grid=(b, HG_HEADS, t // tblk),
        in_specs=[col(0), col(1), col(2), col(3),
                  pl.BlockSpec((depth, dk), lambda i, h, j: (0, h)),
                  pl.BlockSpec((None, 1, dk), lambda i, h, j: (layer, 0, 0)),
                  _const_spec(lev.shape, lambda i, h, j: (0, 0)),
                  _const_spec(sel.shape, lambda i, h, j: (0, 0)),
                  _const_spec(tri.shape, lambda i, h, j: (0, 0))],
        out_specs=[pl.BlockSpec((None, tblk, dk), lambda i, h, j: (i, j, h)),
                   pl.BlockSpec((None, None, dk, dk), lambda i, h, j: (i, h, 0, 0))],
        out_shape=[jax.ShapeDtypeStruct((b, t, hd), BF16),
                   jax.ShapeDtypeStruct((b, HG_HEADS, dk, dk), F32)],
        scratch_shapes=[pltpu.VMEM((dk, dk), F32)],
        compiler_params=_params(("parallel", "parallel", "arbitrary")),
        name="hgrn_prompt",
    )(zh, zh, zh, zh, lb_raw, norm_g, lev, sel, tri)


def _hgrn_sample_kernel(zh_ref, lb_ref, ng_ref, lev_ref, sel_ref, tri_ref, s0_ref, o_ref, sf_ref,
                        *, nb, layer, t_valid):
    t_pad = zh_ref.shape[1]
    hd = o_ref.shape[2]
    dk = hd // HG_HEADS
    lev = lev_ref[...]
    sel = sel_ref[...]
    tri = tri_ref[...]
    ng = ng_ref[...]
    live = lax.broadcasted_iota(jnp.int32, (t_pad, dk), 0) < t_valid
    lbs = [_hgrn_lower_bound(lb_ref[:, h * dk:(h + 1) * dk], layer) for h in range(HG_HEADS)]

    def one_batch(bi, carry):
        for h in range(HG_HEADS):
            part = lambda p: zh_ref[bi, :, (p * HG_HEADS + h) * dk:(p * HG_HEADS + h + 1) * dk]
            gates = _hgrn_chunk_gates(part(0), part(1), part(2), lbs[h], tri, live)
            o_local, q_in, s_own, d_all = _hgrn_chunk_local(*gates, lev, sel)
            st = s0_ref[bi, h].T
            o_ref[bi, :, h * dk:(h + 1) * dk] = _hgrn_chunk_finish(o_local, q_in, st, ng, part(3))
            sf_ref[bi, h] = (st * d_all + s_own).T
        return carry

    lax.fori_loop(0, nb, one_batch, 0, unroll=2)


def _hgrn_sample(zh, lb_raw, norm_g, s0, layer, tables, *, nb, t_valid):
    b, t_pad, w4 = zh.shape
    hd = w4 // 4
    dk = hd // HG_HEADS
    depth = lb_raw.shape[0]
    lev, sel, tri = tables
    return pl.pallas_call(
        functools.partial(_hgrn_sample_kernel, nb=nb, layer=layer, t_valid=t_valid),
        grid=(b // nb,),
        in_specs=[pl.BlockSpec((nb, t_pad, w4), lambda i: (i, 0, 0)),
                  pl.BlockSpec((depth, hd), lambda i: (0, 0)),
                  pl.BlockSpec((None, 1, dk), lambda i: (layer, 0, 0)),
                  _const_spec(lev.shape, lambda i: (0, 0)),
                  _const_spec(sel.shape, lambda i: (0, 0)),
                  _const_spec(tri.shape, lambda i: (0, 0)),
                  pl.BlockSpec((None, nb, HG_HEADS, dk, dk), lambda i: (layer, i, 0, 0, 0))],
        out_specs=[pl.BlockSpec((nb, t_pad, hd), lambda i: (i, 0, 0)),
                   pl.BlockSpec((nb, HG_HEADS, dk, dk), lambda i: (i, 0, 0, 0))],
        out_shape=[jax.ShapeDtypeStruct((b, t_pad, hd), F32),
                   jax.ShapeDtypeStruct((b, HG_HEADS, dk, dk), F32)],
        compiler_params=_params(("parallel",)),
        name="hgrn_sample",
    )(zh, lb_raw, norm_g, lev, sel, tri, s0)


def _pad_rows(x, rows):
    return jnp.pad(x, ((0, 0), (0, rows - x.shape[1]), (0, 0)))


def _key_minor(cache):
    depth, b, n_past, h, hd = cache.shape
    return jnp.transpose(cache, (0, 1, 3, 4, 2)).reshape(depth, b, h * hd, n_past)


def kernel(x_prompt, x_sample, state_conv, cache_k_win, cache_v_win, state_hgrn, ln_ffn1, w_ffn1_gate, w_ffn1_up, w_ffn1_down, ln_mix, w_in, conv_dw_w, conv_dw_b, conv_ln_g, conv_ln_b, hg_lower_bounds, hg_norm_g, w_out, ln_ffn2, w_ffn2_gate, w_ffn2_up, w_ffn2_down, ln_final):
    bp, seq, d = x_prompt.shape
    bs, t_new, _ = x_sample.shape
    depth = w_in.shape[0]
    conv_dim = conv_dw_w.shape[2]
    att_dim = ATT_HEADS * ATT_HDIM
    hg_dim = hg_lower_bounds.shape[1]
    widths = (2 * conv_dim, att_dim, att_dim, att_dim, 4 * hg_dim)
    n_past = cache_k_win.shape[2]
    t_pad = SUBLANES
    keep = min(ATT_SPAN, seq)

    wg1, wu1, wd1 = (w.astype(BF16) for w in (w_ffn1_gate, w_ffn1_up, w_ffn1_down))
    wg2, wu2, wd2 = (w.astype(BF16) for w in (w_ffn2_gate, w_ffn2_up, w_ffn2_down))
    wi, wo = w_in.astype(BF16), w_out.astype(BF16)
    vec3 = lambda a: a.reshape(depth, 1, a.shape[-1])
    ln1, lnm, ln2 = vec3(ln_ffn1), vec3(ln_mix), vec3(ln_ffn2)
    dwb, clg, clb, hgn = vec3(conv_dw_b), vec3(conv_ln_g), vec3(conv_ln_b), vec3(hg_norm_g)
    lnf = ln_final.reshape(1, d)
    dww = jnp.pad(conv_dw_w, ((0, 0), (0, CONV_HALO - CONV_WIDTH), (0, 0)))
    cache_k, cache_v = _key_minor(cache_k_win), _key_minor(cache_v_win)
    halo_s = jnp.pad(state_conv, ((0, 0), (0, 0), (CONV_HALO - (CONV_WIDTH - 1), 0), (0, 0)))
    halo_p = jnp.zeros((bp, CONV_HALO, conv_dim), F32)

    bias_p = jnp.asarray(_prompt_bias_tables())
    bias_c, bias_n = (jnp.asarray(a) for a in _sample_bias_tables(n_past, t_new, t_pad))
    chunk_p = 128
    tabs_p = tuple(jnp.asarray(a, dt) for a, dt in zip(_hgrn_tables(chunk_p), (jnp.int32, BF16, BF16)))
    tabs_s = tuple(jnp.asarray(a, dt) for a, dt in zip(_hgrn_tables(t_pad), (jnp.int32, BF16, BF16)))

    hp = x_prompt.reshape(bp * seq, d)
    hs = x_sample.reshape(bs * t_new, d)
    outs = [[] for _ in range(8)]
    for l in range(depth):
        final = l == depth - 1
        x1, zc, q, k, v, zh = _dense_in(hp, l, ln1, wg1, wu1, wd1, lnm, wi, tm=512, f_chunk=1408, widths=widths)
        seq3 = lambda a: a.reshape(bp, seq, a.shape[-1])
        yc, tail = _conv_group(seq3(zc), halo_p, l, dww, dwb, clg, clb, nb=1, tt=512, out_dtype=BF16)
        ya = _attn_prompt(seq3(q), seq3(k), seq3(v), bias_p, span=ATT_SPAN)
        yh, sp = _hgrn_prompt(seq3(zh), hg_lower_bounds, hgn, l, tabs_p, tblk=1024, chunk=chunk_p)
        flat = lambda a: a.reshape(bp * seq, a.shape[-1])
        hp = _dense_out(x1, flat(yc), flat(ya), flat(yh), l, wo, ln2, wg2, wu2, wd2, lnf,
                        tm=512, f_chunk=1408, final=final)
        outs[0].append(tail[:, -(CONV_WIDTH - 1):])
        outs[2].append(seq3(k)[:, -keep:].reshape(bp, keep, ATT_HEADS, ATT_HDIM))
        outs[3].append(seq3(v)[:, -keep:].reshape(bp, keep, ATT_HEADS, ATT_HDIM))
        outs[6].append(sp)

        x1s, zcs, qs, ks, vs, zhs = _dense_in(hs, l, ln1, wg1, wu1, wd1, lnm, wi,
                                              tm=bs * t_new, f_chunk=1408, widths=widths)
        new3 = lambda a: _pad_rows(a.reshape(bs, t_new, a.shape[-1]), t_pad)
        ycs, us = _conv_group(new3(zcs), halo_s[l], l, dww, dwb, clg, clb, nb=bs, tt=t_pad, out_dtype=F32)
        yas = _attn_sample(new3(qs), new3(ks), new3(vs), cache_k, cache_v, l, bias_c, bias_n)
        yhs, ss = _hgrn_sample(new3(zhs), hg_lower_bounds, hgn, state_hgrn, l, tabs_s, nb=8, t_valid=t_new)
        unpad = lambda a: a[:, :t_new].reshape(bs * t_new, a.shape[-1])
        hs = _dense_out(x1s, unpad(ycs), unpad(yas), unpad(yhs), l, wo, ln2, wg2, wu2, wd2, lnf,
                        tm=bs * t_new, f_chunk=1408, final=final)
        outs[1].append(jnp.concatenate([state_conv[l][:, t_new:], us[:, :t_new]], axis=1))
        outs[4].append(ks.reshape(bs, t_new, ATT_HEADS, ATT_HDIM))
        outs[5].append(vs.reshape(bs, t_new, ATT_HEADS, ATT_HDIM))
        outs[7].append(ss)

    y_prompt = hp.reshape(bp, seq, d)
    y_sample = hs.reshape(bs, t_new, d)
    st = [jnp.stack(o) for o in outs]
    return (y_prompt, y_sample, st[0], st[1], st[2], st[3], st[4], st[5], st[6], st[7])
```

```python
import functools
import math

import numpy as np
import jax
import jax.numpy as jnp
from jax import lax
from jax.experimental import pallas as pl
from jax.experimental.pallas import tpu as pltpu

F32 = jnp.float32
BF16 = jnp.bfloat16

EPS = 1e-6
NEG_BIG = -1e30
CONV_WIDTH = 31
CONV_HALO = 32
ATT_HEADS = 4
ATT_HDIM = 64
ATT_BLK = 128
DILATIONS = (1, 4, 16)
ATT_SPAN = 2048
ATT_UNROLL = 8
HG_HEADS = 4
LANES = 128
SUBLANES = 8
VMEM_LIMIT = 56 * 1024 * 1024


def _params(sem, vmem=VMEM_LIMIT):
    return pltpu.CompilerParams(dimension_semantics=sem, vmem_limit_bytes=vmem)


def _const_spec(shape, index):
    return pl.BlockSpec(shape, index, pipeline_mode=pl.Buffered(1))


def _rms(x, g):
    ms = jnp.mean(x * x, axis=-1, keepdims=True)
    return x * lax.rsqrt(ms + EPS) * g


def _silu(x):
    return x * jax.nn.sigmoid(x)


def _dot(a, b):
    return jnp.dot(a, b, preferred_element_type=F32)


def _dot_nt(a, b):
    return lax.dot_general(a, b, (((1,), (1,)), ((), ())), preferred_element_type=F32)


def _dot_tn(a, b):
    return lax.dot_general(a, b, (((0,), (0,)), ((), ())), preferred_element_type=F32)


def _swiglu(h, wg_ref, wu_ref, wd_ref, f_chunk):
    ffn = wg_ref.shape[1]
    acc = None
    for c0 in range(0, ffn, f_chunk):
        g = _dot(h, wg_ref[:, c0:c0 + f_chunk])
        u = _dot(h, wu_ref[:, c0:c0 + f_chunk])
        a = (_silu(g) * u).astype(BF16)
        y = _dot(a, wd_ref[c0:c0 + f_chunk, :])
        acc = y if acc is None else acc + y
    return acc


def _ffn_kernel(x_ref, ln_ref, wg_ref, wu_ref, wd_ref, o_ref, *, f_chunk):
    x = x_ref[...]
    h = _rms(x, ln_ref[...]).astype(BF16)
    o_ref[...] = x + 0.5 * _swiglu(h, wg_ref, wu_ref, wd_ref, f_chunk)


def _ffn_half_step(x, layer, ln, wg, wu, wd, *, tm, f_chunk):
    n, d = x.shape
    ffn = wg.shape[2]
    row = pl.BlockSpec((tm, d), lambda i: (i, 0))
    return pl.pallas_call(
        functools.partial(_ffn_kernel, f_chunk=f_chunk),
        grid=(n // tm,),
        in_specs=[row, pl.BlockSpec((None, 1, d), lambda i: (layer, 0, 0)),
                  _const_spec((None, d, ffn), lambda i: (layer, 0, 0)),
                  _const_spec((None, d, ffn), lambda i: (layer, 0, 0)),
                  _const_spec((None, ffn, d), lambda i: (layer, 0, 0))],
        out_specs=row,
        out_shape=jax.ShapeDtypeStruct((n, d), F32),
        compiler_params=_params(("parallel",)),
        name="ffn_half_step",
    )(x, ln, wg, wu, wd)


def _proj_in_kernel(x_ref, ln_ref, wi_ref, zc_ref, q_ref, k_ref, v_ref, zh_ref):
    h = _rms(x_ref[...], ln_ref[...]).astype(BF16)
    c0 = 0
    for ref in (zc_ref, q_ref, k_ref, v_ref, zh_ref):
        w = ref.shape[1]
        ref[...] = _dot(h, wi_ref[:, c0:c0 + w])
        c0 += w


def _proj_in(x, layer, ln, wi, *, tm, widths):
    n, d = x.shape
    n_in = wi.shape[2]
    row = lambda w: pl.BlockSpec((tm, w), lambda i: (i, 0))
    return pl.pallas_call(
        _proj_in_kernel,
        grid=(n // tm,),
        in_specs=[row(d), pl.BlockSpec((None, 1, d), lambda i: (layer, 0, 0)),
                  _const_spec((None, d, n_in), lambda i: (layer, 0, 0))],
        out_specs=[row(w) for w in widths],
        out_shape=[jax.ShapeDtypeStruct((n, w), F32) for w in widths],
        compiler_params=_params(("parallel",)),
        name="proj_in",
    )(x, ln, wi)


def _dense_in(x, layer, ln1, wg, wu, wd, lnm, wi, *, tm, f_chunk, widths):
    x1 = _ffn_half_step(x, layer, ln1, wg, wu, wd, tm=tm, f_chunk=f_chunk)
    return (x1,) + tuple(_proj_in(x1, layer, lnm, wi, tm=tm, widths=widths))


def _dense_out_kernel(x_ref, yc_ref, ya_ref, yh_ref, wo_ref, ln2_ref, wg_ref, wu_ref, wd_ref, lnf_ref,
                      o_ref, *, f_chunk, final):
    y = jnp.concatenate([r[...].astype(BF16) for r in (yc_ref, ya_ref, yh_ref)], axis=-1)
    x2 = x_ref[...] + _dot(y, wo_ref[...])
    h = _rms(x2, ln2_ref[...]).astype(BF16)
    x3 = x2 + 0.5 * _swiglu(h, wg_ref, wu_ref, wd_ref, f_chunk)
    if final:
        x3 = _rms(x3, lnf_ref[...])
    o_ref[...] = x3


def _dense_out(x1, yc, ya, yh, layer, wo, ln2, wg, wu, wd, lnf, *, tm, f_chunk, final):
    n, d = x1.shape
    ffn = wg.shape[2]
    row = lambda w: pl.BlockSpec((tm, w), lambda i: (i, 0))
    vec = pl.BlockSpec((None, 1, d), lambda i: (layer, 0, 0))
    return pl.pallas_call(
        functools.partial(_dense_out_kernel, f_chunk=f_chunk, final=final),
        grid=(n // tm,),
        in_specs=[row(d), row(yc.shape[1]), row(ya.shape[1]), row(yh.shape[1]),
                  _const_spec((None, d, d), lambda i: (layer, 0, 0)),
                  vec,
                  _const_spec((None, d, ffn), lambda i: (layer, 0, 0)),
                  _const_spec((None, d, ffn), lambda i: (layer, 0, 0)),
                  _const_spec((None, ffn, d), lambda i: (layer, 0, 0)),
                  pl.BlockSpec((1, d), lambda i: (0, 0))],
        out_specs=row(d),
        out_shape=jax.ShapeDtypeStruct((n, d), F32),
        compiler_params=_params(("parallel",)),
        name="dense_out",
    )(x1, yc, ya, yh, wo, ln2, wg, wu, wd, lnf)


def _conv_tile(z, ext_ref, b, dww_ref, dwb, lng, lnb):
    tt, c2 = z.shape
    c = c2 // 2
    lo = CONV_HALO - (CONV_WIDTH - 1)
    u = z[:, :c] * jax.nn.sigmoid(z[:, c:])
    ext_ref[b, CONV_HALO:CONV_HALO + tt, :] = u
    y = None
    for r in range(SUBLANES):
        z_r = None
        for o in range(r, CONV_HALO + 1, SUBLANES):
            if o < lo:
                continue
            term = ext_ref[b, o - r:o - r + tt + SUBLANES, :] * dww_ref[o - lo:o - lo + 1, :]
            z_r = term if z_r is None else z_r + term
        part = z_r[r:r + tt, :]
        y = part if y is None else y + part
    y = y + dwb
    yc = y - jnp.mean(y, axis=-1, keepdims=True)
    yn = yc * lax.rsqrt(jnp.mean(yc * yc, axis=-1, keepdims=True) + EPS)
    carry = ext_ref[b, tt:tt + CONV_HALO, :]
    ext_ref[b, 0:CONV_HALO, :] = carry
    return _silu(yn * lng + lnb), u


def _conv_kernel(zc_ref, halo_ref, dww_ref, dwb_ref, lng_ref, lnb_ref, y_ref, tail_ref, ext_ref, *, nb, tt, tail):
    c = y_ref.shape[-1]

    def one_batch(b):
        @pl.when(pl.program_id(1) == 0)
        def _():
            ext_ref[b, 0:CONV_HALO, :] = halo_ref[b]
            ext_ref[b, CONV_HALO + tt:CONV_HALO + tt + SUBLANES, :] = jnp.zeros((SUBLANES, c), F32)

        y, u = _conv_tile(zc_ref[b], ext_ref, b, dww_ref, dwb_ref[...], lng_ref[...], lnb_ref[...])
        y_ref[b] = y.astype(y_ref.dtype)
        tail_ref[b] = u[tt - tail:, :]

    if nb == 1:
        one_batch(0)
    else:
        def body(b, carry):
            one_batch(b)
            return carry
        lax.fori_loop(0, nb, body, 0)


def _conv_group(zc, halo, layer, dww, dwb, lng, lnb, *, nb, tt, out_dtype):
    b, t, c2 = zc.shape
    c = c2 // 2
    tail = min(CONV_HALO, tt)
    vec = pl.BlockSpec((None, 1, c), lambda i, j: (layer, 0, 0))
    return pl.pallas_call(
        functools.partial(_conv_kernel, nb=nb, tt=tt, tail=tail),
        grid=(b // nb, t // tt),
        in_specs=[pl.BlockSpec((nb, tt, c2), lambda i, j: (i, j, 0)),
                  pl.BlockSpec((nb, CONV_HALO, c), lambda i, j: (i, 0, 0)),
                  pl.BlockSpec((None, CONV_HALO, c), lambda i, j: (layer, 0, 0)),
                  vec, vec, vec],
        out_specs=[pl.BlockSpec((nb, tt, c), lambda i, j: (i, j, 0)),
                   pl.BlockSpec((nb, tail, c), lambda i, j: (i, 0, 0))],
        out_shape=[jax.ShapeDtypeStruct((b, t, c), out_dtype),
                   jax.ShapeDtypeStruct((b, tail, c), F32)],
        scratch_shapes=[pltpu.VMEM((nb, CONV_HALO + tt + SUBLANES, c), F32)],
        compiler_params=_params(("parallel", "arbitrary")),
        name="conv_group",
    )(zc, halo, dww, dwb, lng, lnb)


def _alibi_slopes():
    return [2.0 ** (-8.0 * (h + 1) / ATT_HEADS) for h in range(ATT_HEADS)]


def _prompt_bias_tables():
    qi = np.arange(ATT_BLK)[:, None]
    ki = np.arange(2 * ATT_BLK)[None, :]
    j = qi - ki + ATT_BLK
    tabs = []
    for dil in DILATIONS:
        for first in (False, True):
            valid = (j >= 0) & (j <= ATT_BLK) & ((not first) | (ki >= ATT_BLK))
            rows = []
            for s in _alibi_slopes():
                bias = (-np.float32(s)) * (j * dil).astype(np.float32)
                rows.append(np.where(valid, bias, np.float32(NEG_BIG)).astype(np.float32))
            tabs.append(np.concatenate(rows, axis=0))
    return np.stack(tabs, axis=0)


def _head_pair_split(x, lo_mask):
    zero = jnp.zeros_like(x)
    return jnp.concatenate([jnp.where(lo_mask, x, zero), jnp.where(lo_mask, zero, x)], axis=0)


def _attn_prompt_kernel(bias_ref, q_ref, kp_ref, kc_ref, vp_ref, vc_ref, o_ref,
                        qs_ref, ks_ref, vs_ref, os_ref, ls_ref, *, span):
    j = pl.program_id(1)
    nslab = qs_ref.shape[0]
    scale = ATT_HDIM ** -0.5
    for sl in range(nslab):
        cols = slice(sl * LANES, (sl + 1) * LANES)
        qs_ref[sl] = q_ref[:, cols] * scale
        ks_ref[sl, 0:span, :] = kp_ref[:, cols]
        ks_ref[sl, span:2 * span, :] = kc_ref[:, cols]
        vs_ref[sl, 0:span, :] = vp_ref[:, cols]
        vs_ref[sl, span:2 * span, :] = vc_ref[:, cols]

    lane = lax.broadcasted_iota(jnp.int32, (ATT_BLK, LANES), 1)
    lo_mask = lane < ATT_HDIM
    n_blocks = span // ATT_BLK

    order = sorted(range(len(DILATIONS)), key=lambda b: -DILATIONS[b])
    for pos, br in enumerate(order):
        dil = DILATIONS[br]

        def rows(start, size, dil=dil):
            if dil == 1:
                return pl.ds(pl.multiple_of(start, ATT_BLK), size)
            return pl.ds(start, size, stride=dil)

        def scores(i, dil=dil, rows=rows):
            q0 = (i // dil) * (ATT_BLK * dil) + i % dil
            k0 = span + q0 - ATT_BLK * dil
            s_parts, v_parts = [], []
            for sl in range(nslab):
                ql = qs_ref[sl, rows(q0, ATT_BLK), :]
                kl = ks_ref[sl, rows(k0, 2 * ATT_BLK), :]
                v_parts.append(vs_ref[sl, rows(k0, 2 * ATT_BLK), :].astype(BF16))
                lhs = _head_pair_split(ql, lo_mask).astype(BF16)
                s_parts.append(_dot_nt(lhs, kl.astype(BF16)))
            return jnp.concatenate(s_parts, axis=0), jnp.concatenate(v_parts, axis=1)

        def finish(i, s, v, br=br, dil=dil, pos=pos, rows=rows):
            q0 = (i // dil) * (ATT_BLK * dil) + i % dil
            first = jnp.logical_and(j == 0, i // dil == 0).astype(jnp.int32)
            bias = bias_ref[2 * br + first]
            s = jnp.where(bias > 0.5 * NEG_BIG, s + bias, NEG_BIG)
            m = jnp.max(s, axis=-1, keepdims=True)
            p = jnp.exp(s - m)
            l = jnp.sum(p, axis=-1, keepdims=True)
            pv = _dot(p.astype(BF16), v)
            lse_all = m + jnp.log(l)
            for sl in range(nslab):
                cols = slice(sl * LANES, (sl + 1) * LANES)
                ra = slice((2 * sl) * ATT_BLK, (2 * sl + 1) * ATT_BLK)
                rb = slice((2 * sl + 1) * ATT_BLK, (2 * sl + 2) * ATT_BLK)
                pair = lambda x: jnp.where(lo_mask, jnp.broadcast_to(x[ra], (ATT_BLK, LANES)),
                                           jnp.broadcast_to(x[rb], (ATT_BLK, LANES)))
                o_new = jnp.where(lo_mask, pv[ra, cols], pv[rb, cols]) / pair(l)
                lse_new = pair(lse_all)
                out_rows = rows(q0, ATT_BLK)
                if pos == 0:
                    o_tot, lse_tot = o_new, lse_new
                else:
                    o_old = os_ref[sl, out_rows, :]
                    lse_old = ls_ref[sl, out_rows, :]
                    mx = jnp.maximum(lse_old, lse_new)
                    e_old = jnp.exp(lse_old - mx)
                    e_new = jnp.exp(lse_new - mx)
                    den = e_old + e_new
                    o_tot = (e_old * o_old + e_new * o_new) / den
                    lse_tot = mx + jnp.log(den)
                os_ref[sl, out_rows, :] = o_tot
                ls_ref[sl, out_rows, :] = lse_tot

        def trip(t, carry, scores=scores, finish=finish):
            base = t * ATT_UNROLL
            ahead = scores(base)
            for u in range(ATT_UNROLL):
                s, v = ahead
                if u + 1 < ATT_UNROLL:
                    ahead = scores(base + u + 1)
                finish(base + u, s, v)
            return carry

        lax.fori_loop(0, n_blocks // ATT_UNROLL, trip, 0)

    o_ref[...] = jnp.concatenate([os_ref[sl] for sl in range(nslab)], axis=-1).astype(o_ref.dtype)


def _attn_prompt(q, k, v, bias_tabs, *, span):
    b, t, w = q.shape
    nslab = w // LANES
    cur = pl.BlockSpec((None, span, w), lambda i, j: (i, j, 0))
    prev = pl.BlockSpec((None, span, w), lambda i, j: (i, jnp.maximum(j - 1, 0), 0))
    slab = lambda rows: pltpu.VMEM((nslab, rows, LANES), F32)
    return pl.pallas_call(
        functools.partial(_attn_prompt_kernel, span=span),
        grid=(b, t // span),
        in_specs=[_const_spec(bias_tabs.shape, lambda i, j: (0, 0, 0)), cur, prev, cur, prev, cur],
        out_specs=cur,
        out_shape=jax.ShapeDtypeStruct((b, t, w), BF16),
        scratch_shapes=[slab(span), slab(2 * span), slab(2 * span), slab(span), slab(span)],
        compiler_params=_params(("parallel", "arbitrary")),
        name="attn_prompt",
    )(bias_tabs, q, k, k, v, v)


def _sample_bias_tables(n_past, t_new, t_pad):
    slopes = _alibi_slopes()
    n_keys = n_past + t_pad
    tabs = np.full((len(DILATIONS), ATT_HEADS * t_pad, n_keys), NEG_BIG, np.float32)
    for br, dil in enumerate(DILATIONS):
        for h, s in enumerate(slopes):
            for t in range(t_pad):
                tq = min(t, t_new - 1)
                for jj in range(ATT_BLK + 1):
                    idx = n_past + tq - jj * dil
                    if idx < 0:
                        continue
                    tabs[br, h * t_pad + t, idx] = -np.float32(s) * np.float32(jj * dil)
    return tabs[:, :, :n_past], tabs[:, :, n_past:]


def _attn_sample_kernel(bc_ref, bn_ref, q_ref, kn_ref, vn_ref, kc_ref, vc_ref, o_ref):
    t_pad, w = q_ref.shape
    scale = ATT_HDIM ** -0.5
    q = q_ref[...] * scale
    lane = lax.broadcasted_iota(jnp.int32, (t_pad, w), 1)
    head_of_lane = lane // ATT_HDIM
    zero = jnp.zeros_like(q)
    lhs = jnp.concatenate([jnp.where(head_of_lane == h, q, zero) for h in range(ATT_HEADS)], axis=0).astype(BF16)
    s_c = _dot(lhs, kc_ref[...].astype(BF16))
    s_n = _dot_nt(lhs, kn_ref[...].astype(BF16))
    vc = vc_ref[...].astype(BF16)
    vn = vn_ref[...].astype(BF16)
    pcs, pns, ls, lses = [], [], [], []
    for br in range(len(DILATIONS)):
        bc = bc_ref[br]
        bn = bn_ref[br]
        sc = jnp.where(bc > 0.5 * NEG_BIG, s_c + bc, NEG_BIG)
        sn = jnp.where(bn > 0.5 * NEG_BIG, s_n + bn, NEG_BIG)
        m = jnp.maximum(jnp.max(sc, axis=-1, keepdims=True), jnp.max(sn, axis=-1, keepdims=True))
        pc = jnp.exp(sc - m)
        pn = jnp.exp(sn - m)
        l = jnp.sum(pc, axis=-1, keepdims=True) + jnp.sum(pn, axis=-1, keepdims=True)
        pcs.append(pc.astype(BF16))
        pns.append(pn.astype(BF16))
        ls.append(l)
        lses.append(m + jnp.log(l))
    nq = ATT_HEADS * t_pad
    pv = _dot_nt(jnp.concatenate(pcs, axis=0), vc) + _dot(jnp.concatenate(pns, axis=0), vn)
    outs = [pv[br * nq:(br + 1) * nq] / ls[br] for br in range(len(DILATIONS))]
    mx = functools.reduce(jnp.maximum, lses)
    es = [jnp.exp(x - mx) for x in lses]
    den = functools.reduce(lambda a, b: a + b, es)
    o = functools.reduce(lambda a, b: a + b, [e * x for e, x in zip(es, outs)]) / den
    res = zero
    for h in range(ATT_HEADS):
        res = jnp.where(head_of_lane == h, o[h * t_pad:(h + 1) * t_pad, :], res)
    o_ref[...] = res.astype(o_ref.dtype)


def _attn_sample(q, k, v, cache_k, cache_v, layer, bias_c, bias_n):
    b, t_pad, w = q.shape
    n_past = cache_k.shape[3]
    new = pl.BlockSpec((None, t_pad, w), lambda i: (i, 0, 0))
    cache = pl.BlockSpec((None, None, w, n_past), lambda i: (layer, i, 0, 0))
    return pl.pallas_call(
        _attn_sample_kernel,
        grid=(b,),
        in_specs=[_const_spec(bias_c.shape, lambda i: (0, 0, 0)),
                  _const_spec(bias_n.shape, lambda i: (0, 0, 0)),
                  new, new, new, cache, cache],
        out_specs=new,
        out_shape=jax.ShapeDtypeStruct((b, t_pad, w), F32),
        compiler_params=_params(("parallel",)),
        name="attn_sample",
    )(bias_c, bias_n, q, k, v, cache_k, cache_v)


def _hgrn_tables(chunk):
    sub = min(chunk, LANES)
    r = np.arange(sub)[:, None]
    c = np.arange(sub)[None, :]
    x = np.bitwise_xor(r, c)
    hb = np.zeros_like(x)
    for bit in range(1, 16):
        hb = np.where(x >> bit > 0, bit, hb)
    lev = np.where(c > r, -1, np.where(r // SUBLANES == c // SUBLANES, 0, hb)).astype(np.int32)
    lev = np.tile(lev, (chunk // sub, 1))
    sel = np.zeros((SUBLANES * LANES, sub), np.float32)
    for s in range(SUBLANES):
        sel[s * LANES:(s + 1) * LANES, s::SUBLANES] = 1.0
    rr = np.arange(chunk)
    tri = (rr[None, :] <= rr[:, None]).astype(np.float32)
    return lev, sel, tri


def _split3(x):
    hi = x.astype(BF16)
    r1 = x - hi.astype(F32)
    mid = r1.astype(BF16)
    lo = (r1 - mid.astype(F32)).astype(BF16)
    return hi, mid, lo


def _group_row(x, s):
    c, l = x.shape
    x3 = x.reshape(c // SUBLANES, SUBLANES, l)
    return jnp.broadcast_to(x3[:, s:s + 1, :], x3.shape).reshape(c, l)


def _block_last(x, m):
    c, l = x.shape
    x3 = x.reshape(c // m, m, l)
    return jnp.broadcast_to(x3[:, m - 1:m, :], x3.shape).reshape(c, l)


def _hgrn_lower_bound(raw, layer):
    e = jnp.exp(raw - jnp.max(raw, axis=0, keepdims=True))
    sm = e / jnp.sum(e, axis=0, keepdims=True)
    lb = jnp.zeros_like(sm[0:1])
    for i in range(1, layer + 1):
        lb = lb + sm[i:i + 1]
    return lb


def _hgrn_chunk_gates(zq, zf, zi, lb, tri, live):
    dk = zq.shape[1]
    q = _silu(zq)
    f = lb + (1.0 - lb) * jax.nn.sigmoid(zf)
    g = jnp.log2(f)
    k = 1.0 - f
    if live is not None:
        g = jnp.where(live, g, 0.0)
        k = jnp.where(live, k, 0.0)
    b3 = _dot(tri, jnp.concatenate(_split3(g), axis=1))
    b = b3[:, :dk] + b3[:, dk:2 * dk] + b3[:, 2 * dk:]
    return q, k, b, zi.astype(BF16)


def _hgrn_chunk_local(q, k, b, vb, lev, sel):
    chunk, dk = q.shape
    sub = lev.shape[1]

    vals = []
    for s in range(SUBLANES):
        decay = jnp.exp2(jnp.minimum(b - _group_row(b, s), 0.0))
        vals.append((q * decay * _group_row(k, s)).astype(BF16))
    a = jnp.where(lev == 0, _dot(jnp.concatenate(vals, axis=1), sel), 0.0)
    subs = [slice(r0, r0 + sub) for r0 in range(0, chunk, sub)]
    for bit in range(3, int(math.log2(sub))):
        m = 1 << bit
        b_end = _block_last(b, m)
        b_prev = jnp.concatenate([jnp.zeros((m, dk), F32), b_end[:chunk - m]], axis=0)
        qd = (q * jnp.exp2(b - b_prev)).astype(BF16)
        kd = (k * jnp.exp2(b_end - b)).astype(BF16)
        a_l = jnp.concatenate([_dot_nt(qd[rs], kd[rs]) for rs in subs], axis=0)
        a = jnp.where(lev == bit, a_l, a)
    ab = a.astype(BF16)
    o_parts = []
    for i, rs in enumerate(subs):
        if i == 0:
            o_parts.append(_dot(ab[rs], vb[rs]))
            continue
        b_piv = b[rs.start - 1:rs.start, :]
        qd = (q[rs] * jnp.exp2(b[rs] - b_piv)).astype(BF16)
        kd = (k[:rs.start] * jnp.exp2(b_piv - b[:rs.start])).astype(BF16)
        a_row = jnp.concatenate([_dot_nt(qd, kd).astype(BF16), ab[rs]], axis=1)
        o_parts.append(_dot(a_row, vb[:rs.stop]))
    o = jnp.concatenate(o_parts, axis=0) if len(o_parts) > 1 else o_parts[0]
    q_in = (q * jnp.exp2(b)).astype(BF16)
    b_last = b[chunk - 1:chunk, :]
    kd = (k * jnp.exp2(b_last - b)).astype(BF16)
    return o, q_in, _dot_tn(vb, kd), jnp.exp2(b_last)


def _hgrn_chunk_finish(o_local, q_in, st, ng, zg):
    o = o_local + _dot_nt(q_in, st.astype(BF16))
    on = o * lax.rsqrt(jnp.mean(o * o, axis=-1, keepdims=True) + EPS) * ng
    return on * _silu(zg)


def _hgrn_sample_kernel(zh_ref, lb_ref, ng_ref, lev_ref, sel_ref, tri_ref, s0_ref, o_ref, sf_ref,
                        *, nb, layer, t_valid):
    t_pad = zh_ref.shape[1]
    hd = o_ref.shape[2]
    dk = hd // HG_HEADS
    lev = lev_ref[...]
    sel = sel_ref[...]
    tri = tri_ref[...]
    ng = ng_ref[...]
    live = lax.broadcasted_iota(jnp.int32, (t_pad, dk), 0) < t_valid
    lbs = [_hgrn_lower_bound(lb_ref[:, h * dk:(h + 1) * dk], layer) for h in range(HG_HEADS)]

    def one_batch(bi, carry):
        for h in range(HG_HEADS):
            part = lambda p: zh_ref[bi, :, (p * HG_HEADS + h) * dk:(p * HG_HEADS + h + 1) * dk]
            gates = _hgrn_chunk_gates(part(0), part(1), part(2), lbs[h], tri, live)
            o_local, q_in, s_own, d_all = _hgrn_chunk_local(*gates, lev, sel)
            st = s0_ref[bi, h].T
            o_ref[bi, :, h * dk:(h + 1) * dk] = _hgrn_chunk_finish(o_local, q_in, st, ng, part(3))
            sf_ref[bi, h] = (st * d_all + s_own).T
        return carry

    lax.fori_loop(0, nb, one_batch, 0, unroll=2)


def _hgrn_sample(zh, lb_raw, norm_g, s0, layer, tables, *, nb, t_valid):
    b, t_pad, w4 = zh.shape
    hd = w4 // 4
    dk = hd // HG_HEADS
    depth = lb_raw.shape[0]
    lev, sel, tri = tables
    return pl.pallas_call(
        functools.partial(_hgrn_sample_kernel, nb=nb, layer=layer, t_valid=t_valid),
        grid=(b // nb,),
        in_specs=[pl.BlockSpec((nb, t_pad, w4), lambda i: (i, 0, 0)),
                  pl.BlockSpec((depth, hd), lambda i: (0, 0)),
                  pl.BlockSpec((None, 1, dk), lambda i: (layer, 0, 0)),
                  _const_spec(lev.shape, lambda i: (0, 0)),
                  _const_spec(sel.shape, lambda i: (0, 0)),
                  _const_spec(tri.shape, lambda i: (0, 0)),
                  pl.BlockSpec((None, nb, HG_HEADS, dk, dk), lambda i: (layer, i, 0, 0, 0))],
        out_specs=[pl.BlockSpec((nb, t_pad, hd), lambda i: (i, 0, 0)),
                   pl.BlockSpec((nb, HG_HEADS, dk, dk), lambda i: (i, 0, 0, 0))],
        out_shape=[jax.ShapeDtypeStruct((b, t_pad, hd), F32),
                   jax.ShapeDtypeStruct((b, HG_HEADS, dk, dk), F32)],
        compiler_params=_params(("parallel",)),
        name="hgrn_sample",
    )(zh, lb_raw, norm_g, lev, sel, tri, s0)


def _proj_mix_kernel(x_ref, ln_ref, wi_ref, dww_ref, dwb_ref, lng_ref, lnb_ref, lb_ref, ng_ref,
                     lev_ref, sel_ref, tri_ref,
                     yc_ref, q_ref, k_ref, v_ref, yh_ref, tail_ref, sf_ref,
                     ext_ref, st_ref, *, layer, chunk):
    tstep = pl.program_id(1)
    tt = x_ref.shape[0]
    conv_c = yc_ref.shape[1]
    att = q_ref.shape[1]
    hd = yh_ref.shape[1]
    dk = hd // HG_HEADS
    n_chunks = tt // chunk

    @pl.when(tstep == 0)
    def _():
        ext_ref[0, 0:CONV_HALO, :] = jnp.zeros((CONV_HALO, conv_c), F32)
        ext_ref[0, CONV_HALO + tt:CONV_HALO + tt + SUBLANES, :] = jnp.zeros((SUBLANES, conv_c), F32)
        st_ref[...] = jnp.zeros_like(st_ref)

    h = _rms(x_ref[...], ln_ref[...]).astype(BF16)
    c0 = 2 * conv_c
    zc = _dot(h, wi_ref[:, 0:c0])
    zh = _dot(h, wi_ref[:, c0 + 3 * att:c0 + 3 * att + 4 * hd])
    yc, u = _conv_tile(zc, ext_ref, 0, dww_ref, dwb_ref[...], lng_ref[...], lnb_ref[...])
    yc_ref[...] = yc.astype(yc_ref.dtype)
    tail_ref[...] = u[tt - CONV_HALO:, :]
    for i, ref in enumerate((q_ref, k_ref, v_ref)):
        ref[...] = _dot(h, wi_ref[:, c0 + i * att:c0 + (i + 1) * att])

    lev = lev_ref[...]
    sel = sel_ref[...]
    tri = tri_ref[...]
    ng = ng_ref[...]
    lbs = [_hgrn_lower_bound(lb_ref[:, hh * dk:(hh + 1) * dk], layer) for hh in range(HG_HEADS)]

    def part(u_idx, p):
        hh, c = divmod(u_idx, n_chunks)
        return zh[c * chunk:(c + 1) * chunk, (p * HG_HEADS + hh) * dk:(p * HG_HEADS + hh + 1) * dk]

    def gates(u_idx):
        return _hgrn_chunk_gates(part(u_idx, 0), part(u_idx, 1), part(u_idx, 2), lbs[u_idx // n_chunks], tri, None)

    ahead = gates(0)
    st = None
    for u_idx in range(HG_HEADS * n_chunks):
        hh, c = divmod(u_idx, n_chunks)
        cur = ahead
        if u_idx + 1 < HG_HEADS * n_chunks:
            ahead = gates(u_idx + 1)
        if c == 0:
            st = st_ref[hh]
        o_local, q_in, s_own, d_all = _hgrn_chunk_local(*cur, lev, sel)
        o = _hgrn_chunk_finish(o_local, q_in, st, ng, part(u_idx, 3))
        yh_ref[c * chunk:(c + 1) * chunk, hh * dk:(hh + 1) * dk] = o.astype(yh_ref.dtype)
        st = st * d_all + s_own
        if c == n_chunks - 1:
            st_ref[hh] = st

    @pl.when(tstep == pl.num_programs(1) - 1)
    def _():
        for hh in range(HG_HEADS):
            sf_ref[hh] = st_ref[hh].T


def _proj_mix(x, layer, ln, wi, dww, dwb, lng, lnb, lb_raw, norm_g, tables, *, batch, tm, chunk, widths):
    n, d = x.shape
    n_in = wi.shape[2]
    conv_c = widths[0] // 2
    att = widths[1]
    hd = widths[4] // 4
    dk = hd // HG_HEADS
    depth = lb_raw.shape[0]
    nt = n // batch // tm
    lev, sel, tri = tables
    row = lambda w: pl.BlockSpec((tm, w), lambda b, j: (b * nt + j, 0))
    cvec = pl.BlockSpec((None, 1, conv_c), lambda b, j: (layer, 0, 0))
    const2 = lambda a: _const_spec(a.shape, lambda b, j: (0, 0))
    return pl.pallas_call(
        functools.partial(_proj_mix_kernel, layer=layer, chunk=chunk),
        grid=(batch, nt),
        in_specs=[row(d), pl.BlockSpec((None, 1, d), lambda b, j: (layer, 0, 0)),
                  _const_spec((None, d, n_in), lambda b, j: (layer, 0, 0)),
                  pl.BlockSpec((None, CONV_HALO, conv_c), lambda b, j: (layer, 0, 0)),
                  cvec, cvec, cvec,
                  pl.BlockSpec((depth, hd), lambda b, j: (0, 0)),
                  pl.BlockSpec((None, 1, dk), lambda b, j: (layer, 0, 0)),
                  const2(lev), const2(sel), const2(tri)],
        out_specs=[row(conv_c), row(att), row(att), row(att), row(hd),
                   pl.BlockSpec((None, CONV_HALO, conv_c), lambda b, j: (b, 0, 0)),
                   pl.BlockSpec((None, HG_HEADS, dk, dk), lambda b, j: (b, 0, 0, 0))],
        out_shape=[jax.ShapeDtypeStruct((n, conv_c), BF16),
                   jax.ShapeDtypeStruct((n, att), F32),
                   jax.ShapeDtypeStruct((n, att), F32),
                   jax.ShapeDtypeStruct((n, att), F32),
                   jax.ShapeDtypeStruct((n, hd), BF16),
                   jax.ShapeDtypeStruct((batch, CONV_HALO, conv_c), F32),
                   jax.ShapeDtypeStruct((batch, HG_HEADS, dk, dk), F32)],
        scratch_shapes=[pltpu.VMEM((1, CONV_HALO + tm + SUBLANES, conv_c), F32),
                        pltpu.VMEM((HG_HEADS, dk, dk), F32)],
        compiler_params=_params(("parallel", "arbitrary")),
        name="proj_mix",
    )(x, ln, wi, dww, dwb, lng, lnb, lb_raw, norm_g, lev, sel, tri)


def _pad_rows(x, rows):
    return jnp.pad(x, ((0, 0), (0, rows - x.shape[1]), (0, 0)))


def _key_minor(cache):
    depth, b, n_past, h, hd = cache.shape
    return jnp.transpose(cache, (0, 1, 3, 4, 2)).reshape(depth, b, h * hd, n_past)


def kernel(x_prompt, x_sample, state_conv, cache_k_win, cache_v_win, state_hgrn, ln_ffn1, w_ffn1_gate, w_ffn1_up, w_ffn1_down, ln_mix, w_in, conv_dw_w, conv_dw_b, conv_ln_g, conv_ln_b, hg_lower_bounds, hg_norm_g, w_out, ln_ffn2, w_ffn2_gate, w_ffn2_up, w_ffn2_down, ln_final):
    bp, seq, d = x_prompt.shape
    bs, t_new, _ = x_sample.shape
    depth = w_in.shape[0]
    conv_dim = conv_dw_w.shape[2]
    att_dim = ATT_HEADS * ATT_HDIM
    hg_dim = hg_lower_bounds.shape[1]
    widths = (2 * conv_dim, att_dim, att_dim, att_dim, 4 * hg_dim)
    n_past = cache_k_win.shape[2]
    t_pad = SUBLANES
    keep = min(ATT_SPAN, seq)

    wg1, wu1, wd1 = (w.astype(BF16) for w in (w_ffn1_gate, w_ffn1_up, w_ffn1_down))
    wg2, wu2, wd2 = (w.astype(BF16) for w in (w_ffn2_gate, w_ffn2_up, w_ffn2_down))
    wi, wo = w_in.astype(BF16), w_out.astype(BF16)
    vec3 = lambda a: a.reshape(depth, 1, a.shape[-1])
    ln1, lnm, ln2 = vec3(ln_ffn1), vec3(ln_mix), vec3(ln_ffn2)
    dwb, clg, clb, hgn = vec3(conv_dw_b), vec3(conv_ln_g), vec3(conv_ln_b), vec3(hg_norm_g)
    lnf = ln_final.reshape(1, d)
    dww = jnp.pad(conv_dw_w, ((0, 0), (0, CONV_HALO - CONV_WIDTH), (0, 0)))
    cache_k, cache_v = _key_minor(cache_k_win), _key_minor(cache_v_win)
    halo_s = jnp.pad(state_conv, ((0, 0), (0, 0), (CONV_HALO - (CONV_WIDTH - 1), 0), (0, 0)))

    bias_p = jnp.asarray(_prompt_bias_tables())
    bias_c, bias_n = (jnp.asarray(a) for a in _sample_bias_tables(n_past, t_new, t_pad))
    chunk_p = 128
    tabs_p = tuple(jnp.asarray(a, dt) for a, dt in zip(_hgrn_tables(chunk_p), (jnp.int32, BF16, BF16)))
    tabs_s = tuple(jnp.asarray(a, dt) for a, dt in zip(_hgrn_tables(t_pad), (jnp.int32, BF16, BF16)))

    hp = x_prompt.reshape(bp * seq, d)
    hs = x_sample.reshape(bs * t_new, d)
    outs = [[] for _ in range(8)]
    for l in range(depth):
        final = l == depth - 1
        x1 = _ffn_half_step(hp, l, ln1, wg1, wu1, wd1, tm=512, f_chunk=1408)
        yc, q, k, v, yh, tail, sp = _proj_mix(x1, l, lnm, wi, dww, dwb, clg, clb, hg_lower_bounds, hgn, tabs_p,
                                              batch=bp, tm=512, chunk=chunk_p, widths=widths)
        seq3 = lambda a: a.reshape(bp, seq, a.shape[-1])
        ya = _attn_prompt(seq3(q), seq3(k), seq3(v), bias_p, span=ATT_SPAN)
        hp = _dense_out(x1, yc, ya.reshape(bp * seq, att_dim), yh, l, wo, ln2, wg2, wu2, wd2, lnf,
                        tm=512, f_chunk=1408, final=final)
        outs[0].append(tail[:, -(CONV_WIDTH - 1):])
        outs[2].append(seq3(k)[:, -keep:].reshape(bp, keep, ATT_HEADS, ATT_HDIM))
        outs[3].append(seq3(v)[:, -keep:].reshape(bp, keep, ATT_HEADS, ATT_HDIM))
        outs[6].append(sp)

        x1s, zcs, qs, ks, vs, zhs = _dense_in(hs, l, ln1, wg1, wu1, wd1, lnm, wi,
                                              tm=bs * t_new, f_chunk=1408, widths=widths)
        new3 = lambda a: _pad_rows(a.reshape(bs, t_new, a.shape[-1]), t_pad)
        ycs, us = _conv_group(new3(zcs), halo_s[l], l, dww, dwb, clg, clb, nb=bs, tt=t_pad, out_dtype=F32)
        yas = _attn_sample(new3(qs), new3(ks), new3(vs), cache_k, cache_v, l, bias_c, bias_n)
        yhs, ss = _hgrn_sample(new3(zhs), hg_lower_bounds, hgn, state_hgrn, l, tabs_s, nb=8, t_valid=t_new)
        unpad = lambda a: a[:, :t_new].reshape(bs * t_new, a.shape[-1])
        hs = _dense_out(x1s, unpad(ycs), unpad(yas), unpad(yhs), l, wo, ln2, wg2, wu2, wd2, lnf,
                        tm=bs * t_new, f_chunk=1408, final=final)
        outs[1].append(jnp.concatenate([state_conv[l][:, t_new:], us[:, :t_new]], axis=1))
        outs[4].append(ks.reshape(bs, t_new, ATT_HEADS, ATT_HDIM))
        outs[5].append(vs.reshape(bs, t_new, ATT_HEADS, ATT_HDIM))
        outs[7].append(ss)

    y_prompt = hp.reshape(bp, seq, d)
    y_sample = hs.reshape(bs, t_new, d)
    st = [jnp.stack(o) for o in outs]
    return (y_prompt, y_sample, st[0], st[1], st[2], st[3], st[4], st[5], st[6], st[7])
```

```python
import functools
import math

import numpy as np
import jax
import jax.numpy as jnp
from jax import lax
from jax.experimental import pallas as pl
from jax.experimental.pallas import tpu as pltpu

F32 = jnp.float32
BF16 = jnp.bfloat16

EPS = 1e-6
NEG_BIG = -1e30
CONV_WIDTH = 31
CONV_HALO = 32
ATT_HEADS = 4
ATT_HDIM = 64
ATT_BLK = 128
DILATIONS = (1, 4, 16)
ATT_SPAN = 2048
ATT_UNROLL = 8
HG_HEADS = 4
LANES = 128
SUBLANES = 8
VMEM_LIMIT = 56 * 1024 * 1024


def _params(sem, vmem=VMEM_LIMIT):
    return pltpu.CompilerParams(dimension_semantics=sem, vmem_limit_bytes=vmem)


def _const_spec(shape, index):
    return pl.BlockSpec(shape, index, pipeline_mode=pl.Buffered(1))


def _rms(x, g):
    ms = jnp.mean(x * x, axis=-1, keepdims=True)
    return x * lax.rsqrt(ms + EPS) * g


def _silu(x):
    return x * jax.nn.sigmoid(x)


def _dot(a, b):
    return jnp.dot(a, b, preferred_element_type=F32)


def _dot_nt(a, b):
    return lax.dot_general(a, b, (((1,), (1,)), ((), ())), preferred_element_type=F32)


def _dot_tn(a, b):
    return lax.dot_general(a, b, (((0,), (0,)), ((), ())), preferred_element_type=F32)


def _swiglu(h, wg_ref, wu_ref, wd_ref, f_chunk):
    ffn = wg_ref.shape[1]
    acc = None
    for c0 in range(0, ffn, f_chunk):
        g = _dot(h, wg_ref[:, c0:c0 + f_chunk])
        u = _dot(h, wu_ref[:, c0:c0 + f_chunk])
        a = (_silu(g) * u).astype(BF16)
        y = _dot(a, wd_ref[c0:c0 + f_chunk, :])
        acc = y if acc is None else acc + y
    return acc


def _main_then_extra(n_main, main, extra):
    i = pl.program_id(0)
    pl.when(i < n_main)(main)
    pl.when(i == n_main)(extra)


def _ffn_kernel(xp_ref, xs_ref, ln_ref, wg_ref, wu_ref, wd_ref, op_ref, os_ref, *, f_chunk, n_main):
    def run(x_ref, o_ref):
        x = x_ref[...]
        h = _rms(x, ln_ref[...]).astype(BF16)
        o_ref[...] = x + 0.5 * _swiglu(h, wg_ref, wu_ref, wd_ref, f_chunk)

    _main_then_extra(n_main, lambda: run(xp_ref, op_ref), lambda: run(xs_ref, os_ref))


def _ffn_half_step(xp, xs, layer, ln, wg, wu, wd, *, tm, f_chunk):
    n, d = xp.shape
    ns = xs.shape[0]
    ffn = wg.shape[2]
    n_main = n // tm
    row = pl.BlockSpec((tm, d), lambda i: (jnp.minimum(i, n_main - 1), 0))
    extra = pl.BlockSpec((ns, d), lambda i: (0, 0))
    return pl.pallas_call(
        functools.partial(_ffn_kernel, f_chunk=f_chunk, n_main=n_main),
        grid=(n_main + 1,),
        in_specs=[row, extra, pl.BlockSpec((None, 1, d), lambda i: (layer, 0, 0)),
                  _const_spec((None, d, ffn), lambda i: (layer, 0, 0)),
                  _const_spec((None, d, ffn), lambda i: (layer, 0, 0)),
                  _const_spec((None, ffn, d), lambda i: (layer, 0, 0))],
        out_specs=[row, extra],
        out_shape=[jax.ShapeDtypeStruct((n, d), F32), jax.ShapeDtypeStruct((ns, d), F32)],
        compiler_params=_params(("arbitrary",)),
        name="ffn_half_step",
    )(xp, xs, ln, wg, wu, wd)


def _proj_in_kernel(x_ref, ln_ref, wi_ref, zc_ref, q_ref, k_ref, v_ref, zh_ref):
    h = _rms(x_ref[...], ln_ref[...]).astype(BF16)
    c0 = 0
    for ref in (zc_ref, q_ref, k_ref, v_ref, zh_ref):
        w = ref.shape[1]
        ref[...] = _dot(h, wi_ref[:, c0:c0 + w])
        c0 += w


def _proj_in(x, layer, ln, wi, *, tm, widths):
    n, d = x.shape
    n_in = wi.shape[2]
    row = lambda w: pl.BlockSpec((tm, w), lambda i: (i, 0))
    return pl.pallas_call(
        _proj_in_kernel,
        grid=(n // tm,),
        in_specs=[row(d), pl.BlockSpec((None, 1, d), lambda i: (layer, 0, 0)),
                  _const_spec((None, d, n_in), lambda i: (layer, 0, 0))],
        out_specs=[row(w) for w in widths],
        out_shape=[jax.ShapeDtypeStruct((n, w), F32) for w in widths],
        compiler_params=_params(("parallel",)),
        name="proj_in",
    )(x, ln, wi)


def _dense_out_kernel(xp_ref, ycp_ref, yap_ref, yhp_ref, xs_ref, ycs_ref, yas_ref, yhs_ref,
                      wo_ref, ln2_ref, wg_ref, wu_ref, wd_ref, lnf_ref, op_ref, os_ref, *, f_chunk, final, n_main):
    def run(x_ref, yc_ref, ya_ref, yh_ref, o_ref):
        y = jnp.concatenate([r[...].astype(BF16) for r in (yc_ref, ya_ref, yh_ref)], axis=-1)
        x2 = x_ref[...] + _dot(y, wo_ref[...])
        h = _rms(x2, ln2_ref[...]).astype(BF16)
        x3 = x2 + 0.5 * _swiglu(h, wg_ref, wu_ref, wd_ref, f_chunk)
        if final:
            x3 = _rms(x3, lnf_ref[...])
        o_ref[...] = x3

    _main_then_extra(n_main, lambda: run(xp_ref, ycp_ref, yap_ref, yhp_ref, op_ref),
                     lambda: run(xs_ref, ycs_ref, yas_ref, yhs_ref, os_ref))


def _dense_out(prompt, sample, layer, wo, ln2, wg, wu, wd, lnf, *, tm, f_chunk, final):
    n, d = prompt[0].shape
    ns = sample[0].shape[0]
    ffn = wg.shape[2]
    n_main = n // tm
    row = lambda a: pl.BlockSpec((tm, a.shape[1]), lambda i: (jnp.minimum(i, n_main - 1), 0))
    extra = lambda a: pl.BlockSpec((ns, a.shape[1]), lambda i: (0, 0))
    vec = pl.BlockSpec((None, 1, d), lambda i: (layer, 0, 0))
    return pl.pallas_call(
        functools.partial(_dense_out_kernel, f_chunk=f_chunk, final=final, n_main=n_main),
        grid=(n_main + 1,),
        in_specs=[row(a) for a in prompt] + [extra(a) for a in sample] + [
                  _const_spec((None, d, d), lambda i: (layer, 0, 0)),
                  vec,
                  _const_spec((None, d, ffn), lambda i: (layer, 0, 0)),
                  _const_spec((None, d, ffn), lambda i: (layer, 0, 0)),
                  _const_spec((None, ffn, d), lambda i: (layer, 0, 0)),
                  pl.BlockSpec((1, d), lambda i: (0, 0))],
        out_specs=[row(prompt[0]), extra(sample[0])],
        out_shape=[jax.ShapeDtypeStruct((n, d), F32), jax.ShapeDtypeStruct((ns, d), F32)],
        compiler_params=_params(("arbitrary",)),
        name="dense_out",
    )(*prompt, *sample, wo, ln2, wg, wu, wd, lnf)


def _conv_tile(z, ext_ref, b, dww_ref, dwb, lng, lnb):
    tt, c2 = z.shape
    c = c2 // 2
    lo = CONV_HALO - (CONV_WIDTH - 1)
    u = z[:, :c] * jax.nn.sigmoid(z[:, c:])
    ext_ref[b, CONV_HALO:CONV_HALO + tt, :] = u
    y = None
    for r in range(SUBLANES):
        z_r = None
        for o in range(r, CONV_HALO + 1, SUBLANES):
            if o < lo:
                continue
            term = ext_ref[b, o - r:o - r + tt + SUBLANES, :] * dww_ref[o - lo:o - lo + 1, :]
            z_r = term if z_r is None else z_r + term
        part = z_r[r:r + tt, :]
        y = part if y is None else y + part
    y = y + dwb
    yc = y - jnp.mean(y, axis=-1, keepdims=True)
    yn = yc * lax.rsqrt(jnp.mean(yc * yc, axis=-1, keepdims=True) + EPS)
    carry = ext_ref[b, tt:tt + CONV_HALO, :]
    ext_ref[b, 0:CONV_HALO, :] = carry
    return _silu(yn * lng + lnb), u


def _conv_kernel(zc_ref, halo_ref, dww_ref, dwb_ref, lng_ref, lnb_ref, y_ref, tail_ref, ext_ref, *, nb, tt, tail):
    c = y_ref.shape[-1]

    def one_batch(b):
        @pl.when(pl.program_id(1) == 0)
        def _():
            ext_ref[b, 0:CONV_HALO, :] = halo_ref[b]
            ext_ref[b, CONV_HALO + tt:CONV_HALO + tt + SUBLANES, :] = jnp.zeros((SUBLANES, c), F32)

        y, u = _conv_tile(zc_ref[b], ext_ref, b, dww_ref, dwb_ref[...], lng_ref[...], lnb_ref[...])
        y_ref[b] = y.astype(y_ref.dtype)
        tail_ref[b] = u[tt - tail:, :]

    if nb == 1:
        one_batch(0)
    else:
        def body(b, carry):
            one_batch(b)
            return carry
        lax.fori_loop(0, nb, body, 0)


def _conv_group(zc, halo, layer, dww, dwb, lng, lnb, *, nb, tt, out_dtype):
    b, t, c2 = zc.shape
    c = c2 // 2
    tail = min(CONV_HALO, tt)
    vec = pl.BlockSpec((None, 1, c), lambda i, j: (layer, 0, 0))
    return pl.pallas_call(
        functools.partial(_conv_kernel, nb=nb, tt=tt, tail=tail),
        grid=(b // nb, t // tt),
        in_specs=[pl.BlockSpec((nb, tt, c2), lambda i, j: (i, j, 0)),
                  pl.BlockSpec((nb, CONV_HALO, c), lambda i, j: (i, 0, 0)),
                  pl.BlockSpec((None, CONV_HALO, c), lambda i, j: (layer, 0, 0)),
                  vec, vec, vec],
        out_specs=[pl.BlockSpec((nb, tt, c), lambda i, j: (i, j, 0)),
                   pl.BlockSpec((nb, tail, c), lambda i, j: (i, 0, 0))],
        out_shape=[jax.ShapeDtypeStruct((b, t, c), out_dtype),
                   jax.ShapeDtypeStruct((b, tail, c), F32)],
        scratch_shapes=[pltpu.VMEM((nb, CONV_HALO + tt + SUBLANES, c), F32)],
        compiler_params=_params(("parallel", "arbitrary")),
        name="conv_group",
    )(zc, halo, dww, dwb, lng, lnb)


def _alibi_slopes():
    return [2.0 ** (-8.0 * (h + 1) / ATT_HEADS) for h in range(ATT_HEADS)]


def _prompt_bias_tables():
    qi = np.arange(ATT_BLK)[:, None]
    ki = np.arange(2 * ATT_BLK)[None, :]
    j = qi - ki + ATT_BLK
    tabs = []
    for dil in DILATIONS:
        for first in (False, True):
            valid = (j >= 0) & (j <= ATT_BLK) & ((not first) | (ki >= ATT_BLK))
            rows = []
            for s in _alibi_slopes():
                bias = (-np.float32(s)) * (j * dil).astype(np.float32)
                rows.append(np.where(valid, bias, np.float32(NEG_BIG)).astype(np.float32))
            tabs.append(np.concatenate(rows, axis=0))
    return np.stack(tabs, axis=0)


def _head_pair_split(x, lo_mask):
    zero = jnp.zeros_like(x)
    return jnp.concatenate([jnp.where(lo_mask, x, zero), jnp.where(lo_mask, zero, x)], axis=0)


def _attn_prompt_kernel(bias_ref, q_ref, kp_ref, kc_ref, vp_ref, vc_ref, o_ref,
                        qs_ref, ks_ref, vs_ref, os_ref, ls_ref, *, span):
    j = pl.program_id(1)
    nslab = qs_ref.shape[0]
    scale = ATT_HDIM ** -0.5
    for sl in range(nslab):
        cols = slice(sl * LANES, (sl + 1) * LANES)
        qs_ref[sl] = q_ref[:, cols] * scale
        ks_ref[sl, 0:span, :] = kp_ref[:, cols]
        ks_ref[sl, span:2 * span, :] = kc_ref[:, cols]
        vs_ref[sl, 0:span, :] = vp_ref[:, cols]
        vs_ref[sl, span:2 * span, :] = vc_ref[:, cols]

    lane = lax.broadcasted_iota(jnp.int32, (ATT_BLK, LANES), 1)
    lo_mask = lane < ATT_HDIM
    n_blocks = span // ATT_BLK

    order = sorted(range(len(DILATIONS)), key=lambda b: -DILATIONS[b])
    for pos, br in enumerate(order):
        dil = DILATIONS[br]

        def rows(start, size, dil=dil):
            if dil == 1:
                return pl.ds(pl.multiple_of(start, ATT_BLK), size)
            return pl.ds(start, size, stride=dil)

        def scores(i, dil=dil, rows=rows):
            q0 = (i // dil) * (ATT_BLK * dil) + i % dil
            k0 = span + q0 - ATT_BLK * dil
            s_parts, v_parts = [], []
            for sl in range(nslab):
                ql = qs_ref[sl, rows(q0, ATT_BLK), :]
                kl = ks_ref[sl, rows(k0, 2 * ATT_BLK), :]
                v_parts.append(vs_ref[sl, rows(k0, 2 * ATT_BLK), :].astype(BF16))
                lhs = _head_pair_split(ql, lo_mask).astype(BF16)
                s_parts.append(_dot_nt(lhs, kl.astype(BF16)))
            return jnp.concatenate(s_parts, axis=0), jnp.concatenate(v_parts, axis=1)

        def finish(i, s, v, br=br, dil=dil, pos=pos, rows=rows):
            q0 = (i // dil) * (ATT_BLK * dil) + i % dil
            first = jnp.logical_and(j == 0, i // dil == 0).astype(jnp.int32)
            bias = bias_ref[2 * br + first]
            s = jnp.where(bias > 0.5 * NEG_BIG, s + bias, NEG_BIG)
            m = jnp.max(s, axis=-1, keepdims=True)
            p = jnp.exp(s - m)
            l = jnp.sum(p, axis=-1, keepdims=True)
            pv = _dot(p.astype(BF16), v)
            lse_all = m + jnp.log(l)
            for sl in range(nslab):
                cols = slice(sl * LANES, (sl + 1) * LANES)
                ra = slice((2 * sl) * ATT_BLK, (2 * sl + 1) * ATT_BLK)
                rb = slice((2 * sl + 1) * ATT_BLK, (2 * sl + 2) * ATT_BLK)
                pair = lambda x: jnp.where(lo_mask, jnp.broadcast_to(x[ra], (ATT_BLK, LANES)),
                                           jnp.broadcast_to(x[rb], (ATT_BLK, LANES)))
                o_new = jnp.where(lo_mask, pv[ra, cols], pv[rb, cols]) / pair(l)
                lse_new = pair(lse_all)
                out_rows = rows(q0, ATT_BLK)
                if pos == 0:
                    o_tot, lse_tot = o_new, lse_new
                else:
                    o_old = os_ref[sl, out_rows, :]
                    lse_old = ls_ref[sl, out_rows, :]
                    mx = jnp.maximum(lse_old, lse_new)
                    e_old = jnp.exp(lse_old - mx)
                    e_new = jnp.exp(lse_new - mx)
                    den = e_old + e_new
                    o_tot = (e_old * o_old + e_new * o_new) / den
                    lse_tot = mx + jnp.log(den)
                os_ref[sl, out_rows, :] = o_tot
                ls_ref[sl, out_rows, :] = lse_tot

        def trip(t, carry, scores=scores, finish=finish):
            base = t * ATT_UNROLL
            ahead = scores(base)
            for u in range(ATT_UNROLL):
                s, v = ahead
                if u + 1 < ATT_UNROLL:
                    ahead = scores(base + u + 1)
                finish(base + u, s, v)
            return carry

        lax.fori_loop(0, n_blocks // ATT_UNROLL, trip, 0)

    o_ref[...] = jnp.concatenate([os_ref[sl] for sl in range(nslab)], axis=-1).astype(o_ref.dtype)


def _attn_prompt(q, k, v, bias_tabs, *, span):
    b, t, w = q.shape
    nslab = w // LANES
    cur = pl.BlockSpec((None, span, w), lambda i, j: (i, j, 0))
    prev = pl.BlockSpec((None, span, w), lambda i, j: (i, jnp.maximum(j - 1, 0), 0))
    slab = lambda rows: pltpu.VMEM((nslab, rows, LANES), F32)
    return pl.pallas_call(
        functools.partial(_attn_prompt_kernel, span=span),
        grid=(b, t // span),
        in_specs=[_const_spec(bias_tabs.shape, lambda i, j: (0, 0, 0)), cur, prev, cur, prev, cur],
        out_specs=cur,
        out_shape=jax.ShapeDtypeStruct((b, t, w), BF16),
        scratch_shapes=[slab(span), slab(2 * span), slab(2 * span), slab(span), slab(span)],
        compiler_params=_params(("parallel", "arbitrary")),
        name="attn_prompt",
    )(bias_tabs, q, k, k, v, v)


def _sample_bias_tables(n_past, t_new, t_pad):
    slopes = _alibi_slopes()
    n_keys = n_past + t_pad
    tabs = np.full((len(DILATIONS), ATT_HEADS * t_pad, n_keys), NEG_BIG, np.float32)
    for br, dil in enumerate(DILATIONS):
        for h, s in enumerate(slopes):
            for t in range(t_pad):
                tq = min(t, t_new - 1)
                for jj in range(ATT_BLK + 1):
                    idx = n_past + tq - jj * dil
                    if idx < 0:
                        continue
                    tabs[br, h * t_pad + t, idx] = -np.float32(s) * np.float32(jj * dil)
    return tabs[:, :, :n_past], tabs[:, :, n_past:]


def _attn_sample_kernel(bc_ref, bn_ref, q_ref, kn_ref, vn_ref, kc_ref, vc_ref, o_ref):
    for b in range(q_ref.shape[0]):
        _attn_sample_one(bc_ref, bn_ref, q_ref.at[b], kn_ref.at[b], vn_ref.at[b], kc_ref.at[b], vc_ref.at[b],
                         o_ref.at[b])


def _attn_sample_one(bc_ref, bn_ref, q_ref, kn_ref, vn_ref, kc_ref, vc_ref, o_ref):
    t_pad, w = q_ref.shape
    scale = ATT_HDIM ** -0.5
    q = q_ref[...] * scale
    lane = lax.broadcasted_iota(jnp.int32, (t_pad, w), 1)
    head_of_lane = lane // ATT_HDIM
    zero = jnp.zeros_like(q)
    lhs = jnp.concatenate([jnp.where(head_of_lane == h, q, zero) for h in range(ATT_HEADS)], axis=0).astype(BF16)
    s_c = _dot(lhs, kc_ref[...].astype(BF16))
    s_n = _dot_nt(lhs, kn_ref[...].astype(BF16))
    vc = vc_ref[...].astype(BF16)
    vn = vn_ref[...].astype(BF16)
    pcs, pns, ls, lses = [], [], [], []
    for br in range(len(DILATIONS)):
        bc = bc_ref[br]
        bn = bn_ref[br]
        sc = jnp.where(bc > 0.5 * NEG_BIG, s_c + bc, NEG_BIG)
        sn = jnp.where(bn > 0.5 * NEG_BIG, s_n + bn, NEG_BIG)
        m = jnp.maximum(jnp.max(sc, axis=-1, keepdims=True), jnp.max(sn, axis=-1, keepdims=True))
        pc = jnp.exp(sc - m)
        pn = jnp.exp(sn - m)
        l = jnp.sum(pc, axis=-1, keepdims=True) + jnp.sum(pn, axis=-1, keepdims=True)
        pcs.append(pc.astype(BF16))
        pns.append(pn.astype(BF16))
        ls.append(l)
        lses.append(m + jnp.log(l))
    nq = ATT_HEADS * t_pad
    pv = _dot_nt(jnp.concatenate(pcs, axis=0), vc) + _dot(jnp.concatenate(pns, axis=0), vn)
    outs = [pv[br * nq:(br + 1) * nq] / ls[br] for br in range(len(DILATIONS))]
    mx = functools.reduce(jnp.maximum, lses)
    es = [jnp.exp(x - mx) for x in lses]
    den = functools.reduce(lambda a, b: a + b, es)
    o = functools.reduce(lambda a, b: a + b, [e * x for e, x in zip(es, outs)]) / den
    res = zero
    for h in range(ATT_HEADS):
        res = jnp.where(head_of_lane == h, o[h * t_pad:(h + 1) * t_pad, :], res)
    o_ref[...] = res.astype(o_ref.dtype)


def _attn_sample(q, k, v, cache_k, cache_v, layer, bias_c, bias_n, *, nb):
    b, t_pad, w = q.shape
    n_past = cache_k.shape[3]
    new = pl.BlockSpec((nb, t_pad, w), lambda i: (i, 0, 0))
    cache = pl.BlockSpec((None, nb, w, n_past), lambda i: (layer, i, 0, 0))
    return pl.pallas_call(
        _attn_sample_kernel,
        grid=(b // nb,),
        in_specs=[_const_spec(bias_c.shape, lambda i: (0, 0, 0)),
                  _const_spec(bias_n.shape, lambda i: (0, 0, 0)),
                  new, new, new, cache, cache],
        out_specs=new,
        out_shape=jax.ShapeDtypeStruct((b, t_pad, w), F32),
        compiler_params=_params(("parallel",)),
        name="attn_sample",
    )(bias_c, bias_n, q, k, v, cache_k, cache_v)


def _hgrn_tables(chunk):
    sub = min(chunk, LANES)
    r = np.arange(sub)[:, None]
    c = np.arange(sub)[None, :]
    x = np.bitwise_xor(r, c)
    hb = np.zeros_like(x)
    for bit in range(1, 16):
        hb = np.where(x >> bit > 0, bit, hb)
    lev = np.where(c > r, -1, np.where(r // SUBLANES == c // SUBLANES, 0, hb)).astype(np.int32)
    lev = np.tile(lev, (chunk // sub, 1))
    sel = np.zeros((SUBLANES * LANES, sub), np.float32)
    for s in range(SUBLANES):
        sel[s * LANES:(s + 1) * LANES, s::SUBLANES] = 1.0
    rr = np.arange(chunk)
    tri = (rr[None, :] <= rr[:, None]).astype(np.float32)
    return lev, sel, tri


def _split3(x):
    hi = x.astype(BF16)
    r1 = x - hi.astype(F32)
    mid = r1.astype(BF16)
    lo = (r1 - mid.astype(F32)).astype(BF16)
    return hi, mid, lo


def _group_row(x, s):
    c, l = x.shape
    x3 = x.reshape(c // SUBLANES, SUBLANES, l)
    return jnp.broadcast_to(x3[:, s:s + 1, :], x3.shape).reshape(c, l)


def _block_last(x, m):
    c, l = x.shape
    x3 = x.reshape(c // m, m, l)
    return jnp.broadcast_to(x3[:, m - 1:m, :], x3.shape).reshape(c, l)


def _hgrn_lower_bound(raw, layer):
    e = jnp.exp(raw - jnp.max(raw, axis=0, keepdims=True))
    sm = e / jnp.sum(e, axis=0, keepdims=True)
    lb = jnp.zeros_like(sm[0:1])
    for i in range(1, layer + 1):
        lb = lb + sm[i:i + 1]
    return lb


def _hgrn_chunk_gates(zq, zf, zi, lb, tri, live):
    dk = zq.shape[1]
    q = _silu(zq)
    f = lb + (1.0 - lb) * jax.nn.sigmoid(zf)
    g = jnp.log2(f)
    k = 1.0 - f
    if live is not None:
        g = jnp.where(live, g, 0.0)
        k = jnp.where(live, k, 0.0)
    b3 = _dot(tri, jnp.concatenate(_split3(g), axis=1))
    b = b3[:, :dk] + b3[:, dk:2 * dk] + b3[:, 2 * dk:]
    return q, k, b, zi.astype(BF16)


def _hgrn_chunk_local(q, k, b, vb, lev, sel):
    chunk, dk = q.shape
    sub = lev.shape[1]

    vals = []
    for s in range(SUBLANES):
        decay = jnp.exp2(jnp.minimum(b - _group_row(b, s), 0.0))
        vals.append((q * decay * _group_row(k, s)).astype(BF16))
    a = jnp.where(lev == 0, _dot(jnp.concatenate(vals, axis=1), sel), 0.0)
    subs = [slice(r0, r0 + sub) for r0 in range(0, chunk, sub)]
    for bit in range(3, int(math.log2(sub))):
        m = 1 << bit
        b_end = _block_last(b, m)
        b_prev = jnp.concatenate([jnp.zeros((m, dk), F32), b_end[:chunk - m]], axis=0)
        qd = (q * jnp.exp2(b - b_prev)).astype(BF16)
        kd = (k * jnp.exp2(b_end - b)).astype(BF16)
        a_l = jnp.concatenate([_dot_nt(qd[rs], kd[rs]) for rs in subs], axis=0)
        a = jnp.where(lev == bit, a_l, a)
    ab = a.astype(BF16)
    o_parts = []
    for i, rs in enumerate(subs):
        if i == 0:
            o_parts.append(_dot(ab[rs], vb[rs]))
            continue
        b_piv = b[rs.start - 1:rs.start, :]
        qd = (q[rs] * jnp.exp2(b[rs] - b_piv)).astype(BF16)
        kd = (k[:rs.start] * jnp.exp2(b_piv - b[:rs.start])).astype(BF16)
        a_row = jnp.concatenate([_dot_nt(qd, kd).astype(BF16), ab[rs]], axis=1)
        o_parts.append(_dot(a_row, vb[:rs.stop]))
    o = jnp.concatenate(o_parts, axis=0) if len(o_parts) > 1 else o_parts[0]
    q_in = (q * jnp.exp2(b)).astype(BF16)
    b_last = b[chunk - 1:chunk, :]
    kd = (k * jnp.exp2(b_last - b)).astype(BF16)
    return o, q_in, _dot_tn(vb, kd), jnp.exp2(b_last)


def _hgrn_chunk_finish(o_local, q_in, st, ng, zg):
    o = o_local + _dot_nt(q_in, st.astype(BF16))
    on = o * lax.rsqrt(jnp.mean(o * o, axis=-1, keepdims=True) + EPS) * ng
    return on * _silu(zg)


def _hgrn_sample_kernel(zh_ref, lb_ref, ng_ref, lev_ref, sel_ref, tri_ref, s0_ref, o_ref, sf_ref,
                        *, nb, layer, t_valid):
    t_pad = zh_ref.shape[1]
    hd = o_ref.shape[2]
    dk = hd // HG_HEADS
    lev = lev_ref[...]
    sel = sel_ref[...]
    tri = tri_ref[...]
    ng = ng_ref[...]
    live = lax.broadcasted_iota(jnp.int32, (t_pad, dk), 0) < t_valid
    lbs = [_hgrn_lower_bound(lb_ref[:, h * dk:(h + 1) * dk], layer) for h in range(HG_HEADS)]

    def one_batch(bi, carry):
        for h in range(HG_HEADS):
            part = lambda p: zh_ref[bi, :, (p * HG_HEADS + h) * dk:(p * HG_HEADS + h + 1) * dk]
            gates = _hgrn_chunk_gates(part(0), part(1), part(2), lbs[h], tri, live)
            o_local, q_in, s_own, d_all = _hgrn_chunk_local(*gates, lev, sel)
            st = s0_ref[bi, h].T
            o_ref[bi, :, h * dk:(h + 1) * dk] = _hgrn_chunk_finish(o_local, q_in, st, ng, part(3))
            sf_ref[bi, h] = (st * d_all + s_own).T
        return carry

    lax.fori_loop(0, nb, one_batch, 0, unroll=2)


def _hgrn_sample(zh, lb_raw, norm_g, s0, layer, tables, *, nb, t_valid):
    b, t_pad, w4 = zh.shape
    hd = w4 // 4
    dk = hd // HG_HEADS
    depth = lb_raw.shape[0]
    lev, sel, tri = tables
    return pl.pallas_call(
        functools.partial(_hgrn_sample_kernel, nb=nb, layer=layer, t_valid=t_valid),
        grid=(b // nb,),
        in_specs=[pl.BlockSpec((nb, t_pad, w4), lambda i: (i, 0, 0)),
                  pl.BlockSpec((depth, hd), lambda i: (0, 0)),
                  pl.BlockSpec((None, 1, dk), lambda i: (layer, 0, 0)),
                  _const_spec(lev.shape, lambda i: (0, 0)),
                  _const_spec(sel.shape, lambda i: (0, 0)),
                  _const_spec(tri.shape, lambda i: (0, 0)),
                  pl.BlockSpec((None, nb, HG_HEADS, dk, dk), lambda i: (layer, i, 0, 0, 0))],
        out_specs=[pl.BlockSpec((nb, t_pad, hd), lambda i: (i, 0, 0)),
                   pl.BlockSpec((nb, HG_HEADS, dk, dk), lambda i: (i, 0, 0, 0))],
        out_shape=[jax.ShapeDtypeStruct((b, t_pad, hd), F32),
                   jax.ShapeDtypeStruct((b, HG_HEADS, dk, dk), F32)],
        compiler_params=_params(("parallel",)),
        name="hgrn_sample",
    )(zh, lb_raw, norm_g, lev, sel, tri, s0)


def _proj_mix_kernel(x_ref, ln_ref, wi_ref, dww_ref, dwb_ref, lng_ref, lnb_ref, lb_ref, ng_ref,
                     lev_ref, sel_ref, tri_ref,
                     yc_ref, q_ref, k_ref, v_ref, yh_ref, tail_ref, sf_ref, kt_ref, vt_ref,
                     ext_ref, st_ref, *, layer, chunk):
    tstep = pl.program_id(1)
    tt = x_ref.shape[0]
    conv_c = yc_ref.shape[1]
    att = q_ref.shape[1]
    hd = yh_ref.shape[1]
    dk = hd // HG_HEADS
    n_chunks = tt // chunk

    @pl.when(tstep == 0)
    def _():
        ext_ref[0, 0:CONV_HALO, :] = jnp.zeros((CONV_HALO, conv_c), F32)
        ext_ref[0, CONV_HALO + tt:CONV_HALO + tt + SUBLANES, :] = jnp.zeros((SUBLANES, conv_c), F32)
        st_ref[...] = jnp.zeros_like(st_ref)

    h = _rms(x_ref[...], ln_ref[...]).astype(BF16)
    c0 = 2 * conv_c
    zc = _dot(h, wi_ref[:, 0:c0])
    zh = _dot(h, wi_ref[:, c0 + 3 * att:c0 + 3 * att + 4 * hd])
    yc, u = _conv_tile(zc, ext_ref, 0, dww_ref, dwb_ref[...], lng_ref[...], lnb_ref[...])
    yc_ref[...] = yc.astype(yc_ref.dtype)
    tail_ref[...] = u[tt - CONV_HALO:, :]
    for i, (ref, t_ref) in enumerate(((q_ref, None), (k_ref, kt_ref), (v_ref, vt_ref))):
        z = _dot(h, wi_ref[:, c0 + i * att:c0 + (i + 1) * att])
        ref[...] = z
        if t_ref is not None:
            t_ref[...] = z.T

    lev = lev_ref[...]
    sel = sel_ref[...]
    tri = tri_ref[...]
    ng = ng_ref[...]
    lbs = [_hgrn_lower_bound(lb_ref[:, hh * dk:(hh + 1) * dk], layer) for hh in range(HG_HEADS)]

    def part(u_idx, p):
        hh, c = divmod(u_idx, n_chunks)
        return zh[c * chunk:(c + 1) * chunk, (p * HG_HEADS + hh) * dk:(p * HG_HEADS + hh + 1) * dk]

    def gates(u_idx):
        return _hgrn_chunk_gates(part(u_idx, 0), part(u_idx, 1), part(u_idx, 2), lbs[u_idx // n_chunks], tri, None)

    ahead = gates(0)
    st = None
    for u_idx in range(HG_HEADS * n_chunks):
        hh, c = divmod(u_idx, n_chunks)
        cur = ahead
        if u_idx + 1 < HG_HEADS * n_chunks:
            ahead = gates(u_idx + 1)
        if c == 0:
            st = st_ref[hh]
        o_local, q_in, s_own, d_all = _hgrn_chunk_local(*cur, lev, sel)
        o = _hgrn_chunk_finish(o_local, q_in, st, ng, part(u_idx, 3))
        yh_ref[c * chunk:(c + 1) * chunk, hh * dk:(hh + 1) * dk] = o.astype(yh_ref.dtype)
        st = st * d_all + s_own
        if c == n_chunks - 1:
            st_ref[hh] = st

    @pl.when(tstep == pl.num_programs(1) - 1)
    def _():
        for hh in range(HG_HEADS):
            sf_ref[hh] = st_ref[hh].T


def _proj_mix(x, layer, ln, wi, dww, dwb, lng, lnb, lb_raw, norm_g, tables, *, batch, tm, chunk, widths, keep):
    n, d = x.shape
    n_in = wi.shape[2]
    conv_c = widths[0] // 2
    att = widths[1]
    hd = widths[4] // 4
    dk = hd // HG_HEADS
    depth = lb_raw.shape[0]
    nt = n // batch // tm
    lev, sel, tri = tables
    row = lambda w: pl.BlockSpec((tm, w), lambda b, j: (b * nt + j, 0))
    cvec = pl.BlockSpec((None, 1, conv_c), lambda b, j: (layer, 0, 0))
    const2 = lambda a: _const_spec(a.shape, lambda b, j: (0, 0))
    first_kept = nt - keep // tm
    window = pl.BlockSpec((None, att, tm), lambda b, j: (b, 0, jnp.maximum(j - first_kept, 0)))
    return pl.pallas_call(
        functools.partial(_proj_mix_kernel, layer=layer, chunk=chunk),
        grid=(batch, nt),
        in_specs=[row(d), pl.BlockSpec((None, 1, d), lambda b, j: (layer, 0, 0)),
                  _const_spec((None, d, n_in), lambda b, j: (layer, 0, 0)),
                  pl.BlockSpec((None, CONV_HALO, conv_c), lambda b, j: (layer, 0, 0)),
                  cvec, cvec, cvec,
                  pl.BlockSpec((depth, hd), lambda b, j: (0, 0)),
                  pl.BlockSpec((None, 1, dk), lambda b, j: (layer, 0, 0)),
                  const2(lev), const2(sel), const2(tri)],
        out_specs=[row(conv_c), row(att), row(att), row(att), row(hd),
                   pl.BlockSpec((None, CONV_HALO, conv_c), lambda b, j: (b, 0, 0)),
                   pl.BlockSpec((None, HG_HEADS, dk, dk), lambda b, j: (b, 0, 0, 0)),
                   window, window],
        out_shape=[jax.ShapeDtypeStruct((n, conv_c), BF16),
                   jax.ShapeDtypeStruct((n, att), F32),
                   jax.ShapeDtypeStruct((n, att), F32),
                   jax.ShapeDtypeStruct((n, att), F32),
                   jax.ShapeDtypeStruct((n, hd), BF16),
                   jax.ShapeDtypeStruct((batch, CONV_HALO, conv_c), F32),
                   jax.ShapeDtypeStruct((batch, HG_HEADS, dk, dk), F32),
                   jax.ShapeDtypeStruct((batch, att, keep), F32),
                   jax.ShapeDtypeStruct((batch, att, keep), F32)],
        scratch_shapes=[pltpu.VMEM((1, CONV_HALO + tm + SUBLANES, conv_c), F32),
                        pltpu.VMEM((HG_HEADS, dk, dk), F32)],
        compiler_params=_params(("parallel", "arbitrary")),
        name="proj_mix",
    )(x, ln, wi, dww, dwb, lng, lnb, lb_raw, norm_g, lev, sel, tri)


def _pad_rows(x, rows):
    return jnp.pad(x, ((0, 0), (0, rows - x.shape[1]), (0, 0)))


def _key_minor(cache):
    depth, b, n_past, h, hd = cache.shape
    return jnp.transpose(cache, (0, 1, 3, 4, 2)).reshape(depth, b, h * hd, n_past)


def kernel(x_prompt, x_sample, state_conv, cache_k_win, cache_v_win, state_hgrn, ln_ffn1, w_ffn1_gate, w_ffn1_up, w_ffn1_down, ln_mix, w_in, conv_dw_w, conv_dw_b, conv_ln_g, conv_ln_b, hg_lower_bounds, hg_norm_g, w_out, ln_ffn2, w_ffn2_gate, w_ffn2_up, w_ffn2_down, ln_final):
    bp, seq, d = x_prompt.shape
    bs, t_new, _ = x_sample.shape
    depth = w_in.shape[0]
    conv_dim = conv_dw_w.shape[2]
    att_dim = ATT_HEADS * ATT_HDIM
    hg_dim = hg_lower_bounds.shape[1]
    widths = (2 * conv_dim, att_dim, att_dim, att_dim, 4 * hg_dim)
    n_past = cache_k_win.shape[2]
    t_pad = SUBLANES
    keep = min(ATT_SPAN, seq)

    wg1, wu1, wd1 = (w.astype(BF16) for w in (w_ffn1_gate, w_ffn1_up, w_ffn1_down))
    wg2, wu2, wd2 = (w.astype(BF16) for w in (w_ffn2_gate, w_ffn2_up, w_ffn2_down))
    wi, wo = w_in.astype(BF16), w_out.astype(BF16)
    vec3 = lambda a: a.reshape(depth, 1, a.shape[-1])
    ln1, lnm, ln2 = vec3(ln_ffn1), vec3(ln_mix), vec3(ln_ffn2)
    dwb, clg, clb, hgn = vec3(conv_dw_b), vec3(conv_ln_g), vec3(conv_ln_b), vec3(hg_norm_g)
    lnf = ln_final.reshape(1, d)
    dww = jnp.pad(conv_dw_w, ((0, 0), (0, CONV_HALO - CONV_WIDTH), (0, 0)))
    cache_k, cache_v = _key_minor(cache_k_win), _key_minor(cache_v_win)
    halo_s = jnp.pad(state_conv, ((0, 0), (0, 0), (CONV_HALO - (CONV_WIDTH - 1), 0), (0, 0)))

    bias_p = jnp.asarray(_prompt_bias_tables())
    bias_c, bias_n = (jnp.asarray(a) for a in _sample_bias_tables(n_past, t_new, t_pad))
    chunk_p = 128
    tabs_p = tuple(jnp.asarray(a, dt) for a, dt in zip(_hgrn_tables(chunk_p), (jnp.int32, BF16, BF16)))
    tabs_s = tuple(jnp.asarray(a, dt) for a, dt in zip(_hgrn_tables(t_pad), (jnp.int32, BF16, BF16)))

    hp = x_prompt.reshape(bp * seq, d)
    hs = x_sample.reshape(bs * t_new, d)
    outs = [[] for _ in range(8)]
    for l in range(depth):
        final = l == depth - 1
        x1, x1s = _ffn_half_step(hp, hs, l, ln1, wg1, wu1, wd1, tm=512, f_chunk=1408)
        yc, q, k, v, yh, tail, sp, kt, vt = _proj_mix(x1, l, lnm, wi, dww, dwb, clg, clb, hg_lower_bounds, hgn,
                                                      tabs_p, batch=bp, tm=512, chunk=chunk_p, widths=widths,
                                                      keep=keep)
        seq3 = lambda a: a.reshape(bp, seq, a.shape[-1])
        ya = _attn_prompt(seq3(q), seq3(k), seq3(v), bias_p, span=ATT_SPAN)
        window = lambda a: a.reshape(bp, ATT_HEADS, ATT_HDIM, keep).transpose(0, 3, 1, 2)
        outs[0].append(tail[:, -(CONV_WIDTH - 1):])
        outs[2].append(window(kt))
        outs[3].append(window(vt))
        outs[6].append(sp)

        zcs, qs, ks, vs, zhs = _proj_in(x1s, l, lnm, wi, tm=bs * t_new, widths=widths)
        new3 = lambda a: _pad_rows(a.reshape(bs, t_new, a.shape[-1]), t_pad)
        ycs, us = _conv_group(new3(zcs), halo_s[l], l, dww, dwb, clg, clb, nb=bs, tt=t_pad, out_dtype=F32)
        yas = _attn_sample(new3(qs), new3(ks), new3(vs), cache_k, cache_v, l, bias_c, bias_n, nb=2)
        yhs, ss = _hgrn_sample(new3(zhs), hg_lower_bounds, hgn, state_hgrn, l, tabs_s, nb=8, t_valid=t_new)
        unpad = lambda a: a[:, :t_new].reshape(bs * t_new, a.shape[-1])
        hp, hs = _dense_out((x1, yc, ya.reshape(bp * seq, att_dim), yh),
                            (x1s, unpad(ycs), unpad(yas), unpad(yhs)),
                            l, wo, ln2, wg2, wu2, wd2, lnf, tm=512, f_chunk=1408, final=final)
        outs[1].append(jnp.concatenate([state_conv[l][:, t_new:], us[:, :t_new]], axis=1))
        outs[4].append(ks.reshape(bs, t_new, ATT_HEADS, ATT_HDIM))
        outs[5].append(vs.reshape(bs, t_new, ATT_HEADS, ATT_HDIM))
        outs[7].append(ss)

    y_prompt = hp.reshape(bp, seq, d)
    y_sample = hs.reshape(bs, t_new, d)
    st = [jnp.stack(o) for o in outs]
    return (y_prompt, y_sample, st[0], st[1], st[2], st[3], st[4], st[5], st[6], st[7])
```

```python
import functools
import math

import numpy as np
import jax
import jax.numpy as jnp
from jax import lax
from jax.experimental import pallas as pl
from jax.experimental.pallas import tpu as pltpu

F32 = jnp.float32
BF16 = jnp.bfloat16

EPS = 1e-6
NEG_BIG = -1e30
CONV_WIDTH = 31
CONV_HALO = 32
ATT_HEADS = 4
ATT_HDIM = 64
ATT_BLK = 128
DILATIONS = (1, 4, 16)
ATT_SPAN = 2048
ATT_UNROLL = 8
HG_HEADS = 4
LANES = 128
SUBLANES = 8
BF16_ROWS = 16
VMEM_LIMIT = 56 * 1024 * 1024


def _params(sem, vmem=VMEM_LIMIT):
    return pltpu.CompilerParams(dimension_semantics=sem, vmem_limit_bytes=vmem)


def _const_spec(shape, index):
    return pl.BlockSpec(shape, index, pipeline_mode=pl.Buffered(1))


def _rms(x, g):
    ms = jnp.mean(x * x, axis=-1, keepdims=True)
    return x * lax.rsqrt(ms + EPS) * g


def _silu(x):
    return x * jax.nn.sigmoid(x)


def _dot(a, b):
    return jnp.dot(a, b, preferred_element_type=F32)


def _dot_nt(a, b):
    return lax.dot_general(a, b, (((1,), (1,)), ((), ())), preferred_element_type=F32)


def _dot_tn(a, b):
    return lax.dot_general(a, b, (((0,), (0,)), ((), ())), preferred_element_type=F32)


def _swiglu(h, wg_ref, wu_ref, wd_ref, f_chunk):
    ffn = wg_ref.shape[1]
    acc = None
    for c0 in range(0, ffn, f_chunk):
        g = _dot(h, wg_ref[:, c0:c0 + f_chunk])
        u = _dot(h, wu_ref[:, c0:c0 + f_chunk])
        a = (_silu(g) * u).astype(BF16)
        y = _dot(a, wd_ref[c0:c0 + f_chunk, :])
        acc = y if acc is None else acc + y
    return acc


def _main_then_extra(n_main, main, extra):
    i = pl.program_id(0)
    pl.when(i < n_main)(main)
    pl.when(i == n_main)(extra)


def _cast_specs(stacked, layer, n_steps):
    in_specs, out_specs, shapes = [], [], []
    for w in stacked:
        _, r, c = w.shape
        rows = -(-r // n_steps)
        rows = -(-rows // BF16_ROWS) * BF16_ROWS
        while r % rows:
            rows += BF16_ROWS
        last = r // rows - 1
        in_specs.append(pl.BlockSpec((None, rows, c), lambda i, last=last: (layer, jnp.minimum(i, last), 0)))
        out_specs.append(pl.BlockSpec((rows, c), lambda i, last=last: (jnp.minimum(i, last), 0)))
        shapes.append(jax.ShapeDtypeStruct((r, c), BF16))
    return in_specs, out_specs, shapes


def _cast_slabs(src_refs, dst_refs):
    for src, dst in zip(src_refs, dst_refs):
        dst[...] = src[...].astype(dst.dtype)


def _ffn_kernel(*refs, f_chunk, n_main, n_cast):
    xp_ref, xs_ref, ln_ref, wg_ref, wu_ref, wd_ref = refs[:6]
    cast_src = refs[6:6 + n_cast]
    op_ref, os_ref = refs[6 + n_cast:8 + n_cast]
    cast_dst = refs[8 + n_cast:]

    def run(x_ref, o_ref):
        x = x_ref[...]
        h = _rms(x, ln_ref[...]).astype(BF16)
        o_ref[...] = x + 0.5 * _swiglu(h, wg_ref, wu_ref, wd_ref, f_chunk)

    _main_then_extra(n_main, lambda: run(xp_ref, op_ref), lambda: run(xs_ref, os_ref))
    _cast_slabs(cast_src, cast_dst)


def _ffn_half_step(xp, xs, layer, ln, wg, wu, wd, cast, cast_layer, *, tm, f_chunk):
    n, d = xp.shape
    ns = xs.shape[0]
    ffn = wg.shape[1]
    n_main = n // tm
    row = pl.BlockSpec((tm, d), lambda i: (jnp.minimum(i, n_main - 1), 0))
    extra = pl.BlockSpec((ns, d), lambda i: (0, 0))
    c_in, c_out, c_shapes = _cast_specs(cast, cast_layer, n_main + 1)
    res = pl.pallas_call(
        functools.partial(_ffn_kernel, f_chunk=f_chunk, n_main=n_main, n_cast=len(cast)),
        grid=(n_main + 1,),
        in_specs=[row, extra, pl.BlockSpec((None, 1, d), lambda i: (layer, 0, 0)),
                  _const_spec((d, ffn), lambda i: (0, 0)),
                  _const_spec((d, ffn), lambda i: (0, 0)),
                  _const_spec((ffn, d), lambda i: (0, 0))] + c_in,
        out_specs=[row, extra] + c_out,
        out_shape=[jax.ShapeDtypeStruct((n, d), F32), jax.ShapeDtypeStruct((ns, d), F32)] + c_shapes,
        compiler_params=_params(("arbitrary",)),
        name="ffn_half_step",
    )(xp, xs, ln, wg, wu, wd, *cast)
    return res[0], res[1], tuple(res[2:])


def _proj_in_kernel(x_ref, ln_ref, wi_ref, zc_ref, q_ref, k_ref, v_ref, zh_ref):
    h = _rms(x_ref[...], ln_ref[...]).astype(BF16)
    c0 = 0
    for ref in (zc_ref, q_ref, k_ref, v_ref, zh_ref):
        w = ref.shape[1]
        ref[...] = _dot(h, wi_ref[:, c0:c0 + w])
        c0 += w


def _proj_in(x, layer, ln, wi, *, tm, widths):
    n, d = x.shape
    n_in = wi.shape[2]
    row = lambda w: pl.BlockSpec((tm, w), lambda i: (i, 0))
    return pl.pallas_call(
        _proj_in_kernel,
        grid=(n // tm,),
        in_specs=[row(d), pl.BlockSpec((None, 1, d), lambda i: (layer, 0, 0)),
                  _const_spec((None, d, n_in), lambda i: (layer, 0, 0))],
        out_specs=[row(w) for w in widths],
        out_shape=[jax.ShapeDtypeStruct((n, w), F32) for w in widths],
        compiler_params=_params(("parallel",)),
        name="proj_in",
    )(x, ln, wi)


def _dense_out_kernel(*refs, f_chunk, final, n_main, n_cast):
    (xp_ref, ycp_ref, yap_ref, yhp_ref, xs_ref, ycs_ref, yas_ref, yhs_ref,
     wo_ref, ln2_ref, wg_ref, wu_ref, wd_ref, lnf_ref) = refs[:14]
    cast_src = refs[14:14 + n_cast]
    op_ref, os_ref = refs[14 + n_cast:16 + n_cast]
    cast_dst = refs[16 + n_cast:]

    def run(x_ref, yc_ref, ya_ref, yh_ref, o_ref):
        y = jnp.concatenate([r[...].astype(BF16) for r in (yc_ref, ya_ref, yh_ref)], axis=-1)
        x2 = x_ref[...] + _dot(y, wo_ref[...])
        h = _rms(x2, ln2_ref[...]).astype(BF16)
        x3 = x2 + 0.5 * _swiglu(h, wg_ref, wu_ref, wd_ref, f_chunk)
        if final:
            x3 = _rms(x3, lnf_ref[...])
        o_ref[...] = x3

    _main_then_extra(n_main, lambda: run(xp_ref, ycp_ref, yap_ref, yhp_ref, op_ref),
                     lambda: run(xs_ref, ycs_ref, yas_ref, yhs_ref, os_ref))
    _cast_slabs(cast_src, cast_dst)


def _dense_out(prompt, sample, layer, wo, ln2, wg, wu, wd, lnf, cast, cast_layer, *, tm, f_chunk, final):
    n, d = prompt[0].shape
    ns = sample[0].shape[0]
    ffn = wg.shape[1]
    n_main = n // tm
    row = lambda a: pl.BlockSpec((tm, a.shape[1]), lambda i: (jnp.minimum(i, n_main - 1), 0))
    extra = lambda a: pl.BlockSpec((ns, a.shape[1]), lambda i: (0, 0))
    vec = pl.BlockSpec((None, 1, d), lambda i: (layer, 0, 0))
    c_in, c_out, c_shapes = _cast_specs(cast, cast_layer, n_main + 1)
    res = pl.pallas_call(
        functools.partial(_dense_out_kernel, f_chunk=f_chunk, final=final, n_main=n_main, n_cast=len(cast)),
        grid=(n_main + 1,),
        in_specs=[row(a) for a in prompt] + [extra(a) for a in sample] + [
                  _const_spec((None, d, d), lambda i: (layer, 0, 0)),
                  vec,
                  _const_spec((d, ffn), lambda i: (0, 0)),
                  _const_spec((d, ffn), lambda i: (0, 0)),
                  _const_spec((ffn, d), lambda i: (0, 0)),
                  pl.BlockSpec((1, d), lambda i: (0, 0))] + c_in,
        out_specs=[row(prompt[0]), extra(sample[0])] + c_out,
        out_shape=[jax.ShapeDtypeStruct((n, d), F32), jax.ShapeDtypeStruct((ns, d), F32)] + c_shapes,
        compiler_params=_params(("arbitrary",)),
        name="dense_out",
    )(*prompt, *sample, wo, ln2, wg, wu, wd, lnf, *cast)
    return res[0], res[1], tuple(res[2:])


def _conv_tile(z, ext_ref, b, dww_ref, dwb, lng, lnb):
    tt, c2 = z.shape
    c = c2 // 2
    lo = CONV_HALO - (CONV_WIDTH - 1)
    u = z[:, :c] * jax.nn.sigmoid(z[:, c:])
    ext_ref[b, CONV_HALO:CONV_HALO + tt, :] = u
    y = None
    for r in range(SUBLANES):
        z_r = None
        for o in range(r, CONV_HALO + 1, SUBLANES):
            if o < lo:
                continue
            term = ext_ref[b, o - r:o - r + tt + SUBLANES, :] * dww_ref[o - lo:o - lo + 1, :]
            z_r = term if z_r is None else z_r + term
        part = z_r[r:r + tt, :]
        y = part if y is None else y + part
    y = y + dwb
    yc = y - jnp.mean(y, axis=-1, keepdims=True)
    yn = yc * lax.rsqrt(jnp.mean(yc * yc, axis=-1, keepdims=True) + EPS)
    carry = ext_ref[b, tt:tt + CONV_HALO, :]
    ext_ref[b, 0:CONV_HALO, :] = carry
    return _silu(yn * lng + lnb), u


def _conv_kernel(zc_ref, halo_ref, dww_ref, dwb_ref, lng_ref, lnb_ref, y_ref, tail_ref, ext_ref, *, nb, tt, tail):
    c = y_ref.shape[-1]

    def one_batch(b):
        @pl.when(pl.program_id(1) == 0)
        def _():
            ext_ref[b, 0:CONV_HALO, :] = halo_ref[b]
            ext_ref[b, CONV_HALO + tt:CONV_HALO + tt + SUBLANES, :] = jnp.zeros((SUBLANES, c), F32)

        y, u = _conv_tile(zc_ref[b], ext_ref, b, dww_ref, dwb_ref[...], lng_ref[...], lnb_ref[...])
        y_ref[b] = y.astype(y_ref.dtype)
        tail_ref[b] = u[tt - tail:, :]

    if nb == 1:
        one_batch(0)
    else:
        def body(b, carry):
            one_batch(b)
            return carry
        lax.fori_loop(0, nb, body, 0)


def _conv_group(zc, halo, layer, dww, dwb, lng, lnb, *, nb, tt, out_dtype):
    b, t, c2 = zc.shape
    c = c2 // 2
    tail = min(CONV_HALO, tt)
    vec = pl.BlockSpec((None, 1, c), lambda i, j: (layer, 0, 0))
    return pl.pallas_call(
        functools.partial(_conv_kernel, nb=nb, tt=tt, tail=tail),
        grid=(b // nb, t // tt),
        in_specs=[pl.BlockSpec((nb, tt, c2), lambda i, j: (i, j, 0)),
                  pl.BlockSpec((nb, CONV_HALO, c), lambda i, j: (i, 0, 0)),
                  pl.BlockSpec((None, CONV_HALO, c), lambda i, j: (layer, 0, 0)),
                  vec, vec, vec],
        out_specs=[pl.BlockSpec((nb, tt, c), lambda i, j: (i, j, 0)),
                   pl.BlockSpec((nb, tail, c), lambda i, j: (i, 0, 0))],
        out_shape=[jax.ShapeDtypeStruct((b, t, c), out_dtype),
                   jax.ShapeDtypeStruct((b, tail, c), F32)],
        scratch_shapes=[pltpu.VMEM((nb, CONV_HALO + tt + SUBLANES, c), F32)],
        compiler_params=_params(("parallel", "arbitrary")),
        name="conv_group",
    )(zc, halo, dww, dwb, lng, lnb)


def _alibi_slopes():
    return [2.0 ** (-8.0 * (h + 1) / ATT_HEADS) for h in range(ATT_HEADS)]


def _prompt_bias_tables():
    qi = np.arange(ATT_BLK)[:, None]
    ki = np.arange(2 * ATT_BLK)[None, :]
    j = qi - ki + ATT_BLK
    tabs = []
    for dil in DILATIONS:
        for first in (False, True):
            valid = (j >= 0) & (j <= ATT_BLK) & ((not first) | (ki >= ATT_BLK))
            rows = []
            for s in _alibi_slopes():
                bias = (-np.float32(s)) * (j * dil).astype(np.float32)
                rows.append(np.where(valid, bias, np.float32(NEG_BIG)).astype(np.float32))
            tabs.append(np.concatenate(rows, axis=0))
    return np.stack(tabs, axis=0)


def _head_pair_split(x, lo_mask):
    zero = jnp.zeros_like(x)
    return jnp.concatenate([jnp.where(lo_mask, x, zero), jnp.where(lo_mask, zero, x)], axis=0)


def _attn_prompt_kernel(bias_ref, q_ref, kp_ref, kc_ref, vp_ref, vc_ref, o_ref,
                        qs_ref, ks_ref, vs_ref, os_ref, ls_ref, *, span):
    j = pl.program_id(1)
    nslab = qs_ref.shape[0]
    scale = ATT_HDIM ** -0.5
    for sl in range(nslab):
        cols = slice(sl * LANES, (sl + 1) * LANES)
        qs_ref[sl] = q_ref[:, cols] * scale
        ks_ref[sl, 0:span, :] = kp_ref[:, cols]
        ks_ref[sl, span:2 * span, :] = kc_ref[:, cols]
        vs_ref[sl, 0:span, :] = vp_ref[:, cols]
        vs_ref[sl, span:2 * span, :] = vc_ref[:, cols]

    lane = lax.broadcasted_iota(jnp.int32, (ATT_BLK, LANES), 1)
    lo_mask = lane < ATT_HDIM
    n_blocks = span // ATT_BLK

    order = sorted(range(len(DILATIONS)), key=lambda b: -DILATIONS[b])
    for pos, br in enumerate(order):
        dil = DILATIONS[br]

        def rows(start, size, dil=dil):
            if dil == 1:
                return pl.ds(pl.multiple_of(start, ATT_BLK), size)
            return pl.ds(start, size, stride=dil)

        def scores(i, dil=dil, rows=rows):
            q0 = (i // dil) * (ATT_BLK * dil) + i % dil
            k0 = span + q0 - ATT_BLK * dil
            s_parts, v_parts = [], []
            for sl in range(nslab):
                ql = qs_ref[sl, rows(q0, ATT_BLK), :]
                kl = ks_ref[sl, rows(k0, 2 * ATT_BLK), :]
                v_parts.append(vs_ref[sl, rows(k0, 2 * ATT_BLK), :].astype(BF16))
                lhs = _head_pair_split(ql, lo_mask).astype(BF16)
                s_parts.append(_dot_nt(lhs, kl.astype(BF16)))
            return jnp.concatenate(s_parts, axis=0), jnp.concatenate(v_parts, axis=1)

        def finish(i, s, v, br=br, dil=dil, pos=pos, rows=rows):
            q0 = (i // dil) * (ATT_BLK * dil) + i % dil
            first = jnp.logical_and(j == 0, i // dil == 0).astype(jnp.int32)
            bias = bias_ref[2 * br + first]
            s = jnp.where(bias > 0.5 * NEG_BIG, s + bias, NEG_BIG)
            m = jnp.max(s, axis=-1, keepdims=True)
            p = jnp.exp(s - m)
            l = jnp.sum(p, axis=-1, keepdims=True)
            pv = _dot(p.astype(BF16), v)
            lse_all = m + jnp.log(l)
            for sl in range(nslab):
                cols = slice(sl * LANES, (sl + 1) * LANES)
                ra = slice((2 * sl) * ATT_BLK, (2 * sl + 1) * ATT_BLK)
                rb = slice((2 * sl + 1) * ATT_BLK, (2 * sl + 2) * ATT_BLK)
                pair = lambda x: jnp.where(lo_mask, jnp.broadcast_to(x[ra], (ATT_BLK, LANES)),
                                           jnp.broadcast_to(x[rb], (ATT_BLK, LANES)))
                o_new = jnp.where(lo_mask, pv[ra, cols], pv[rb, cols]) / pair(l)
                lse_new = pair(lse_all)
                out_rows = rows(q0, ATT_BLK)
                if pos == 0:
                    o_tot, lse_tot = o_new, lse_new
                else:
                    o_old = os_ref[sl, out_rows, :]
                    lse_old = ls_ref[sl, out_rows, :]
                    mx = jnp.maximum(lse_old, lse_new)
                    e_old = jnp.exp(lse_old - mx)
                    e_new = jnp.exp(lse_new - mx)
                    den = e_old + e_new
                    o_tot = (e_old * o_old + e_new * o_new) / den
                    lse_tot = mx + jnp.log(den)
                os_ref[sl, out_rows, :] = o_tot
                ls_ref[sl, out_rows, :] = lse_tot

        def trip(t, carry, scores=scores, finish=finish):
            base = t * ATT_UNROLL
            ahead = scores(base)
            for u in range(ATT_UNROLL):
                s, v = ahead
                if u + 1 < ATT_UNROLL:
                    ahead = scores(base + u + 1)
                finish(base + u, s, v)
            return carry

        lax.fori_loop(0, n_blocks // ATT_UNROLL, trip, 0)

    o_ref[...] = jnp.concatenate([os_ref[sl] for sl in range(nslab)], axis=-1).astype(o_ref.dtype)


def _attn_prompt(q, k, v, bias_tabs, *, span):
    b, t, w = q.shape
    nslab = w // LANES
    cur = pl.BlockSpec((None, span, w), lambda i, j: (i, j, 0))
    prev = pl.BlockSpec((None, span, w), lambda i, j: (i, jnp.maximum(j - 1, 0), 0))
    slab = lambda rows: pltpu.VMEM((nslab, rows, LANES), F32)
    return pl.pallas_call(
        functools.partial(_attn_prompt_kernel, span=span),
        grid=(b, t // span),
        in_specs=[_const_spec(bias_tabs.shape, lambda i, j: (0, 0, 0)), cur, prev, cur, prev, cur],
        out_specs=cur,
        out_shape=jax.ShapeDtypeStruct((b, t, w), BF16),
        scratch_shapes=[slab(span), slab(2 * span), slab(2 * span), slab(span), slab(span)],
        compiler_params=_params(("parallel", "arbitrary")),
        name="attn_prompt",
    )(bias_tabs, q, k, k, v, v)


def _sample_bias_tables(n_past, t_new, t_pad):
    slopes = _alibi_slopes()
    n_keys = n_past + t_pad
    tabs = np.full((len(DILATIONS), ATT_HEADS * t_pad, n_keys), NEG_BIG, np.float32)
    for br, dil in enumerate(DILATIONS):
        for h, s in enumerate(slopes):
            for t in range(t_pad):
                tq = min(t, t_new - 1)
                for jj in range(ATT_BLK + 1):
                    idx = n_past + tq - jj * dil
                    if idx < 0:
                        continue
                    tabs[br, h * t_pad + t, idx] = -np.float32(s) * np.float32(jj * dil)
    return tabs[:, :, :n_past], tabs[:, :, n_past:]


def _attn_sample_kernel(bc_ref, bn_ref, q_ref, kn_ref, vn_ref, kc_ref, vc_ref, o_ref):
    for b in range(q_ref.shape[0]):
        _attn_sample_one(bc_ref, bn_ref, q_ref.at[b], kn_ref.at[b], vn_ref.at[b], kc_ref.at[b], vc_ref.at[b],
                         o_ref.at[b])


def _attn_sample_one(bc_ref, bn_ref, q_ref, kn_ref, vn_ref, kc_ref, vc_ref, o_ref):
    t_pad, w = q_ref.shape
    scale = ATT_HDIM ** -0.5
    q = q_ref[...] * scale
    lane = lax.broadcasted_iota(jnp.int32, (t_pad, w), 1)
    head_of_lane = lane // ATT_HDIM
    zero = jnp.zeros_like(q)
    lhs = jnp.concatenate([jnp.where(head_of_lane == h, q, zero) for h in range(ATT_HEADS)], axis=0).astype(BF16)
    s_c = _dot(lhs, kc_ref[...].astype(BF16))
    s_n = _dot_nt(lhs, kn_ref[...].astype(BF16))
    vc = vc_ref[...].astype(BF16)
    vn = vn_ref[...].astype(BF16)
    pcs, pns, ls, lses = [], [], [], []
    for br in range(len(DILATIONS)):
        bc = bc_ref[br]
        bn = bn_ref[br]
        sc = jnp.where(bc > 0.5 * NEG_BIG, s_c + bc, NEG_BIG)
        sn = jnp.where(bn > 0.5 * NEG_BIG, s_n + bn, NEG_BIG)
        m = jnp.maximum(jnp.max(sc, axis=-1, keepdims=True), jnp.max(sn, axis=-1, keepdims=True))
        pc = jnp.exp(sc - m)
        pn = jnp.exp(sn - m)
        l = jnp.sum(pc, axis=-1, keepdims=True) + jnp.sum(pn, axis=-1, keepdims=True)
        pcs.append(pc.astype(BF16))
        pns.append(pn.astype(BF16))
        ls.append(l)
        lses.append(m + jnp.log(l))
    nq = ATT_HEADS * t_pad
    pv = _dot_nt(jnp.concatenate(pcs, axis=0), vc) + _dot(jnp.concatenate(pns, axis=0), vn)
    outs = [pv[br * nq:(br + 1) * nq] / ls[br] for br in range(len(DILATIONS))]
    mx = functools.reduce(jnp.maximum, lses)
    es = [jnp.exp(x - mx) for x in lses]
    den = functools.reduce(lambda a, b: a + b, es)
    o = functools.reduce(lambda a, b: a + b, [e * x for e, x in zip(es, outs)]) / den
    res = zero
    for h in range(ATT_HEADS):
        res = jnp.where(head_of_lane == h, o[h * t_pad:(h + 1) * t_pad, :], res)
    o_ref[...] = res.astype(o_ref.dtype)


def _attn_sample(q, k, v, cache_k, cache_v, layer, bias_c, bias_n, *, nb):
    b, t_pad, w = q.shape
    n_past = cache_k.shape[3]
    new = pl.BlockSpec((nb, t_pad, w), lambda i: (i, 0, 0))
    cache = pl.BlockSpec((None, nb, w, n_past), lambda i: (layer, i, 0, 0))
    return pl.pallas_call(
        _attn_sample_kernel,
        grid=(b // nb,),
        in_specs=[_const_spec(bias_c.shape, lambda i: (0, 0, 0)),
                  _const_spec(bias_n.shape, lambda i: (0, 0, 0)),
                  new, new, new, cache, cache],
        out_specs=new,
        out_shape=jax.ShapeDtypeStruct((b, t_pad, w), F32),
        compiler_params=_params(("parallel",)),
        name="attn_sample",
    )(bias_c, bias_n, q, k, v, cache_k, cache_v)


def _hgrn_tables(chunk):
    sub = min(chunk, LANES)
    r = np.arange(sub)[:, None]
    c = np.arange(sub)[None, :]
    x = np.bitwise_xor(r, c)
    hb = np.zeros_like(x)
    for bit in range(1, 16):
        hb = np.where(x >> bit > 0, bit, hb)
    lev = np.where(c > r, -1, np.where(r // SUBLANES == c // SUBLANES, 0, hb)).astype(np.int32)
    lev = np.tile(lev, (chunk // sub, 1))
    sel = np.zeros((SUBLANES * LANES, sub), np.float32)
    for s in range(SUBLANES):
        sel[s * LANES:(s + 1) * LANES, s::SUBLANES] = 1.0
    rr = np.arange(chunk)
    tri = (rr[None, :] <= rr[:, None]).astype(np.float32)
    return lev, sel, tri


def _split3(x):
    hi = x.astype(BF16)
    r1 = x - hi.astype(F32)
    mid = r1.astype(BF16)
    lo = (r1 - mid.astype(F32)).astype(BF16)
    return hi, mid, lo


def _group_row(x, s):
    c, l = x.shape
    x3 = x.reshape(c // SUBLANES, SUBLANES, l)
    return jnp.broadcast_to(x3[:, s:s + 1, :], x3.shape).reshape(c, l)


def _block_last(x, m):
    c, l = x.shape
    x3 = x.reshape(c // m, m, l)
    return jnp.broadcast_to(x3[:, m - 1:m, :], x3.shape).reshape(c, l)


def _hgrn_lower_bound(raw, layer):
    e = jnp.exp(raw - jnp.max(raw, axis=0, keepdims=True))
    sm = e / jnp.sum(e, axis=0, keepdims=True)
    lb = jnp.zeros_like(sm[0:1])
    for i in range(1, layer + 1):
        lb = lb + sm[i:i + 1]
    return lb


def _hgrn_chunk_gates(zq, zf, zi, lb, tri, live):
    dk = zq.shape[1]
    q = _silu(zq)
    f = lb + (1.0 - lb) * jax.nn.sigmoid(zf)
    g = jnp.log2(f)
    k = 1.0 - f
    if live is not None:
        g = jnp.where(live, g, 0.0)
        k = jnp.where(live, k, 0.0)
    b3 = _dot(tri, jnp.concatenate(_split3(g), axis=1))
    b = b3[:, :dk] + b3[:, dk:2 * dk] + b3[:, 2 * dk:]
    return q, k, b, zi.astype(BF16)


def _hgrn_chunk_local(q, k, b, vb, lev, sel):
    chunk, dk = q.shape
    sub = lev.shape[1]

    vals = []
    for s in range(SUBLANES):
        decay = jnp.exp2(jnp.minimum(b - _group_row(b, s), 0.0))
        vals.append((q * decay * _group_row(k, s)).astype(BF16))
    a = jnp.where(lev == 0, _dot(jnp.concatenate(vals, axis=1), sel), 0.0)
    subs = [slice(r0, r0 + sub) for r0 in range(0, chunk, sub)]
    for bit in range(3, int(math.log2(sub))):
        m = 1 << bit
        b_end = _block_last(b, m)
        b_prev = jnp.concatenate([jnp.zeros((m, dk), F32), b_end[:chunk - m]], axis=0)
        qd = (q * jnp.exp2(b - b_prev)).astype(BF16)
        kd = (k * jnp.exp2(b_end - b)).astype(BF16)
        a_l = jnp.concatenate([_dot_nt(qd[rs], kd[rs]) for rs in subs], axis=0)
        a = jnp.where(lev == bit, a_l, a)
    ab = a.astype(BF16)
    o_parts = []
    for i, rs in enumerate(subs):
        if i == 0:
            o_parts.append(_dot(ab[rs], vb[rs]))
            continue
        b_piv = b[rs.start - 1:rs.start, :]
        qd = (q[rs] * jnp.exp2(b[rs] - b_piv)).astype(BF16)
        kd = (k[:rs.start] * jnp.exp2(b_piv - b[:rs.start])).astype(BF16)
        a_row = jnp.concatenate([_dot_nt(qd, kd).astype(BF16), ab[rs]], axis=1)
        o_parts.append(_dot(a_row, vb[:rs.stop]))
    o = jnp.concatenate(o_parts, axis=0) if len(o_parts) > 1 else o_parts[0]
    q_in = (q * jnp.exp2(b)).astype(BF16)
    b_last = b[chunk - 1:chunk, :]
    kd = (k * jnp.exp2(b_last - b)).astype(BF16)
    return o, q_in, _dot_tn(vb, kd), jnp.exp2(b_last)


def _hgrn_chunk_finish(o_local, q_in, st, ng, zg):
    o = o_local + _dot_nt(q_in, st.astype(BF16))
    on = o * lax.rsqrt(jnp.mean(o * o, axis=-1, keepdims=True) + EPS) * ng
    return on * _silu(zg)


def _hgrn_sample_kernel(zh_ref, lb_ref, ng_ref, lev_ref, sel_ref, tri_ref, s0_ref, o_ref, sf_ref,
                        *, nb, layer, t_valid):
    t_pad = zh_ref.shape[1]
    hd = o_ref.shape[2]
    dk = hd // HG_HEADS
    lev = lev_ref[...]
    sel = sel_ref[...]
    tri = tri_ref[...]
    ng = ng_ref[...]
    live = lax.broadcasted_iota(jnp.int32, (t_pad, dk), 0) < t_valid
    lbs = [_hgrn_lower_bound(lb_ref[:, h * dk:(h + 1) * dk], layer) for h in range(HG_HEADS)]

    def one_batch(bi, carry):
        for h in range(HG_HEADS):
            part = lambda p: zh_ref[bi, :, (p * HG_HEADS + h) * dk:(p * HG_HEADS + h + 1) * dk]
            gates = _hgrn_chunk_gates(part(0), part(1), part(2), lbs[h], tri, live)
            o_local, q_in, s_own, d_all = _hgrn_chunk_local(*gates, lev, sel)
            st = s0_ref[bi, h].T
            o_ref[bi, :, h * dk:(h + 1) * dk] = _hgrn_chunk_finish(o_local, q_in, st, ng, part(3))
            sf_ref[bi, h] = (st * d_all + s_own).T
        return carry

    lax.fori_loop(0, nb, one_batch, 0, unroll=2)


def _hgrn_sample(zh, lb_raw, norm_g, s0, layer, tables, *, nb, t_valid):
    b, t_pad, w4 = zh.shape
    hd = w4 // 4
    dk = hd // HG_HEADS
    depth = lb_raw.shape[0]
    lev, sel, tri = tables
    return pl.pallas_call(
        functools.partial(_hgrn_sample_kernel, nb=nb, layer=layer, t_valid=t_valid),
        grid=(b // nb,),
        in_specs=[pl.BlockSpec((nb, t_pad, w4), lambda i: (i, 0, 0)),
                  pl.BlockSpec((depth, hd), lambda i: (0, 0)),
                  pl.BlockSpec((None, 1, dk), lambda i: (layer, 0, 0)),
                  _const_spec(lev.shape, lambda i: (0, 0)),
                  _const_spec(sel.shape, lambda i: (0, 0)),
                  _const_spec(tri.shape, lambda i: (0, 0)),
                  pl.BlockSpec((None, nb, HG_HEADS, dk, dk), lambda i: (layer, i, 0, 0, 0))],
        out_specs=[pl.BlockSpec((nb, t_pad, hd), lambda i: (i, 0, 0)),
                   pl.BlockSpec((nb, HG_HEADS, dk, dk), lambda i: (i, 0, 0, 0))],
        out_shape=[jax.ShapeDtypeStruct((b, t_pad, hd), F32),
                   jax.ShapeDtypeStruct((b, HG_HEADS, dk, dk), F32)],
        compiler_params=_params(("parallel",)),
        name="hgrn_sample",
    )(zh, lb_raw, norm_g, lev, sel, tri, s0)


def _proj_mix_kernel(x_ref, ln_ref, wi_ref, dww_ref, dwb_ref, lng_ref, lnb_ref, lb_ref, ng_ref,
                     lev_ref, sel_ref, tri_ref,
                     yc_ref, q_ref, k_ref, v_ref, yh_ref, tail_ref, sf_ref, kt_ref, vt_ref,
                     ext_ref, st_ref, *, layer, chunk):
    tstep = pl.program_id(1)
    tt = x_ref.shape[0]
    conv_c = yc_ref.shape[1]
    att = q_ref.shape[1]
    hd = yh_ref.shape[1]
    dk = hd // HG_HEADS
    n_chunks = tt // chunk

    @pl.when(tstep == 0)
    def _():
        ext_ref[0, 0:CONV_HALO, :] = jnp.zeros((CONV_HALO, conv_c), F32)
        ext_ref[0, CONV_HALO + tt:CONV_HALO + tt + SUBLANES, :] = jnp.zeros((SUBLANES, conv_c), F32)
        st_ref[...] = jnp.zeros_like(st_ref)

    h = _rms(x_ref[...], ln_ref[...]).astype(BF16)
    c0 = 2 * conv_c
    zc = _dot(h, wi_ref[:, 0:c0])
    zh = _dot(h, wi_ref[:, c0 + 3 * att:c0 + 3 * att + 4 * hd])
    yc, u = _conv_tile(zc, ext_ref, 0, dww_ref, dwb_ref[...], lng_ref[...], lnb_ref[...])
    yc_ref[...] = yc.astype(yc_ref.dtype)
    tail_ref[...] = u[tt - CONV_HALO:, :]
    for i, (ref, t_ref) in enumerate(((q_ref, None), (k_ref, kt_ref), (v_ref, vt_ref))):
        z = _dot(h, wi_ref[:, c0 + i * att:c0 + (i + 1) * att])
        ref[...] = z
        if t_ref is not None:
            t_ref[...] = z.T

    lev = lev_ref[...]
    sel = sel_ref[...]
    tri = tri_ref[...]
    ng = ng_ref[...]
    lbs = [_hgrn_lower_bound(lb_ref[:, hh * dk:(hh + 1) * dk], layer) for hh in range(HG_HEADS)]

    def part(u_idx, p):
        hh, c = divmod(u_idx, n_chunks)
        return zh[c * chunk:(c + 1) * chunk, (p * HG_HEADS + hh) * dk:(p * HG_HEADS + hh + 1) * dk]

    def gates(u_idx):
        return _hgrn_chunk_gates(part(u_idx, 0), part(u_idx, 1), part(u_idx, 2), lbs[u_idx // n_chunks], tri, None)

    ahead = gates(0)
    st = None
    for u_idx in range(HG_HEADS * n_chunks):
        hh, c = divmod(u_idx, n_chunks)
        cur = ahead
        if u_idx + 1 < HG_HEADS * n_chunks:
            ahead = gates(u_idx + 1)
        if c == 0:
            st = st_ref[hh]
        o_local, q_in, s_own, d_all = _hgrn_chunk_local(*cur, lev, sel)
        o = _hgrn_chunk_finish(o_local, q_in, st, ng, part(u_idx, 3))
        yh_ref[c * chunk:(c + 1) * chunk, hh * dk:(hh + 1) * dk] = o.astype(yh_ref.dtype)
        st = st * d_all + s_own
        if c == n_chunks - 1:
            st_ref[hh] = st

    @pl.when(tstep == pl.num_programs(1) - 1)
    def _():
        for hh in range(HG_HEADS):
            sf_ref[hh] = st_ref[hh].T


def _proj_mix(x, layer, ln, wi, dww, dwb, lng, lnb, lb_raw, norm_g, tables, *, batch, tm, chunk, widths, keep):
    n, d = x.shape
    n_in = wi.shape[2]
    conv_c = widths[0] // 2
    att = widths[1]
    hd = widths[4] // 4
    dk = hd // HG_HEADS
    depth = lb_raw.shape[0]
    nt = n // batch // tm
    lev, sel, tri = tables
    row = lambda w: pl.BlockSpec((tm, w), lambda b, j: (b * nt + j, 0))
    cvec = pl.BlockSpec((None, 1, conv_c), lambda b, j: (layer, 0, 0))
    const2 = lambda a: _const_spec(a.shape, lambda b, j: (0, 0))
    first_kept = nt - keep // tm
    window = pl.BlockSpec((None, att, tm), lambda b, j: (b, 0, jnp.maximum(j - first_kept, 0)))
    return pl.pallas_call(
        functools.partial(_proj_mix_kernel, layer=layer, chunk=chunk),
        grid=(batch, nt),
        in_specs=[row(d), pl.BlockSpec((None, 1, d), lambda b, j: (layer, 0, 0)),
                  _const_spec((None, d, n_in), lambda b, j: (layer, 0, 0)),
                  pl.BlockSpec((None, CONV_HALO, conv_c), lambda b, j: (layer, 0, 0)),
                  cvec, cvec, cvec,
                  pl.BlockSpec((depth, hd), lambda b, j: (0, 0)),
                  pl.BlockSpec((None, 1, dk), lambda b, j: (layer, 0, 0)),
                  const2(lev), const2(sel), const2(tri)],
        out_specs=[row(conv_c), row(att), row(att), row(att), row(hd),
                   pl.BlockSpec((None, CONV_HALO, conv_c), lambda b, j: (b, 0, 0)),
                   pl.BlockSpec((None, HG_HEADS, dk, dk), lambda b, j: (b, 0, 0, 0)),
                   window, window],
        out_shape=[jax.ShapeDtypeStruct((n, conv_c), BF16),
                   jax.ShapeDtypeStruct((n, att), F32),
                   jax.ShapeDtypeStruct((n, att), F32),
                   jax.ShapeDtypeStruct((n, att), F32),
                   jax.ShapeDtypeStruct((n, hd), BF16),
                   jax.ShapeDtypeStruct((batch, CONV_HALO, conv_c), F32),
                   jax.ShapeDtypeStruct((batch, HG_HEADS, dk, dk), F32),
                   jax.ShapeDtypeStruct((batch, att, keep), F32),
                   jax.ShapeDtypeStruct((batch, att, keep), F32)],
        scratch_shapes=[pltpu.VMEM((1, CONV_HALO + tm + SUBLANES, conv_c), F32),
                        pltpu.VMEM((HG_HEADS, dk, dk), F32)],
        compiler_params=_params(("parallel", "arbitrary")),
        name="proj_mix",
    )(x, ln, wi, dww, dwb, lng, lnb, lb_raw, norm_g, lev, sel, tri)


def _pad_rows(x, rows):
    return jnp.pad(x, ((0, 0), (0, rows - x.shape[1]), (0, 0)))


def _key_minor(cache):
    depth, b, n_past, h, hd = cache.shape
    return jnp.transpose(cache, (0, 1, 3, 4, 2)).reshape(depth, b, h * hd, n_past)


def kernel(x_prompt, x_sample, state_conv, cache_k_win, cache_v_win, state_hgrn, ln_ffn1, w_ffn1_gate, w_ffn1_up, w_ffn1_down, ln_mix, w_in, conv_dw_w, conv_dw_b, conv_ln_g, conv_ln_b, hg_lower_bounds, hg_norm_g, w_out, ln_ffn2, w_ffn2_gate, w_ffn2_up, w_ffn2_down, ln_final):
    bp, seq, d = x_prompt.shape
    bs, t_new, _ = x_sample.shape
    depth = w_in.shape[0]
    conv_dim = conv_dw_w.shape[2]
    att_dim = ATT_HEADS * ATT_HDIM
    hg_dim = hg_lower_bounds.shape[1]
    widths = (2 * conv_dim, att_dim, att_dim, att_dim, 4 * hg_dim)
    n_past = cache_k_win.shape[2]
    t_pad = SUBLANES
    keep = min(ATT_SPAN, seq)

    ffn1_f32 = (w_ffn1_gate, w_ffn1_up, w_ffn1_down)
    ffn2_f32 = (w_ffn2_gate, w_ffn2_up, w_ffn2_down)
    ffn1_w = tuple(w[0].astype(BF16) for w in ffn1_f32)
    ffn2_w = tuple(w[0].astype(BF16) for w in ffn2_f32)
    wi, wo = w_in.astype(BF16), w_out.astype(BF16)
    vec3 = lambda a: a.reshape(depth, 1, a.shape[-1])
    ln1, lnm, ln2 = vec3(ln_ffn1), vec3(ln_mix), vec3(ln_ffn2)
    dwb, clg, clb, hgn = vec3(conv_dw_b), vec3(conv_ln_g), vec3(conv_ln_b), vec3(hg_norm_g)
    lnf = ln_final.reshape(1, d)
    dww = jnp.pad(conv_dw_w, ((0, 0), (0, CONV_HALO - CONV_WIDTH), (0, 0)))
    cache_k, cache_v = _key_minor(cache_k_win), _key_minor(cache_v_win)
    halo_s = jnp.pad(state_conv, ((0, 0), (0, 0), (CONV_HALO - (CONV_WIDTH - 1), 0), (0, 0)))

    bias_p = jnp.asarray(_prompt_bias_tables())
    bias_c, bias_n = (jnp.asarray(a) for a in _sample_bias_tables(n_past, t_new, t_pad))
    chunk_p = 128
    tabs_p = tuple(jnp.asarray(a, dt) for a, dt in zip(_hgrn_tables(chunk_p), (jnp.int32, BF16, BF16)))
    tabs_s = tuple(jnp.asarray(a, dt) for a, dt in zip(_hgrn_tables(t_pad), (jnp.int32, BF16, BF16)))

    hp = x_prompt.reshape(bp * seq, d)
    hs = x_sample.reshape(bs * t_new, d)
    outs = [[] for _ in range(8)]
    for l in range(depth):
        final = l == depth - 1
        nxt1 = ffn1_f32 if not final else ()
        nxt2 = ffn2_f32 if not final else ()
        x1, x1s, ffn1_next = _ffn_half_step(hp, hs, l, ln1, *ffn1_w, nxt1, l + 1, tm=512, f_chunk=1408)
        yc, q, k, v, yh, tail, sp, kt, vt = _proj_mix(x1, l, lnm, wi, dww, dwb, clg, clb, hg_lower_bounds, hgn,
                                                      tabs_p, batch=bp, tm=512, chunk=chunk_p, widths=widths,
                                                      keep=keep)
        seq3 = lambda a: a.reshape(bp, seq, a.shape[-1])
        ya = _attn_prompt(seq3(q), seq3(k), seq3(v), bias_p, span=ATT_SPAN)
        window = lambda a: a.reshape(bp, ATT_HEADS, ATT_HDIM, keep).transpose(0, 3, 1, 2)
        outs[0].append(tail[:, -(CONV_WIDTH - 1):])
        outs[2].append(window(kt))
        outs[3].append(window(vt))
        outs[6].append(sp)

        zcs, qs, ks, vs, zhs = _proj_in(x1s, l, lnm, wi, tm=bs * t_new, widths=widths)
        new3 = lambda a: _pad_rows(a.reshape(bs, t_new, a.shape[-1]), t_pad)
        ycs, us = _conv_group(new3(zcs), halo_s[l], l, dww, dwb, clg, clb, nb=bs, tt=t_pad, out_dtype=F32)
        yas = _attn_sample(new3(qs), new3(ks), new3(vs), cache_k, cache_v, l, bias_c, bias_n, nb=2)
        yhs, ss = _hgrn_sample(new3(zhs), hg_lower_bounds, hgn, state_hgrn, l, tabs_s, nb=8, t_valid=t_new)
        unpad = lambda a: a[:, :t_new].reshape(bs * t_new, a.shape[-1])
        hp, hs, ffn2_next = _dense_out((x1, yc, ya.reshape(bp * seq, att_dim), yh),
                                       (x1s, unpad(ycs), unpad(yas), unpad(yhs)),
                                       l, wo, ln2, *ffn2_w, lnf, nxt2, l + 1, tm=512, f_chunk=1408, final=final)
        ffn1_w, ffn2_w = ffn1_next, ffn2_next
        outs[1].append(jnp.concatenate([state_conv[l][:, t_new:], us[:, :t_new]], axis=1))
        outs[4].append(ks.reshape(bs, t_new, ATT_HEADS, ATT_HDIM))
        outs[5].append(vs.reshape(bs, t_new, ATT_HEADS, ATT_HDIM))
        outs[7].append(ss)

    y_prompt = hp.reshape(bp, seq, d)
    y_sample = hs.reshape(bs, t_new, d)
    st = [jnp.stack(o) for o in outs]
    return (y_prompt, y_sample, st[0], st[1], st[2], st[3], st[4], st[5], st[6], st[7])
```

```python
import functools
import math

import numpy as np
import jax
import jax.numpy as jnp
from jax import lax
from jax.experimental import pallas as pl
from jax.experimental.pallas import tpu as pltpu

F32 = jnp.float32
BF16 = jnp.bfloat16

EPS = 1e-6
NEG_BIG = -1e30
CONV_WIDTH = 31
CONV_HALO = 32
ATT_HEADS = 4
ATT_HDIM = 64
ATT_BLK = 128
DILATIONS = (1, 4, 16)
ATT_SPAN = 2048
ATT_UNROLL = 8
HG_HEADS = 4
LANES = 128
SUBLANES = 8
BF16_ROWS = 16
VMEM_LIMIT = 56 * 1024 * 1024


def _params(sem, vmem=VMEM_LIMIT):
    return pltpu.CompilerParams(dimension_semantics=sem, vmem_limit_bytes=vmem)


def _const_spec(shape, index):
    return pl.BlockSpec(shape, index, pipeline_mode=pl.Buffered(1))


def _rms(x, g):
    ms = jnp.mean(x * x, axis=-1, keepdims=True)
    return x * lax.rsqrt(ms + EPS) * g


def _silu(x):
    return x * jax.nn.sigmoid(x)


def _dot(a, b):
    return jnp.dot(a, b, preferred_element_type=F32)


def _dot_nt(a, b):
    return lax.dot_general(a, b, (((1,), (1,)), ((), ())), preferred_element_type=F32)


def _dot_tn(a, b):
    return lax.dot_general(a, b, (((0,), (0,)), ((), ())), preferred_element_type=F32)


def _swiglu(h, wg_ref, wu_ref, wd_ref, f_chunk):
    ffn = wg_ref.shape[1]
    acc = None
    for c0 in range(0, ffn, f_chunk):
        g = _dot(h, wg_ref[:, c0:c0 + f_chunk])
        u = _dot(h, wu_ref[:, c0:c0 + f_chunk])
        a = (_silu(g) * u).astype(BF16)
        y = _dot(a, wd_ref[c0:c0 + f_chunk, :])
        acc = y if acc is None else acc + y
    return acc


def _main_then_extra(n_main, main, extra):
    i = pl.program_id(0)
    pl.when(i < n_main)(main)
    pl.when(i == n_main)(extra)


def _cast_specs(stacked, layer, n_steps):
    in_specs, out_specs, shapes = [], [], []
    for w in stacked:
        _, r, c = w.shape
        rows = -(-r // n_steps)
        rows = -(-rows // BF16_ROWS) * BF16_ROWS
        while r % rows:
            rows += BF16_ROWS
        last = r // rows - 1
        in_specs.append(pl.BlockSpec((None, rows, c), lambda i, last=last: (layer, jnp.minimum(i, last), 0)))
        out_specs.append(pl.BlockSpec((rows, c), lambda i, last=last: (jnp.minimum(i, last), 0)))
        shapes.append(jax.ShapeDtypeStruct((r, c), BF16))
    return in_specs, out_specs, shapes


def _cast_slabs(src_refs, dst_refs):
    for src, dst in zip(src_refs, dst_refs):
        dst[...] = src[...].astype(dst.dtype)


def _ffn_kernel(*refs, f_chunk, n_main, n_cast):
    xp_ref, xs_ref, ln_ref, wg_ref, wu_ref, wd_ref = refs[:6]
    cast_src = refs[6:6 + n_cast]
    op_ref, os_ref = refs[6 + n_cast:8 + n_cast]
    cast_dst = refs[8 + n_cast:]

    def run(x_ref, o_ref):
        x = x_ref[...]
        h = _rms(x, ln_ref[...]).astype(BF16)
        o_ref[...] = x + 0.5 * _swiglu(h, wg_ref, wu_ref, wd_ref, f_chunk)

    _main_then_extra(n_main, lambda: run(xp_ref, op_ref), lambda: run(xs_ref, os_ref))
    _cast_slabs(cast_src, cast_dst)


def _ffn_half_step(xp, xs, layer, ln, wg, wu, wd, cast, cast_layer, *, tm, f_chunk):
    n, d = xp.shape
    ns = xs.shape[0]
    ffn = wg.shape[1]
    n_main = n // tm
    row = pl.BlockSpec((tm, d), lambda i: (jnp.minimum(i, n_main - 1), 0))
    extra = pl.BlockSpec((ns, d), lambda i: (0, 0))
    c_in, c_out, c_shapes = _cast_specs(cast, cast_layer, n_main + 1)
    res = pl.pallas_call(
        functools.partial(_ffn_kernel, f_chunk=f_chunk, n_main=n_main, n_cast=len(cast)),
        grid=(n_main + 1,),
        in_specs=[row, extra, pl.BlockSpec((None, 1, d), lambda i: (layer, 0, 0)),
                  _const_spec((d, ffn), lambda i: (0, 0)),
                  _const_spec((d, ffn), lambda i: (0, 0)),
                  _const_spec((ffn, d), lambda i: (0, 0))] + c_in,
        out_specs=[row, extra] + c_out,
        out_shape=[jax.ShapeDtypeStruct((n, d), F32), jax.ShapeDtypeStruct((ns, d), F32)] + c_shapes,
        compiler_params=_params(("arbitrary",)),
        name="ffn_half_step",
    )(xp, xs, ln, wg, wu, wd, *cast)
    return res[0], res[1], tuple(res[2:])


def _proj_in_kernel(x_ref, ln_ref, wi_ref, zc_ref, q_ref, k_ref, v_ref, zh_ref):
    h = _rms(x_ref[...], ln_ref[...]).astype(BF16)
    c0 = 0
    for ref in (zc_ref, q_ref, k_ref, v_ref, zh_ref):
        w = ref.shape[1]
        ref[...] = _dot(h, wi_ref[:, c0:c0 + w])
        c0 += w


def _proj_in(x, layer, ln, wi, *, tm, widths):
    n, d = x.shape
    row = lambda w: pl.BlockSpec((tm, w), lambda i: (i, 0))
    return pl.pallas_call(
        _proj_in_kernel,
        grid=(n // tm,),
        in_specs=[row(d), pl.BlockSpec((None, 1, d), lambda i: (layer, 0, 0)),
                  _const_spec(wi.shape, lambda i: (0, 0))],
        out_specs=[row(w) for w in widths],
        out_shape=[jax.ShapeDtypeStruct((n, w), F32) for w in widths],
        compiler_params=_params(("parallel",)),
        name="proj_in",
    )(x, ln, wi)


def _dense_out_kernel(*refs, f_chunk, final, n_main, n_cast):
    (xp_ref, ycp_ref, yap_ref, yhp_ref, xs_ref, ycs_ref, yas_ref, yhs_ref,
     wo_ref, ln2_ref, wg_ref, wu_ref, wd_ref, lnf_ref) = refs[:14]
    cast_src = refs[14:14 + n_cast]
    op_ref, os_ref = refs[14 + n_cast:16 + n_cast]
    cast_dst = refs[16 + n_cast:]

    def run(x_ref, yc_ref, ya_ref, yh_ref, o_ref):
        y = jnp.concatenate([r[...].astype(BF16) for r in (yc_ref, ya_ref, yh_ref)], axis=-1)
        x2 = x_ref[...] + _dot(y, wo_ref[...])
        h = _rms(x2, ln2_ref[...]).astype(BF16)
        x3 = x2 + 0.5 * _swiglu(h, wg_ref, wu_ref, wd_ref, f_chunk)
        if final:
            x3 = _rms(x3, lnf_ref[...])
        o_ref[...] = x3

    _main_then_extra(n_main, lambda: run(xp_ref, ycp_ref, yap_ref, yhp_ref, op_ref),
                     lambda: run(xs_ref, ycs_ref, yas_ref, yhs_ref, os_ref))
    _cast_slabs(cast_src, cast_dst)


def _dense_out(prompt, sample, layer, wo, ln2, wg, wu, wd, lnf, cast, cast_layer, *, tm, f_chunk, final):
    n, d = prompt[0].shape
    ns = sample[0].shape[0]
    ffn = wg.shape[1]
    n_main = n // tm
    row = lambda a: pl.BlockSpec((tm, a.shape[1]), lambda i: (jnp.minimum(i, n_main - 1), 0))
    extra = lambda a: pl.BlockSpec((ns, a.shape[1]), lambda i: (0, 0))
    vec = pl.BlockSpec((None, 1, d), lambda i: (layer, 0, 0))
    c_in, c_out, c_shapes = _cast_specs(cast, cast_layer, n_main + 1)
    res = pl.pallas_call(
        functools.partial(_dense_out_kernel, f_chunk=f_chunk, final=final, n_main=n_main, n_cast=len(cast)),
        grid=(n_main + 1,),
        in_specs=[row(a) for a in prompt] + [extra(a) for a in sample] + [
                  _const_spec((d, d), lambda i: (0, 0)),
                  vec,
                  _const_spec((d, ffn), lambda i: (0, 0)),
                  _const_spec((d, ffn), lambda i: (0, 0)),
                  _const_spec((ffn, d), lambda i: (0, 0)),
                  pl.BlockSpec((1, d), lambda i: (0, 0))] + c_in,
        out_specs=[row(prompt[0]), extra(sample[0])] + c_out,
        out_shape=[jax.ShapeDtypeStruct((n, d), F32), jax.ShapeDtypeStruct((ns, d), F32)] + c_shapes,
        compiler_params=_params(("arbitrary",)),
        name="dense_out",
    )(*prompt, *sample, wo, ln2, wg, wu, wd, lnf, *cast)
    return res[0], res[1], tuple(res[2:])


def _conv_tile(z, ext_ref, b, dww_ref, dwb, lng, lnb):
    tt, c2 = z.shape
    c = c2 // 2
    lo = CONV_HALO - (CONV_WIDTH - 1)
    u = z[:, :c] * jax.nn.sigmoid(z[:, c:])
    ext_ref[b, CONV_HALO:CONV_HALO + tt, :] = u
    y = None
    for r in range(SUBLANES):
        z_r = None
        for o in range(r, CONV_HALO + 1, SUBLANES):
            if o < lo:
                continue
            term = ext_ref[b, o - r:o - r + tt + SUBLANES, :] * dww_ref[o - lo:o - lo + 1, :]
            z_r = term if z_r is None else z_r + term
        part = z_r[r:r + tt, :]
        y = part if y is None else y + part
    y = y + dwb
    yc = y - jnp.mean(y, axis=-1, keepdims=True)
    yn = yc * lax.rsqrt(jnp.mean(yc * yc, axis=-1, keepdims=True) + EPS)
    carry = ext_ref[b, tt:tt + CONV_HALO, :]
    ext_ref[b, 0:CONV_HALO, :] = carry
    return _silu(yn * lng + lnb), u


def _conv_kernel(zc_ref, halo_ref, dww_ref, dwb_ref, lng_ref, lnb_ref, y_ref, tail_ref, ext_ref, *, nb, tt, tail):
    c = y_ref.shape[-1]

    def one_batch(b):
        @pl.when(pl.program_id(1) == 0)
        def _():
            ext_ref[b, 0:CONV_HALO, :] = halo_ref[b]
            ext_ref[b, CONV_HALO + tt:CONV_HALO + tt + SUBLANES, :] = jnp.zeros((SUBLANES, c), F32)

        y, u = _conv_tile(zc_ref[b], ext_ref, b, dww_ref, dwb_ref[...], lng_ref[...], lnb_ref[...])
        y_ref[b] = y.astype(y_ref.dtype)
        tail_ref[b] = u[tt - tail:, :]

    if nb == 1:
        one_batch(0)
    else:
        def body(b, carry):
            one_batch(b)
            return carry
        lax.fori_loop(0, nb, body, 0)


def _conv_group(zc, halo, layer, dww, dwb, lng, lnb, *, nb, tt, out_dtype):
    b, t, c2 = zc.shape
    c = c2 // 2
    tail = min(CONV_HALO, tt)
    vec = pl.BlockSpec((None, 1, c), lambda i, j: (layer, 0, 0))
    return pl.pallas_call(
        functools.partial(_conv_kernel, nb=nb, tt=tt, tail=tail),
        grid=(b // nb, t // tt),
        in_specs=[pl.BlockSpec((nb, tt, c2), lambda i, j: (i, j, 0)),
                  pl.BlockSpec((nb, CONV_HALO, c), lambda i, j: (i, 0, 0)),
                  pl.BlockSpec((None, CONV_HALO, c), lambda i, j: (layer, 0, 0)),
                  vec, vec, vec],
        out_specs=[pl.BlockSpec((nb, tt, c), lambda i, j: (i, j, 0)),
                   pl.BlockSpec((nb, tail, c), lambda i, j: (i, 0, 0))],
        out_shape=[jax.ShapeDtypeStruct((b, t, c), out_dtype),
                   jax.ShapeDtypeStruct((b, tail, c), F32)],
        scratch_shapes=[pltpu.VMEM((nb, CONV_HALO + tt + SUBLANES, c), F32)],
        compiler_params=_params(("parallel", "arbitrary")),
        name="conv_group",
    )(zc, halo, dww, dwb, lng, lnb)


def _alibi_slopes():
    return [2.0 ** (-8.0 * (h + 1) / ATT_HEADS) for h in range(ATT_HEADS)]


def _prompt_bias_tables():
    qi = np.arange(ATT_BLK)[:, None]
    ki = np.arange(2 * ATT_BLK)[None, :]
    j = qi - ki + ATT_BLK
    tabs = []
    for dil in DILATIONS:
        for first in (False, True):
            valid = (j >= 0) & (j <= ATT_BLK) & ((not first) | (ki >= ATT_BLK))
            rows = []
            for s in _alibi_slopes():
                bias = (-np.float32(s)) * (j * dil).astype(np.float32)
                rows.append(np.where(valid, bias, np.float32(NEG_BIG)).astype(np.float32))
            tabs.append(np.concatenate(rows, axis=0))
    return np.stack(tabs, axis=0)


def _head_pair_split(x, lo_mask):
    zero = jnp.zeros_like(x)
    return jnp.concatenate([jnp.where(lo_mask, x, zero), jnp.where(lo_mask, zero, x)], axis=0)


def _attn_prompt_kernel(bias_ref, q_ref, kp_ref, kc_ref, vp_ref, vc_ref, o_ref,
                        qs_ref, ks_ref, vs_ref, os_ref, ls_ref, *, span):
    j = pl.program_id(1)
    nslab = qs_ref.shape[0]
    scale = ATT_HDIM ** -0.5
    for sl in range(nslab):
        cols = slice(sl * LANES, (sl + 1) * LANES)
        qs_ref[sl] = q_ref[:, cols] * scale
        ks_ref[sl, 0:span, :] = kp_ref[:, cols]
        ks_ref[sl, span:2 * span, :] = kc_ref[:, cols]
        vs_ref[sl, 0:span, :] = vp_ref[:, cols]
        vs_ref[sl, span:2 * span, :] = vc_ref[:, cols]

    lane = lax.broadcasted_iota(jnp.int32, (ATT_BLK, LANES), 1)
    lo_mask = lane < ATT_HDIM
    n_blocks = span // ATT_BLK

    order = sorted(range(len(DILATIONS)), key=lambda b: -DILATIONS[b])
    for pos, br in enumerate(order):
        dil = DILATIONS[br]

        def rows(start, size, dil=dil):
            if dil == 1:
                return pl.ds(pl.multiple_of(start, ATT_BLK), size)
            return pl.ds(start, size, stride=dil)

        def scores(i, dil=dil, rows=rows):
            q0 = (i // dil) * (ATT_BLK * dil) + i % dil
            k0 = span + q0 - ATT_BLK * dil
            s_parts, v_parts = [], []
            for sl in range(nslab):
                ql = qs_ref[sl, rows(q0, ATT_BLK), :]
                kl = ks_ref[sl, rows(k0, 2 * ATT_BLK), :]
                v_parts.append(vs_ref[sl, rows(k0, 2 * ATT_BLK), :].astype(BF16))
                lhs = _head_pair_split(ql, lo_mask).astype(BF16)
                s_parts.append(_dot_nt(lhs, kl.astype(BF16)))
            return jnp.concatenate(s_parts, axis=0), jnp.concatenate(v_parts, axis=1)

        def finish(i, s, v, br=br, dil=dil, pos=pos, rows=rows):
            q0 = (i // dil) * (ATT_BLK * dil) + i % dil
            first = jnp.logical_and(j == 0, i // dil == 0).astype(jnp.int32)
            bias = bias_ref[2 * br + first]
            s = jnp.where(bias > 0.5 * NEG_BIG, s + bias, NEG_BIG)
            m = jnp.max(s, axis=-1, keepdims=True)
            p = jnp.exp(s - m)
            l = jnp.sum(p, axis=-1, keepdims=True)
            pv = _dot(p.astype(BF16), v)
            lse_all = m + jnp.log(l)
            for sl in range(nslab):
                cols = slice(sl * LANES, (sl + 1) * LANES)
                ra = slice((2 * sl) * ATT_BLK, (2 * sl + 1) * ATT_BLK)
                rb = slice((2 * sl + 1) * ATT_BLK, (2 * sl + 2) * ATT_BLK)
                pair = lambda x: jnp.where(lo_mask, jnp.broadcast_to(x[ra], (ATT_BLK, LANES)),
                                           jnp.broadcast_to(x[rb], (ATT_BLK, LANES)))
                o_new = jnp.where(lo_mask, pv[ra, cols], pv[rb, cols]) / pair(l)
                lse_new = pair(lse_all)
                out_rows = rows(q0, ATT_BLK)
                if pos == 0:
                    o_tot, lse_tot = o_new, lse_new
                else:
                    o_old = os_ref[sl, out_rows, :]
                    lse_old = ls_ref[sl, out_rows, :]
                    mx = jnp.maximum(lse_old, lse_new)
                    e_old = jnp.exp(lse_old - mx)
                    e_new = jnp.exp(lse_new - mx)
                    den = e_old + e_new
                    o_tot = (e_old * o_old + e_new * o_new) / den
                    lse_tot = mx + jnp.log(den)
                os_ref[sl, out_rows, :] = o_tot
                ls_ref[sl, out_rows, :] = lse_tot

        def trip(t, carry, scores=scores, finish=finish):
            base = t * ATT_UNROLL
            ahead = scores(base)
            for u in range(ATT_UNROLL):
                s, v = ahead
                if u + 1 < ATT_UNROLL:
                    ahead = scores(base + u + 1)
                finish(base + u, s, v)
            return carry

        lax.fori_loop(0, n_blocks // ATT_UNROLL, trip, 0)

    o_ref[...] = jnp.concatenate([os_ref[sl] for sl in range(nslab)], axis=-1).astype(o_ref.dtype)


def _attn_prompt(q, k, v, bias_tabs, *, span):
    b, t, w = q.shape
    nslab = w // LANES
    cur = pl.BlockSpec((None, span, w), lambda i, j: (i, j, 0))
    prev = pl.BlockSpec((None, span, w), lambda i, j: (i, jnp.maximum(j - 1, 0), 0))
    slab = lambda rows: pltpu.VMEM((nslab, rows, LANES), F32)
    return pl.pallas_call(
        functools.partial(_attn_prompt_kernel, span=span),
        grid=(b, t // span),
        in_specs=[_const_spec(bias_tabs.shape, lambda i, j: (0, 0, 0)), cur, prev, cur, prev, cur],
        out_specs=cur,
        out_shape=jax.ShapeDtypeStruct((b, t, w), BF16),
        scratch_shapes=[slab(span), slab(2 * span), slab(2 * span), slab(span), slab(span)],
        compiler_params=_params(("parallel", "arbitrary")),
        name="attn_prompt",
    )(bias_tabs, q, k, k, v, v)


def _sample_bias_tables(n_past, t_new, t_pad):
    slopes = _alibi_slopes()
    n_keys = n_past + t_pad
    tabs = np.full((len(DILATIONS), ATT_HEADS * t_pad, n_keys), NEG_BIG, np.float32)
    for br, dil in enumerate(DILATIONS):
        for h, s in enumerate(slopes):
            for t in range(t_pad):
                tq = min(t, t_new - 1)
                for jj in range(ATT_BLK + 1):
                    idx = n_past + tq - jj * dil
                    if idx < 0:
                        continue
                    tabs[br, h * t_pad + t, idx] = -np.float32(s) * np.float32(jj * dil)
    return tabs[:, :, :n_past], tabs[:, :, n_past:]


def _attn_sample_kernel(bc_ref, bn_ref, q_ref, kn_ref, vn_ref, kc_ref, vc_ref, o_ref):
    for b in range(q_ref.shape[0]):
        _attn_sample_one(bc_ref, bn_ref, q_ref.at[b], kn_ref.at[b], vn_ref.at[b], kc_ref.at[b], vc_ref.at[b],
                         o_ref.at[b])


def _attn_sample_one(bc_ref, bn_ref, q_ref, kn_ref, vn_ref, kc_ref, vc_ref, o_ref):
    t_pad, w = q_ref.shape
    scale = ATT_HDIM ** -0.5
    q = q_ref[...] * scale
    lane = lax.broadcasted_iota(jnp.int32, (t_pad, w), 1)
    head_of_lane = lane // ATT_HDIM
    zero = jnp.zeros_like(q)
    lhs = jnp.concatenate([jnp.where(head_of_lane == h, q, zero) for h in range(ATT_HEADS)], axis=0).astype(BF16)
    s_c = _dot(lhs, kc_ref[...].astype(BF16))
    s_n = _dot_nt(lhs, kn_ref[...].astype(BF16))
    vc = vc_ref[...].astype(BF16)
    vn = vn_ref[...].astype(BF16)
    pcs, pns, ls, lses = [], [], [], []
    for br in range(len(DILATIONS)):
        bc = bc_ref[br]
        bn = bn_ref[br]
        sc = jnp.where(bc > 0.5 * NEG_BIG, s_c + bc, NEG_BIG)
        sn = jnp.where(bn > 0.5 * NEG_BIG, s_n + bn, NEG_BIG)
        m = jnp.maximum(jnp.max(sc, axis=-1, keepdims=True), jnp.max(sn, axis=-1, keepdims=True))
        pc = jnp.exp(sc - m)
        pn = jnp.exp(sn - m)
        l = jnp.sum(pc, axis=-1, keepdims=True) + jnp.sum(pn, axis=-1, keepdims=True)
        pcs.append(pc.astype(BF16))
        pns.append(pn.astype(BF16))
        ls.append(l)
        lses.append(m + jnp.log(l))
    nq = ATT_HEADS * t_pad
    pv = _dot_nt(jnp.concatenate(pcs, axis=0), vc) + _dot(jnp.concatenate(pns, axis=0), vn)
    outs = [pv[br * nq:(br + 1) * nq] / ls[br] for br in range(len(DILATIONS))]
    mx = functools.reduce(jnp.maximum, lses)
    es = [jnp.exp(x - mx) for x in lses]
    den = functools.reduce(lambda a, b: a + b, es)
    o = functools.reduce(lambda a, b: a + b, [e * x for e, x in zip(es, outs)]) / den
    res = zero
    for h in range(ATT_HEADS):
        res = jnp.where(head_of_lane == h, o[h * t_pad:(h + 1) * t_pad, :], res)
    o_ref[...] = res.astype(o_ref.dtype)


def _attn_sample(q, k, v, cache_k, cache_v, layer, bias_c, bias_n, *, nb):
    b, t_pad, w = q.shape
    n_past = cache_k.shape[3]
    new = pl.BlockSpec((nb, t_pad, w), lambda i: (i, 0, 0))
    cache = pl.BlockSpec((None, nb, w, n_past), lambda i: (layer, i, 0, 0))
    return pl.pallas_call(
        _attn_sample_kernel,
        grid=(b // nb,),
        in_specs=[_const_spec(bias_c.shape, lambda i: (0, 0, 0)),
                  _const_spec(bias_n.shape, lambda i: (0, 0, 0)),
                  new, new, new, cache, cache],
        out_specs=new,
        out_shape=jax.ShapeDtypeStruct((b, t_pad, w), F32),
        compiler_params=_params(("parallel",)),
        name="attn_sample",
    )(bias_c, bias_n, q, k, v, cache_k, cache_v)


def _hgrn_tables(chunk):
    sub = min(chunk, LANES)
    r = np.arange(sub)[:, None]
    c = np.arange(sub)[None, :]
    x = np.bitwise_xor(r, c)
    hb = np.zeros_like(x)
    for bit in range(1, 16):
        hb = np.where(x >> bit > 0, bit, hb)
    lev = np.where(c > r, -1, np.where(r // SUBLANES == c // SUBLANES, 0, hb)).astype(np.int32)
    lev = np.tile(lev, (chunk // sub, 1))
    sel = np.zeros((SUBLANES * LANES, sub), np.float32)
    for s in range(SUBLANES):
        sel[s * LANES:(s + 1) * LANES, s::SUBLANES] = 1.0
    rr = np.arange(chunk)
    tri = (rr[None, :] <= rr[:, None]).astype(np.float32)
    return lev, sel, tri


def _split3(x):
    hi = x.astype(BF16)
    r1 = x - hi.astype(F32)
    mid = r1.astype(BF16)
    lo = (r1 - mid.astype(F32)).astype(BF16)
    return hi, mid, lo


def _group_row(x, s):
    c, l = x.shape
    x3 = x.reshape(c // SUBLANES, SUBLANES, l)
    return jnp.broadcast_to(x3[:, s:s + 1, :], x3.shape).reshape(c, l)


def _block_last(x, m):
    c, l = x.shape
    x3 = x.reshape(c // m, m, l)
    return jnp.broadcast_to(x3[:, m - 1:m, :], x3.shape).reshape(c, l)


def _hgrn_lower_bound(raw, layer):
    e = jnp.exp(raw - jnp.max(raw, axis=0, keepdims=True))
    sm = e / jnp.sum(e, axis=0, keepdims=True)
    lb = jnp.zeros_like(sm[0:1])
    for i in range(1, layer + 1):
        lb = lb + sm[i:i + 1]
    return lb


def _hgrn_chunk_gates(zq, zf, zi, lb, tri, live):
    dk = zq.shape[1]
    q = _silu(zq)
    f = lb + (1.0 - lb) * jax.nn.sigmoid(zf)
    g = jnp.log2(f)
    k = 1.0 - f
    if live is not None:
        g = jnp.where(live, g, 0.0)
        k = jnp.where(live, k, 0.0)
    b3 = _dot(tri, jnp.concatenate(_split3(g), axis=1))
    b = b3[:, :dk] + b3[:, dk:2 * dk] + b3[:, 2 * dk:]
    return q, k, b, zi.astype(BF16)


def _hgrn_chunk_local(q, k, b, vb, lev, sel):
    chunk, dk = q.shape
    sub = lev.shape[1]

    vals = []
    for s in range(SUBLANES):
        decay = jnp.exp2(jnp.minimum(b - _group_row(b, s), 0.0))
        vals.append((q * decay * _group_row(k, s)).astype(BF16))
    a = jnp.where(lev == 0, _dot(jnp.concatenate(vals, axis=1), sel), 0.0)
    subs = [slice(r0, r0 + sub) for r0 in range(0, chunk, sub)]
    for bit in range(3, int(math.log2(sub))):
        m = 1 << bit
        b_end = _block_last(b, m)
        b_prev = jnp.concatenate([jnp.zeros((m, dk), F32), b_end[:chunk - m]], axis=0)
        qd = (q * jnp.exp2(b - b_prev)).astype(BF16)
        kd = (k * jnp.exp2(b_end - b)).astype(BF16)
        a_l = jnp.concatenate([_dot_nt(qd[rs], kd[rs]) for rs in subs], axis=0)
        a = jnp.where(lev == bit, a_l, a)
    ab = a.astype(BF16)
    o_parts = []
    for i, rs in enumerate(subs):
        if i == 0:
            o_parts.append(_dot(ab[rs], vb[rs]))
            continue
        b_piv = b[rs.start - 1:rs.start, :]
        qd = (q[rs] * jnp.exp2(b[rs] - b_piv)).astype(BF16)
        kd = (k[:rs.start] * jnp.exp2(b_piv - b[:rs.start])).astype(BF16)
        a_row = jnp.concatenate([_dot_nt(qd, kd).astype(BF16), ab[rs]], axis=1)
        o_parts.append(_dot(a_row, vb[:rs.stop]))
    o = jnp.concatenate(o_parts, axis=0) if len(o_parts) > 1 else o_parts[0]
    q_in = (q * jnp.exp2(b)).astype(BF16)
    b_last = b[chunk - 1:chunk, :]
    kd = (k * jnp.exp2(b_last - b)).astype(BF16)
    return o, q_in, _dot_tn(vb, kd), jnp.exp2(b_last)


def _hgrn_chunk_finish(o_local, q_in, st, ng, zg):
    o = o_local + _dot_nt(q_in, st.astype(BF16))
    on = o * lax.rsqrt(jnp.mean(o * o, axis=-1, keepdims=True) + EPS) * ng
    return on * _silu(zg)


def _hgrn_sample_kernel(zh_ref, lb_ref, ng_ref, lev_ref, sel_ref, tri_ref, s0_ref, o_ref, sf_ref,
                        *, nb, layer, t_valid):
    t_pad = zh_ref.shape[1]
    hd = o_ref.shape[2]
    dk = hd // HG_HEADS
    lev = lev_ref[...]
    sel = sel_ref[...]
    tri = tri_ref[...]
    ng = ng_ref[...]
    live = lax.broadcasted_iota(jnp.int32, (t_pad, dk), 0) < t_valid
    lbs = [_hgrn_lower_bound(lb_ref[:, h * dk:(h + 1) * dk], layer) for h in range(HG_HEADS)]

    def one_batch(bi, carry):
        part = lambda h, p: zh_ref[bi, :, (p * HG_HEADS + h) * dk:(p * HG_HEADS + h + 1) * dk]
        heads = range(HG_HEADS)
        gates = [_hgrn_chunk_gates(part(h, 0), part(h, 1), part(h, 2), lbs[h], tri, live) for h in heads]
        states = [s0_ref[bi, h].T for h in heads]
        local = [_hgrn_chunk_local(*gates[h], lev, sel) for h in heads]
        for h in heads:
            o_local, q_in, s_own, d_all = local[h]
            o_ref[bi, :, h * dk:(h + 1) * dk] = _hgrn_chunk_finish(o_local, q_in, states[h], ng, part(h, 3))
            sf_ref[bi, h] = (states[h] * d_all + s_own).T
        return carry

    lax.fori_loop(0, nb, one_batch, 0, unroll=2)


def _hgrn_sample(zh, lb_raw, norm_g, s0, layer, tables, *, nb, t_valid):
    b, t_pad, w4 = zh.shape
    hd = w4 // 4
    dk = hd // HG_HEADS
    depth = lb_raw.shape[0]
    lev, sel, tri = tables
    return pl.pallas_call(
        functools.partial(_hgrn_sample_kernel, nb=nb, layer=layer, t_valid=t_valid),
        grid=(b // nb,),
        in_specs=[pl.BlockSpec((nb, t_pad, w4), lambda i: (i, 0, 0)),
                  pl.BlockSpec((depth, hd), lambda i: (0, 0)),
                  pl.BlockSpec((None, 1, dk), lambda i: (layer, 0, 0)),
                  _const_spec(lev.shape, lambda i: (0, 0)),
                  _const_spec(sel.shape, lambda i: (0, 0)),
                  _const_spec(tri.shape, lambda i: (0, 0)),
                  pl.BlockSpec((None, nb, HG_HEADS, dk, dk), lambda i: (layer, i, 0, 0, 0))],
        out_specs=[pl.BlockSpec((nb, t_pad, hd), lambda i: (i, 0, 0)),
                   pl.BlockSpec((nb, HG_HEADS, dk, dk), lambda i: (i, 0, 0, 0))],
        out_shape=[jax.ShapeDtypeStruct((b, t_pad, hd), F32),
                   jax.ShapeDtypeStruct((b, HG_HEADS, dk, dk), F32)],
        compiler_params=_params(("parallel",)),
        name="hgrn_sample",
    )(zh, lb_raw, norm_g, lev, sel, tri, s0)


def _proj_mix_kernel(x_ref, ln_ref, wi_ref, dww_ref, dwb_ref, lng_ref, lnb_ref, lb_ref, ng_ref,
                     lev_ref, sel_ref, tri_ref,
                     yc_ref, q_ref, k_ref, v_ref, yh_ref, tail_ref, sf_ref, kt_ref, vt_ref,
                     ext_ref, st_ref, *, layer, chunk):
    tstep = pl.program_id(1)
    tt = x_ref.shape[0]
    conv_c = yc_ref.shape[1]
    att = q_ref.shape[1]
    hd = yh_ref.shape[1]
    dk = hd // HG_HEADS
    n_chunks = tt // chunk

    @pl.when(tstep == 0)
    def _():
        ext_ref[0, 0:CONV_HALO, :] = jnp.zeros((CONV_HALO, conv_c), F32)
        ext_ref[0, CONV_HALO + tt:CONV_HALO + tt + SUBLANES, :] = jnp.zeros((SUBLANES, conv_c), F32)
        st_ref[...] = jnp.zeros_like(st_ref)

    h = _rms(x_ref[...], ln_ref[...]).astype(BF16)
    c0 = 2 * conv_c
    zc = _dot(h, wi_ref[:, 0:c0])
    zh = _dot(h, wi_ref[:, c0 + 3 * att:c0 + 3 * att + 4 * hd])
    yc, u = _conv_tile(zc, ext_ref, 0, dww_ref, dwb_ref[...], lng_ref[...], lnb_ref[...])
    yc_ref[...] = yc.astype(yc_ref.dtype)
    tail_ref[...] = u[tt - CONV_HALO:, :]
    for i, (ref, t_ref) in enumerate(((q_ref, None), (k_ref, kt_ref), (v_ref, vt_ref))):
        z = _dot(h, wi_ref[:, c0 + i * att:c0 + (i + 1) * att])
        ref[...] = z
        if t_ref is not None:
            t_ref[...] = z.T

    lev = lev_ref[...]
    sel = sel_ref[...]
    tri = tri_ref[...]
    ng = ng_ref[...]
    lbs = [_hgrn_lower_bound(lb_ref[:, hh * dk:(hh + 1) * dk], layer) for hh in range(HG_HEADS)]

    def part(u_idx, p):
        hh, c = divmod(u_idx, n_chunks)
        return zh[c * chunk:(c + 1) * chunk, (p * HG_HEADS + hh) * dk:(p * HG_HEADS + hh + 1) * dk]

    def gates(u_idx):
        return _hgrn_chunk_gates(part(u_idx, 0), part(u_idx, 1), part(u_idx, 2), lbs[u_idx // n_chunks], tri, None)

    ahead = gates(0)
    st = None
    for u_idx in range(HG_HEADS * n_chunks):
        hh, c = divmod(u_idx, n_chunks)
        cur = ahead
        if u_idx + 1 < HG_HEADS * n_chunks:
            ahead = gates(u_idx + 1)
        if c == 0:
            st = st_ref[hh]
        o_local, q_in, s_own, d_all = _hgrn_chunk_local(*cur, lev, sel)
        o = _hgrn_chunk_finish(o_local, q_in, st, ng, part(u_idx, 3))
        yh_ref[c * chunk:(c + 1) * chunk, hh * dk:(hh + 1) * dk] = o.astype(yh_ref.dtype)
        st = st * d_all + s_own
        if c == n_chunks - 1:
            st_ref[hh] = st

    @pl.when(tstep == pl.num_programs(1) - 1)
    def _():
        for hh in range(HG_HEADS):
            sf_ref[hh] = st_ref[hh].T


def _proj_mix(x, layer, ln, wi, dww, dwb, lng, lnb, lb_raw, norm_g, tables, *, batch, tm, chunk, widths, keep):
    n, d = x.shape
    conv_c = widths[0] // 2
    att = widths[1]
    hd = widths[4] // 4
    dk = hd // HG_HEADS
    depth = lb_raw.shape[0]
    nt = n // batch // tm
    lev, sel, tri = tables
    row = lambda w: pl.BlockSpec((tm, w), lambda b, j: (b * nt + j, 0))
    cvec = pl.BlockSpec((None, 1, conv_c), lambda b, j: (layer, 0, 0))
    const2 = lambda a: _const_spec(a.shape, lambda b, j: (0, 0))
    first_kept = nt - keep // tm
    window = pl.BlockSpec((None, att, tm), lambda b, j: (b, 0, jnp.maximum(j - first_kept, 0)))
    return pl.pallas_call(
        functools.partial(_proj_mix_kernel, layer=layer, chunk=chunk),
        grid=(batch, nt),
        in_specs=[row(d), pl.BlockSpec((None, 1, d), lambda b, j: (layer, 0, 0)),
                  _const_spec(wi.shape, lambda b, j: (0, 0)),
                  pl.BlockSpec((None, CONV_HALO, conv_c), lambda b, j: (layer, 0, 0)),
                  cvec, cvec, cvec,
                  pl.BlockSpec((depth, hd), lambda b, j: (0, 0)),
                  pl.BlockSpec((None, 1, dk), lambda b, j: (layer, 0, 0)),
                  const2(lev), const2(sel), const2(tri)],
        out_specs=[row(conv_c), row(att), row(att), row(att), row(hd),
                   pl.BlockSpec((None, CONV_HALO, conv_c), lambda b, j: (b, 0, 0)),
                   pl.BlockSpec((None, HG_HEADS, dk, dk), lambda b, j: (b, 0, 0, 0)),
                   window, window],
        out_shape=[jax.ShapeDtypeStruct((n, conv_c), BF16),
                   jax.ShapeDtypeStruct((n, att), F32),
                   jax.ShapeDtypeStruct((n, att), F32),
                   jax.ShapeDtypeStruct((n, att), F32),
                   jax.ShapeDtypeStruct((n, hd), BF16),
                   jax.ShapeDtypeStruct((batch, CONV_HALO, conv_c), F32),
                   jax.ShapeDtypeStruct((batch, HG_HEADS, dk, dk), F32),
                   jax.ShapeDtypeStruct((batch, att, keep), F32),
                   jax.ShapeDtypeStruct((batch, att, keep), F32)],
        scratch_shapes=[pltpu.VMEM((1, CONV_HALO + tm + SUBLANES, conv_c), F32),
                        pltpu.VMEM((HG_HEADS, dk, dk), F32)],
        compiler_params=_params(("parallel", "arbitrary")),
        name="proj_mix",
    )(x, ln, wi, dww, dwb, lng, lnb, lb_raw, norm_g, lev, sel, tri)


def _pad_rows(x, rows):
    return jnp.pad(x, ((0, 0), (0, rows - x.shape[1]), (0, 0)))


def _key_minor(cache):
    depth, b, n_past, h, hd = cache.shape
    return jnp.transpose(cache, (0, 1, 3, 4, 2)).reshape(depth, b, h * hd, n_past)


def kernel(x_prompt, x_sample, state_conv, cache_k_win, cache_v_win, state_hgrn, ln_ffn1, w_ffn1_gate, w_ffn1_up, w_ffn1_down, ln_mix, w_in, conv_dw_w, conv_dw_b, conv_ln_g, conv_ln_b, hg_lower_bounds, hg_norm_g, w_out, ln_ffn2, w_ffn2_gate, w_ffn2_up, w_ffn2_down, ln_final):
    bp, seq, d = x_prompt.shape
    bs, t_new, _ = x_sample.shape
    depth = w_in.shape[0]
    conv_dim = conv_dw_w.shape[2]
    att_dim = ATT_HEADS * ATT_HDIM
    hg_dim = hg_lower_bounds.shape[1]
    widths = (2 * conv_dim, att_dim, att_dim, att_dim, 4 * hg_dim)
    n_past = cache_k_win.shape[2]
    t_pad = SUBLANES
    keep = min(ATT_SPAN, seq)

    early_f32 = (w_ffn1_gate, w_ffn1_up, w_ffn1_down, w_in, w_out)
    ffn2_f32 = (w_ffn2_gate, w_ffn2_up, w_ffn2_down)
    early_w = tuple(w[0].astype(BF16) for w in early_f32)
    ffn2_w = tuple(w[0].astype(BF16) for w in ffn2_f32)
    vec3 = lambda a: a.reshape(depth, 1, a.shape[-1])
    ln1, lnm, ln2 = vec3(ln_ffn1), vec3(ln_mix), vec3(ln_ffn2)
    dwb, clg, clb, hgn = vec3(conv_dw_b), vec3(conv_ln_g), vec3(conv_ln_b), vec3(hg_norm_g)
    lnf = ln_final.reshape(1, d)
    dww = jnp.pad(conv_dw_w, ((0, 0), (0, CONV_HALO - CONV_WIDTH), (0, 0)))
    cache_k, cache_v = _key_minor(cache_k_win), _key_minor(cache_v_win)
    halo_s = jnp.pad(state_conv, ((0, 0), (0, 0), (CONV_HALO - (CONV_WIDTH - 1), 0), (0, 0)))

    bias_p = jnp.asarray(_prompt_bias_tables())
    bias_c, bias_n = (jnp.asarray(a) for a in _sample_bias_tables(n_past, t_new, t_pad))
    chunk_p = 128
    tabs_p = tuple(jnp.asarray(a, dt) for a, dt in zip(_hgrn_tables(chunk_p), (jnp.int32, BF16, BF16)))
    tabs_s = tuple(jnp.asarray(a, dt) for a, dt in zip(_hgrn_tables(t_pad), (jnp.int32, BF16, BF16)))

    hp = x_prompt.reshape(bp * seq, d)
    hs = x_sample.reshape(bs * t_new, d)
    outs = [[] for _ in range(8)]
    for l in range(depth):
        final = l == depth - 1
        nxt1 = early_f32 if not final else ()
        nxt2 = ffn2_f32 if not final else ()
        wg1, wu1, wd1, wi, wo = early_w
        x1, x1s, early_next = _ffn_half_step(hp, hs, l, ln1, wg1, wu1, wd1, nxt1, l + 1, tm=512, f_chunk=1408)
        yc, q, k, v, yh, tail, sp, kt, vt = _proj_mix(x1, l, lnm, wi, dww, dwb, clg, clb, hg_lower_bounds, hgn,
                                                      tabs_p, batch=bp, tm=512, chunk=chunk_p, widths=widths,
                                                      keep=keep)
        seq3 = lambda a: a.reshape(bp, seq, a.shape[-1])
        ya = _attn_prompt(seq3(q), seq3(k), seq3(v), bias_p, span=ATT_SPAN)
        window = lambda a: a.reshape(bp, ATT_HEADS, ATT_HDIM, keep).transpose(0, 3, 1, 2)
        outs[0].append(tail[:, -(CONV_WIDTH - 1):])
        outs[2].append(window(kt))
        outs[3].append(window(vt))
        outs[6].append(sp)

        zcs, qs, ks, vs, zhs = _proj_in(x1s, l, lnm, wi, tm=bs * t_new, widths=widths)
        new3 = lambda a: _pad_rows(a.reshape(bs, t_new, a.shape[-1]), t_pad)
        ycs, us = _conv_group(new3(zcs), halo_s[l], l, dww, dwb, clg, clb, nb=bs, tt=t_pad, out_dtype=F32)
        yas = _attn_sample(new3(qs), new3(ks), new3(vs), cache_k, cache_v, l, bias_c, bias_n, nb=2)
        yhs, ss = _hgrn_sample(new3(zhs), hg_lower_bounds, hgn, state_hgrn, l, tabs_s, nb=8, t_valid=t_new)
        unpad = lambda a: a[:, :t_new].reshape(bs * t_new, a.shape[-1])
        hp, hs, ffn2_next = _dense_out((x1, yc, ya.reshape(bp * seq, att_dim), yh),
                                       (x1s, unpad(ycs), unpad(yas), unpad(yhs)),
                                       l, wo, ln2, *ffn2_w, lnf, nxt2, l + 1, tm=512, f_chunk=1408, final=final)
        early_w, ffn2_w = early_next, ffn2_next
        outs[1].append(jnp.concatenate([state_conv[l][:, t_new:], us[:, :t_new]], axis=1))
        outs[4].append(ks.reshape(bs, t_new, ATT_HEADS, ATT_HDIM))
        outs[5].append(vs.reshape(bs, t_new, ATT_HEADS, ATT_HDIM))
        outs[7].append(ss)

    y_prompt = hp.reshape(bp, seq, d)
    y_sample = hs.reshape(bs, t_new, d)
    st = [jnp.stack(o) for o in outs]
    return (y_prompt, y_sample, st[0], st[1], st[2], st[3], st[4], st[5], st[6], st[7])
```

```python
import functools
import math

import numpy as np
import jax
import jax.numpy as jnp
from jax import lax
from jax.experimental import pallas as pl
from jax.experimental.pallas import tpu as pltpu

F32 = jnp.float32
BF16 = jnp.bfloat16

EPS = 1e-6
NEG_BIG = -1e30
CONV_WIDTH = 31
CONV_HALO = 32
ATT_HEADS = 4
ATT_HDIM = 64
ATT_BLK = 128
DILATIONS = (1, 4, 16)
ATT_SPAN = 2048
ATT_UNROLL = 16
HG_HEADS = 4
LANES = 128
SUBLANES = 8
BF16_ROWS = 16
VMEM_LIMIT = 56 * 1024 * 1024


def _params(sem, vmem=VMEM_LIMIT):
    return pltpu.CompilerParams(dimension_semantics=sem, vmem_limit_bytes=vmem)


def _const_spec(shape, index):
    return pl.BlockSpec(shape, index, pipeline_mode=pl.Buffered(1))


def _rms(x, g):
    ms = jnp.mean(x * x, axis=-1, keepdims=True)
    return x * lax.rsqrt(ms + EPS) * g


def _silu(x):
    return x * jax.nn.sigmoid(x)


def _dot(a, b):
    return jnp.dot(a, b, preferred_element_type=F32)


def _dot_nt(a, b):
    return lax.dot_general(a, b, (((1,), (1,)), ((), ())), preferred_element_type=F32)


def _dot_tn(a, b):
    return lax.dot_general(a, b, (((0,), (0,)), ((), ())), preferred_element_type=F32)


def _swiglu(h, wg_ref, wu_ref, wd_ref, f_chunk):
    ffn = wg_ref.shape[1]
    acc = None
    for c0 in range(0, ffn, f_chunk):
        g = _dot(h, wg_ref[:, c0:c0 + f_chunk])
        u = _dot(h, wu_ref[:, c0:c0 + f_chunk])
        a = (_silu(g) * u).astype(BF16)
        y = _dot(a, wd_ref[c0:c0 + f_chunk, :])
        acc = y if acc is None else acc + y
    return acc


def _main_then_extra(n_main, main, extra):
    i = pl.program_id(0)
    pl.when(i < n_main)(main)
    pl.when(i == n_main)(extra)


def _cast_specs(stacked, layer, n_steps):
    in_specs, out_specs, shapes = [], [], []
    for w in stacked:
        _, r, c = w.shape
        rows = -(-r // n_steps)
        rows = -(-rows // BF16_ROWS) * BF16_ROWS
        while r % rows:
            rows += BF16_ROWS
        last = r // rows - 1
        in_specs.append(pl.BlockSpec((None, rows, c), lambda i, last=last: (layer, jnp.minimum(i, last), 0)))
        out_specs.append(pl.BlockSpec((rows, c), lambda i, last=last: (jnp.minimum(i, last), 0)))
        shapes.append(jax.ShapeDtypeStruct((r, c), BF16))
    return in_specs, out_specs, shapes


def _cast_slabs(src_refs, dst_refs):
    for src, dst in zip(src_refs, dst_refs):
        dst[...] = src[...].astype(dst.dtype)


def _ffn_kernel(*refs, f_chunk, n_main, n_cast):
    xp_ref, xs_ref, ln_ref, wg_ref, wu_ref, wd_ref = refs[:6]
    cast_src = refs[6:6 + n_cast]
    op_ref, os_ref = refs[6 + n_cast:8 + n_cast]
    cast_dst = refs[8 + n_cast:]

    def run(x_ref, o_ref):
        x = x_ref[...]
        h = _rms(x, ln_ref[...]).astype(BF16)
        o_ref[...] = x + 0.5 * _swiglu(h, wg_ref, wu_ref, wd_ref, f_chunk)

    _main_then_extra(n_main, lambda: run(xp_ref, op_ref), lambda: run(xs_ref, os_ref))
    _cast_slabs(cast_src, cast_dst)


def _ffn_half_step(xp, xs, layer, ln, wg, wu, wd, cast, cast_layer, *, tm, f_chunk):
    n, d = xp.shape
    ns = xs.shape[0]
    ffn = wg.shape[1]
    n_main = n // tm
    row = pl.BlockSpec((tm, d), lambda i: (jnp.minimum(i, n_main - 1), 0))
    extra = pl.BlockSpec((ns, d), lambda i: (0, 0))
    c_in, c_out, c_shapes = _cast_specs(cast, cast_layer, n_main + 1)
    res = pl.pallas_call(
        functools.partial(_ffn_kernel, f_chunk=f_chunk, n_main=n_main, n_cast=len(cast)),
        grid=(n_main + 1,),
        in_specs=[row, extra, pl.BlockSpec((None, 1, d), lambda i: (layer, 0, 0)),
                  _const_spec((d, ffn), lambda i: (0, 0)),
                  _const_spec((d, ffn), lambda i: (0, 0)),
                  _const_spec((ffn, d), lambda i: (0, 0))] + c_in,
        out_specs=[row, extra] + c_out,
        out_shape=[jax.ShapeDtypeStruct((n, d), F32), jax.ShapeDtypeStruct((ns, d), F32)] + c_shapes,
        compiler_params=_params(("arbitrary",)),
        name="ffn_half_step",
    )(xp, xs, ln, wg, wu, wd, *cast)
    return res[0], res[1], tuple(res[2:])


def _proj_in_kernel(x_ref, ln_ref, wi_ref, zc_ref, q_ref, k_ref, v_ref, zh_ref):
    h = _rms(x_ref[...], ln_ref[...]).astype(BF16)
    c0 = 0
    for ref in (zc_ref, q_ref, k_ref, v_ref, zh_ref):
        w = ref.shape[1]
        ref[...] = _dot(h, wi_ref[:, c0:c0 + w])
        c0 += w


def _proj_in(x, layer, ln, wi, *, tm, widths):
    n, d = x.shape
    row = lambda w: pl.BlockSpec((tm, w), lambda i: (i, 0))
    return pl.pallas_call(
        _proj_in_kernel,
        grid=(n // tm,),
        in_specs=[row(d), pl.BlockSpec((None, 1, d), lambda i: (layer, 0, 0)),
                  _const_spec(wi.shape, lambda i: (0, 0))],
        out_specs=[row(w) for w in widths],
        out_shape=[jax.ShapeDtypeStruct((n, w), F32) for w in widths],
        compiler_params=_params(("parallel",)),
        name="proj_in",
    )(x, ln, wi)


def _dense_out_kernel(*refs, f_chunk, final, n_main, n_cast):
    (xp_ref, ycp_ref, yap_ref, yhp_ref, xs_ref, ycs_ref, yas_ref, yhs_ref,
     wo_ref, ln2_ref, wg_ref, wu_ref, wd_ref, lnf_ref) = refs[:14]
    cast_src = refs[14:14 + n_cast]
    op_ref, os_ref = refs[14 + n_cast:16 + n_cast]
    cast_dst = refs[16 + n_cast:]

    def run(x_ref, yc_ref, ya_ref, yh_ref, o_ref):
        y = jnp.concatenate([r[...].astype(BF16) for r in (yc_ref, ya_ref, yh_ref)], axis=-1)
        x2 = x_ref[...] + _dot(y, wo_ref[...])
        h = _rms(x2, ln2_ref[...]).astype(BF16)
        x3 = x2 + 0.5 * _swiglu(h, wg_ref, wu_ref, wd_ref, f_chunk)
        if final:
            x3 = _rms(x3, lnf_ref[...])
        o_ref[...] = x3

    _main_then_extra(n_main, lambda: run(xp_ref, ycp_ref, yap_ref, yhp_ref, op_ref),
                     lambda: run(xs_ref, ycs_ref, yas_ref, yhs_ref, os_ref))
    _cast_slabs(cast_src, cast_dst)


def _dense_out(prompt, sample, layer, wo, ln2, wg, wu, wd, lnf, cast, cast_layer, *, tm, f_chunk, final):
    n, d = prompt[0].shape
    ns = sample[0].shape[0]
    ffn = wg.shape[1]
    n_main = n // tm
    row = lambda a: pl.BlockSpec((tm, a.shape[1]), lambda i: (jnp.minimum(i, n_main - 1), 0))
    extra = lambda a: pl.BlockSpec((ns, a.shape[1]), lambda i: (0, 0))
    vec = pl.BlockSpec((None, 1, d), lambda i: (layer, 0, 0))
    c_in, c_out, c_shapes = _cast_specs(cast, cast_layer, n_main + 1)
    res = pl.pallas_call(
        functools.partial(_dense_out_kernel, f_chunk=f_chunk, final=final, n_main=n_main, n_cast=len(cast)),
        grid=(n_main + 1,),
        in_specs=[row(a) for a in prompt] + [extra(a) for a in sample] + [
                  _const_spec((d, d), lambda i: (0, 0)),
                  vec,
                  _const_spec((d, ffn), lambda i: (0, 0)),
                  _const_spec((d, ffn), lambda i: (0, 0)),
                  _const_spec((ffn, d), lambda i: (0, 0)),
                  pl.BlockSpec((1, d), lambda i: (0, 0))] + c_in,
        out_specs=[row(prompt[0]), extra(sample[0])] + c_out,
        out_shape=[jax.ShapeDtypeStruct((n, d), F32), jax.ShapeDtypeStruct((ns, d), F32)] + c_shapes,
        compiler_params=_params(("arbitrary",)),
        name="dense_out",
    )(*prompt, *sample, wo, ln2, wg, wu, wd, lnf, *cast)
    return res[0], res[1], tuple(res[2:])


def _conv_tile(z, ext_ref, b, dww_ref, dwb, lng, lnb):
    tt, c2 = z.shape
    c = c2 // 2
    lo = CONV_HALO - (CONV_WIDTH - 1)
    u = z[:, :c] * jax.nn.sigmoid(z[:, c:])
    ext_ref[b, CONV_HALO:CONV_HALO + tt, :] = u
    y = None
    for r in range(SUBLANES):
        z_r = None
        for o in range(r, CONV_HALO + 1, SUBLANES):
            if o < lo:
                continue
            term = ext_ref[b, o - r:o - r + tt + SUBLANES, :] * dww_ref[o - lo:o - lo + 1, :]
            z_r = term if z_r is None else z_r + term
        part = z_r[r:r + tt, :]
        y = part if y is None else y + part
    y = y + dwb
    yc = y - jnp.mean(y, axis=-1, keepdims=True)
    yn = yc * lax.rsqrt(jnp.mean(yc * yc, axis=-1, keepdims=True) + EPS)
    carry = ext_ref[b, tt:tt + CONV_HALO, :]
    ext_ref[b, 0:CONV_HALO, :] = carry
    return _silu(yn * lng + lnb), u


def _conv_kernel(zc_ref, halo_ref, dww_ref, dwb_ref, lng_ref, lnb_ref, y_ref, tail_ref, ext_ref, *, nb, tt, tail):
    c = y_ref.shape[-1]

    def one_batch(b):
        @pl.when(pl.program_id(1) == 0)
        def _():
            ext_ref[b, 0:CONV_HALO, :] = halo_ref[b]
            ext_ref[b, CONV_HALO + tt:CONV_HALO + tt + SUBLANES, :] = jnp.zeros((SUBLANES, c), F32)

        y, u = _conv_tile(zc_ref[b], ext_ref, b, dww_ref, dwb_ref[...], lng_ref[...], lnb_ref[...])
        y_ref[b] = y.astype(y_ref.dtype)
        tail_ref[b] = u[tt - tail:, :]

    if nb == 1:
        one_batch(0)
    else:
        def body(b, carry):
            one_batch(b)
            return carry
        lax.fori_loop(0, nb, body, 0)


def _conv_group(zc, halo, layer, dww, dwb, lng, lnb, *, nb, tt, out_dtype):
    b, t, c2 = zc.shape
    c = c2 // 2
    tail = min(CONV_HALO, tt)
    vec = pl.BlockSpec((None, 1, c), lambda i, j: (layer, 0, 0))
    return pl.pallas_call(
        functools.partial(_conv_kernel, nb=nb, tt=tt, tail=tail),
        grid=(b // nb, t // tt),
        in_specs=[pl.BlockSpec((nb, tt, c2), lambda i, j: (i, j, 0)),
                  pl.BlockSpec((nb, CONV_HALO, c), lambda i, j: (i, 0, 0)),
                  pl.BlockSpec((None, CONV_HALO, c), lambda i, j: (layer, 0, 0)),
                  vec, vec, vec],
        out_specs=[pl.BlockSpec((nb, tt, c), lambda i, j: (i, j, 0)),
                   pl.BlockSpec((nb, tail, c), lambda i, j: (i, 0, 0))],
        out_shape=[jax.ShapeDtypeStruct((b, t, c), out_dtype),
                   jax.ShapeDtypeStruct((b, tail, c), F32)],
        scratch_shapes=[pltpu.VMEM((nb, CONV_HALO + tt + SUBLANES, c), F32)],
        compiler_params=_params(("parallel", "arbitrary")),
        name="conv_group",
    )(zc, halo, dww, dwb, lng, lnb)


def _alibi_slopes():
    return [2.0 ** (-8.0 * (h + 1) / ATT_HEADS) for h in range(ATT_HEADS)]


def _prompt_bias_tables():
    qi = np.arange(ATT_BLK)[:, None]
    ki = np.arange(2 * ATT_BLK)[None, :]
    j = qi - ki + ATT_BLK
    tabs = []
    for dil in DILATIONS:
        for first in (False, True):
            valid = (j >= 0) & (j <= ATT_BLK) & ((not first) | (ki >= ATT_BLK))
            rows = []
            for s in _alibi_slopes():
                bias = (-np.float32(s)) * (j * dil).astype(np.float32)
                rows.append(np.where(valid, bias, np.float32(NEG_BIG)).astype(np.float32))
            tabs.append(np.concatenate(rows, axis=0))
    return np.stack(tabs, axis=0)


def _head_pair_split(x, lo_mask):
    zero = jnp.zeros_like(x)
    return jnp.concatenate([jnp.where(lo_mask, x, zero), jnp.where(lo_mask, zero, x)], axis=0)


def _attn_prompt_kernel(bias_ref, q_ref, kp_ref, kc_ref, vp_ref, vc_ref, o_ref,
                        qs_ref, ks_ref, vs_ref, os_ref, ls_ref, *, span):
    j = pl.program_id(1)
    nslab = qs_ref.shape[0]
    scale = ATT_HDIM ** -0.5
    for sl in range(nslab):
        cols = slice(sl * LANES, (sl + 1) * LANES)
        qs_ref[sl] = q_ref[:, cols] * scale
        ks_ref[sl, 0:span, :] = kp_ref[:, cols]
        ks_ref[sl, span:2 * span, :] = kc_ref[:, cols]
        vs_ref[sl, 0:span, :] = vp_ref[:, cols]
        vs_ref[sl, span:2 * span, :] = vc_ref[:, cols]

    lane = lax.broadcasted_iota(jnp.int32, (ATT_BLK, LANES), 1)
    lo_mask = lane < ATT_HDIM
    n_blocks = span // ATT_BLK

    order = sorted(range(len(DILATIONS)), key=lambda b: -DILATIONS[b])
    for pos, br in enumerate(order):
        dil = DILATIONS[br]

        def rows(start, size, dil=dil):
            if dil == 1:
                return pl.ds(start if isinstance(start, int) else pl.multiple_of(start, ATT_BLK), size)
            return pl.ds(start, size, stride=dil)

        def scores(i, dil=dil, rows=rows):
            q0 = (i // dil) * (ATT_BLK * dil) + i % dil
            k0 = span + q0 - ATT_BLK * dil
            s_parts, v_parts = [], []
            for sl in range(nslab):
                ql = qs_ref[sl, rows(q0, ATT_BLK), :]
                kl = ks_ref[sl, rows(k0, 2 * ATT_BLK), :]
                v_parts.append(vs_ref[sl, rows(k0, 2 * ATT_BLK), :].astype(BF16))
                lhs = _head_pair_split(ql, lo_mask).astype(BF16)
                s_parts.append(_dot_nt(lhs, kl.astype(BF16)))
            return jnp.concatenate(s_parts, axis=0), jnp.concatenate(v_parts, axis=1)

        def finish(i, s, v, br=br, dil=dil, pos=pos, rows=rows):
            q0 = (i // dil) * (ATT_BLK * dil) + i % dil
            first = jnp.logical_and(j == 0, i // dil == 0).astype(jnp.int32)
            bias = bias_ref[2 * br + first]
            s = jnp.where(bias > 0.5 * NEG_BIG, s + bias, NEG_BIG)
            m = jnp.max(s, axis=-1, keepdims=True)
            p = jnp.exp(s - m)
            l = jnp.sum(p, axis=-1, keepdims=True)
            pv = _dot(p.astype(BF16), v)
            lse_all = m + jnp.log(l)
            for sl in range(nslab):
                cols = slice(sl * LANES, (sl + 1) * LANES)
                ra = slice((2 * sl) * ATT_BLK, (2 * sl + 1) * ATT_BLK)
                rb = slice((2 * sl + 1) * ATT_BLK, (2 * sl + 2) * ATT_BLK)
                pair = lambda x: jnp.where(lo_mask, jnp.broadcast_to(x[ra], (ATT_BLK, LANES)),
                                           jnp.broadcast_to(x[rb], (ATT_BLK, LANES)))
                o_new = jnp.where(lo_mask, pv[ra, cols], pv[rb, cols]) / pair(l)
                lse_new = pair(lse_all)
                out_rows = rows(q0, ATT_BLK)
                if pos == 0:
                    o_tot, lse_tot = o_new, lse_new
                else:
                    o_old = os_ref[sl, out_rows, :]
                    lse_old = ls_ref[sl, out_rows, :]
                    mx = jnp.maximum(lse_old, lse_new)
                    e_old = jnp.exp(lse_old - mx)
                    e_new = jnp.exp(lse_new - mx)
                    den = e_old + e_new
                    o_tot = (e_old * o_old + e_new * o_new) / den
                    lse_tot = mx + jnp.log(den)
                os_ref[sl, out_rows, :] = o_tot
                ls_ref[sl, out_rows, :] = lse_tot

        def trip(t, carry, scores=scores, finish=finish):
            base = t * ATT_UNROLL
            ahead = scores(base)
            for u in range(ATT_UNROLL):
                s, v = ahead
                if u + 1 < ATT_UNROLL:
                    ahead = scores(base + u + 1)
                finish(base + u, s, v)
            return carry

        if n_blocks == ATT_UNROLL:
            trip(0, 0)
        else:
            lax.fori_loop(0, n_blocks // ATT_UNROLL, trip, 0)

    o_ref[...] = jnp.concatenate([os_ref[sl] for sl in range(nslab)], axis=-1).astype(o_ref.dtype)


def _attn_prompt(q, k, v, bias_tabs, *, span):
    b, t, w = q.shape
    nslab = w // LANES
    cur = pl.BlockSpec((None, span, w), lambda i, j: (i, j, 0))
    prev = pl.BlockSpec((None, span, w), lambda i, j: (i, jnp.maximum(j - 1, 0), 0))
    slab = lambda rows: pltpu.VMEM((nslab, rows, LANES), F32)
    return pl.pallas_call(
        functools.partial(_attn_prompt_kernel, span=span),
        grid=(b, t // span),
        in_specs=[_const_spec(bias_tabs.shape, lambda i, j: (0, 0, 0)), cur, prev, cur, prev, cur],
        out_specs=cur,
        out_shape=jax.ShapeDtypeStruct((b, t, w), BF16),
        scratch_shapes=[slab(span), slab(2 * span), slab(2 * span), slab(span), slab(span)],
        compiler_params=_params(("parallel", "arbitrary")),
        name="attn_prompt",
    )(bias_tabs, q, k, k, v, v)


def _sample_bias_tables(n_past, t_new, t_pad):
    slopes = _alibi_slopes()
    n_keys = n_past + t_pad
    tabs = np.full((len(DILATIONS), ATT_HEADS * t_pad, n_keys), NEG_BIG, np.float32)
    for br, dil in enumerate(DILATIONS):
        for h, s in enumerate(slopes):
            for t in range(t_pad):
                tq = min(t, t_new - 1)
                for jj in range(ATT_BLK + 1):
                    idx = n_past + tq - jj * dil
                    if idx < 0:
                        continue
                    tabs[br, h * t_pad + t, idx] = -np.float32(s) * np.float32(jj * dil)
    return tabs[:, :, :n_past], tabs[:, :, n_past:]


def _attn_sample_kernel(bc_ref, bn_ref, q_ref, kn_ref, vn_ref, kc_ref, vc_ref, o_ref):
    for b in range(q_ref.shape[0]):
        _attn_sample_one(bc_ref, bn_ref, q_ref.at[b], kn_ref.at[b], vn_ref.at[b], kc_ref.at[b], vc_ref.at[b],
                         o_ref.at[b])


def _attn_sample_one(bc_ref, bn_ref, q_ref, kn_ref, vn_ref, kc_ref, vc_ref, o_ref):
    t_pad, w = q_ref.shape
    scale = ATT_HDIM ** -0.5
    q = q_ref[...] * scale
    lane = lax.broadcasted_iota(jnp.int32, (t_pad, w), 1)
    head_of_lane = lane // ATT_HDIM
    zero = jnp.zeros_like(q)
    lhs = jnp.concatenate([jnp.where(head_of_lane == h, q, zero) for h in range(ATT_HEADS)], axis=0).astype(BF16)
    s_c = _dot(lhs, kc_ref[...].astype(BF16))
    s_n = _dot_nt(lhs, kn_ref[...].astype(BF16))
    vc = vc_ref[...].astype(BF16)
    vn = vn_ref[...].astype(BF16)
    pcs, pns, ls, lses = [], [], [], []
    for br in range(len(DILATIONS)):
        bc = bc_ref[br]
        bn = bn_ref[br]
        sc = jnp.where(bc > 0.5 * NEG_BIG, s_c + bc, NEG_BIG)
        sn = jnp.where(bn > 0.5 * NEG_BIG, s_n + bn, NEG_BIG)
        m = jnp.maximum(jnp.max(sc, axis=-1, keepdims=True), jnp.max(sn, axis=-1, keepdims=True))
        pc = jnp.exp(sc - m)
        pn = jnp.exp(sn - m)
        l = jnp.sum(pc, axis=-1, keepdims=True) + jnp.sum(pn, axis=-1, keepdims=True)
        pcs.append(pc.astype(BF16))
        pns.append(pn.astype(BF16))
        ls.append(l)
        lses.append(m + jnp.log(l))
    nq = ATT_HEADS * t_pad
    pv = _dot_nt(jnp.concatenate(pcs, axis=0), vc) + _dot(jnp.concatenate(pns, axis=0), vn)
    outs = [pv[br * nq:(br + 1) * nq] / ls[br] for br in range(len(DILATIONS))]
    mx = functools.reduce(jnp.maximum, lses)
    es = [jnp.exp(x - mx) for x in lses]
    den = functools.reduce(lambda a, b: a + b, es)
    o = functools.reduce(lambda a, b: a + b, [e * x for e, x in zip(es, outs)]) / den
    res = zero
    for h in range(ATT_HEADS):
        res = jnp.where(head_of_lane == h, o[h * t_pad:(h + 1) * t_pad, :], res)
    o_ref[...] = res.astype(o_ref.dtype)


def _attn_sample(q, k, v, cache_k, cache_v, layer, bias_c, bias_n, *, nb):
    b, t_pad, w = q.shape
    n_past = cache_k.shape[3]
    new = pl.BlockSpec((nb, t_pad, w), lambda i: (i, 0, 0))
    cache = pl.BlockSpec((None, nb, w, n_past), lambda i: (layer, i, 0, 0))
    return pl.pallas_call(
        _attn_sample_kernel,
        grid=(b // nb,),
        in_specs=[_const_spec(bias_c.shape, lambda i: (0, 0, 0)),
                  _const_spec(bias_n.shape, lambda i: (0, 0, 0)),
                  new, new, new, cache, cache],
        out_specs=new,
        out_shape=jax.ShapeDtypeStruct((b, t_pad, w), F32),
        compiler_params=_params(("parallel",)),
        name="attn_sample",
    )(bias_c, bias_n, q, k, v, cache_k, cache_v)


def _hgrn_tables(chunk):
    sub = min(chunk, LANES)
    r = np.arange(sub)[:, None]
    c = np.arange(sub)[None, :]
    x = np.bitwise_xor(r, c)
    hb = np.zeros_like(x)
    for bit in range(1, 16):
        hb = np.where(x >> bit > 0, bit, hb)
    lev = np.where(c > r, -1, np.where(r // SUBLANES == c // SUBLANES, 0, hb)).astype(np.int32)
    lev = np.tile(lev, (chunk // sub, 1))
    sel = np.zeros((SUBLANES * LANES, sub), np.float32)
    for s in range(SUBLANES):
        sel[s * LANES:(s + 1) * LANES, s::SUBLANES] = 1.0
    rr = np.arange(chunk)
    tri = (rr[None, :] <= rr[:, None]).astype(np.float32)
    return lev, sel, tri


def _split3(x):
    hi = x.astype(BF16)
    r1 = x - hi.astype(F32)
    mid = r1.astype(BF16)
    lo = (r1 - mid.astype(F32)).astype(BF16)
    return hi, mid, lo


def _group_row(x, s):
    c, l = x.shape
    x3 = x.reshape(c // SUBLANES, SUBLANES, l)
    return jnp.broadcast_to(x3[:, s:s + 1, :], x3.shape).reshape(c, l)


def _block_last(x, m):
    c, l = x.shape
    x3 = x.reshape(c // m, m, l)
    return jnp.broadcast_to(x3[:, m - 1:m, :], x3.shape).reshape(c, l)


def _hgrn_lower_bound(raw, layer):
    e = jnp.exp(raw - jnp.max(raw, axis=0, keepdims=True))
    sm = e / jnp.sum(e, axis=0, keepdims=True)
    lb = jnp.zeros_like(sm[0:1])
    for i in range(1, layer + 1):
        lb = lb + sm[i:i + 1]
    return lb


def _hgrn_chunk_gates(zq, zf, zi, lb, tri, live):
    dk = zq.shape[1]
    q = _silu(zq)
    f = lb + (1.0 - lb) * jax.nn.sigmoid(zf)
    g = jnp.log2(f)
    k = 1.0 - f
    if live is not None:
        g = jnp.where(live, g, 0.0)
        k = jnp.where(live, k, 0.0)
    b3 = _dot(tri, jnp.concatenate(_split3(g), axis=1))
    b = b3[:, :dk] + b3[:, dk:2 * dk] + b3[:, 2 * dk:]
    return q, k, b, zi.astype(BF16)


def _hgrn_chunk_local(q, k, b, vb, lev, sel):
    chunk, dk = q.shape
    sub = lev.shape[1]

    vals = []
    for s in range(SUBLANES):
        decay = jnp.exp2(jnp.minimum(b - _group_row(b, s), 0.0))
        vals.append((q * decay * _group_row(k, s)).astype(BF16))
    a = jnp.where(lev == 0, _dot(jnp.concatenate(vals, axis=1), sel), 0.0)
    subs = [slice(r0, r0 + sub) for r0 in range(0, chunk, sub)]
    for bit in range(3, int(math.log2(sub))):
        m = 1 << bit
        b_end = _block_last(b, m)
        b_prev = jnp.concatenate([jnp.zeros((m, dk), F32), b_end[:chunk - m]], axis=0)
        qd = (q * jnp.exp2(b - b_prev)).astype(BF16)
        kd = (k * jnp.exp2(b_end - b)).astype(BF16)
        a_l = jnp.concatenate([_dot_nt(qd[rs], kd[rs]) for rs in subs], axis=0)
        a = jnp.where(lev == bit, a_l, a)
    ab = a.astype(BF16)
    o_parts = []
    for i, rs in enumerate(subs):
        if i == 0:
            o_parts.append(_dot(ab[rs], vb[rs]))
            continue
        b_piv = b[rs.start - 1:rs.start, :]
        qd = (q[rs] * jnp.exp2(b[rs] - b_piv)).astype(BF16)
        kd = (k[:rs.start] * jnp.exp2(b_piv - b[:rs.start])).astype(BF16)
        a_row = jnp.concatenate([_dot_nt(qd, kd).astype(BF16), ab[rs]], axis=1)
        o_parts.append(_dot(a_row, vb[:rs.stop]))
    o = jnp.concatenate(o_parts, axis=0) if len(o_parts) > 1 else o_parts[0]
    q_in = (q * jnp.exp2(b)).astype(BF16)
    b_last = b[chunk - 1:chunk, :]
    kd = (k * jnp.exp2(b_last - b)).astype(BF16)
    return o, q_in, _dot_tn(vb, kd), jnp.exp2(b_last)


def _hgrn_chunk_finish(o_local, q_in, st, ng, zg):
    o = o_local + _dot_nt(q_in, st.astype(BF16))
    on = o * lax.rsqrt(jnp.mean(o * o, axis=-1, keepdims=True) + EPS) * ng
    return on * _silu(zg)


def _hgrn_sample_kernel(zh_ref, lb_ref, ng_ref, lev_ref, sel_ref, tri_ref, s0_ref, o_ref, sf_ref,
                        *, nb, layer, t_valid):
    t_pad = zh_ref.shape[1]
    hd = o_ref.shape[2]
    dk = hd // HG_HEADS
    lev = lev_ref[...]
    sel = sel_ref[...]
    tri = tri_ref[...]
    ng = ng_ref[...]
    live = lax.broadcasted_iota(jnp.int32, (t_pad, dk), 0) < t_valid
    lbs = [_hgrn_lower_bound(lb_ref[:, h * dk:(h + 1) * dk], layer) for h in range(HG_HEADS)]

    def one_batch(bi, carry):
        part = lambda h, p: zh_ref[bi, :, (p * HG_HEADS + h) * dk:(p * HG_HEADS + h + 1) * dk]
        heads = range(HG_HEADS)
        gates = [_hgrn_chunk_gates(part(h, 0), part(h, 1), part(h, 2), lbs[h], tri, live) for h in heads]
        states = [s0_ref[bi, h].T for h in heads]
        local = [_hgrn_chunk_local(*gates[h], lev, sel) for h in heads]
        for h in heads:
            o_local, q_in, s_own, d_all = local[h]
            o_ref[bi, :, h * dk:(h + 1) * dk] = _hgrn_chunk_finish(o_local, q_in, states[h], ng, part(h, 3))
            sf_ref[bi, h] = (states[h] * d_all + s_own).T
        return carry

    lax.fori_loop(0, nb, one_batch, 0, unroll=2)


def _hgrn_sample(zh, lb_raw, norm_g, s0, layer, tables, *, nb, t_valid):
    b, t_pad, w4 = zh.shape
    hd = w4 // 4
    dk = hd // HG_HEADS
    depth = lb_raw.shape[0]
    lev, sel, tri = tables
    return pl.pallas_call(
        functools.partial(_hgrn_sample_kernel, nb=nb, layer=layer, t_valid=t_valid),
        grid=(b // nb,),
        in_specs=[pl.BlockSpec((nb, t_pad, w4), lambda i: (i, 0, 0)),
                  pl.BlockSpec((depth, hd), lambda i: (0, 0)),
                  pl.BlockSpec((None, 1, dk), lambda i: (layer, 0, 0)),
                  _const_spec(lev.shape, lambda i: (0, 0)),
                  _const_spec(sel.shape, lambda i: (0, 0)),
                  _const_spec(tri.shape, lambda i: (0, 0)),
                  pl.BlockSpec((None, nb, HG_HEADS, dk, dk), lambda i: (layer, i, 0, 0, 0))],
        out_specs=[pl.BlockSpec((nb, t_pad, hd), lambda i: (i, 0, 0)),
                   pl.BlockSpec((nb, HG_HEADS, dk, dk), lambda i: (i, 0, 0, 0))],
        out_shape=[jax.ShapeDtypeStruct((b, t_pad, hd), F32),
                   jax.ShapeDtypeStruct((b, HG_HEADS, dk, dk), F32)],
        compiler_params=_params(("parallel",)),
        name="hgrn_sample",
    )(zh, lb_raw, norm_g, lev, sel, tri, s0)


def _proj_mix_kernel(x_ref, ln_ref, wi_ref, dww_ref, dwb_ref, lng_ref, lnb_ref, lb_ref, ng_ref,
                     lev_ref, sel_ref, tri_ref,
                     yc_ref, q_ref, k_ref, v_ref, yh_ref, tail_ref, sf_ref, kt_ref, vt_ref,
                     ext_ref, st_ref, *, layer, chunk):
    tstep = pl.program_id(1)
    tt = x_ref.shape[0]
    conv_c = yc_ref.shape[1]
    att = q_ref.shape[1]
    hd = yh_ref.shape[1]
    dk = hd // HG_HEADS
    n_chunks = tt // chunk

    @pl.when(tstep == 0)
    def _():
        ext_ref[0, 0:CONV_HALO, :] = jnp.zeros((CONV_HALO, conv_c), F32)
        ext_ref[0, CONV_HALO + tt:CONV_HALO + tt + SUBLANES, :] = jnp.zeros((SUBLANES, conv_c), F32)
        st_ref[...] = jnp.zeros_like(st_ref)

    h = _rms(x_ref[...], ln_ref[...]).astype(BF16)
    c0 = 2 * conv_c
    zc = _dot(h, wi_ref[:, 0:c0])
    zh = _dot(h, wi_ref[:, c0 + 3 * att:c0 + 3 * att + 4 * hd])
    yc, u = _conv_tile(zc, ext_ref, 0, dww_ref, dwb_ref[...], lng_ref[...], lnb_ref[...])
    yc_ref[...] = yc.astype(yc_ref.dtype)
    tail_ref[...] = u[tt - CONV_HALO:, :]
    for i, (ref, t_ref) in enumerate(((q_ref, None), (k_ref, kt_ref), (v_ref, vt_ref))):
        z = _dot(h, wi_ref[:, c0 + i * att:c0 + (i + 1) * att])
        ref[...] = z
        if t_ref is not None:
            t_ref[...] = z.T

    lev = lev_ref[...]
    sel = sel_ref[...]
    tri = tri_ref[...]
    ng = ng_ref[...]
    lbs = [_hgrn_lower_bound(lb_ref[:, hh * dk:(hh + 1) * dk], layer) for hh in range(HG_HEADS)]

    def part(u_idx, p):
        hh, c = divmod(u_idx, n_chunks)
        return zh[c * chunk:(c + 1) * chunk, (p * HG_HEADS + hh) * dk:(p * HG_HEADS + hh + 1) * dk]

    def gates(u_idx):
        return _hgrn_chunk_gates(part(u_idx, 0), part(u_idx, 1), part(u_idx, 2), lbs[u_idx // n_chunks], tri, None)

    ahead = gates(0)
    st = None
    for u_idx in range(HG_HEADS * n_chunks):
        hh, c = divmod(u_idx, n_chunks)
        cur = ahead
        if u_idx + 1 < HG_HEADS * n_chunks:
            ahead = gates(u_idx + 1)
        if c == 0:
            st = st_ref[hh]
        o_local, q_in, s_own, d_all = _hgrn_chunk_local(*cur, lev, sel)
        o = _hgrn_chunk_finish(o_local, q_in, st, ng, part(u_idx, 3))
        yh_ref[c * chunk:(c + 1) * chunk, hh * dk:(hh + 1) * dk] = o.astype(yh_ref.dtype)
        st = st * d_all + s_own
        if c == n_chunks - 1:
            st_ref[hh] = st

    @pl.when(tstep == pl.num_programs(1) - 1)
    def _():
        for hh in range(HG_HEADS):
            sf_ref[hh] = st_ref[hh].T


def _proj_mix(x, layer, ln, wi, dww, dwb, lng, lnb, lb_raw, norm_g, tables, *, batch, tm, chunk, widths, keep):
    n, d = x.shape
    conv_c = widths[0] // 2
    att = widths[1]
    hd = widths[4] // 4
    dk = hd // HG_HEADS
    depth = lb_raw.shape[0]
    nt = n // batch // tm
    lev, sel, tri = tables
    row = lambda w: pl.BlockSpec((tm, w), lambda b, j: (b * nt + j, 0))
    cvec = pl.BlockSpec((None, 1, conv_c), lambda b, j: (layer, 0, 0))
    const2 = lambda a: _const_spec(a.shape, lambda b, j: (0, 0))
    first_kept = nt - keep // tm
    window = pl.BlockSpec((None, att, tm), lambda b, j: (b, 0, jnp.maximum(j - first_kept, 0)))
    return pl.pallas_call(
        functools.partial(_proj_mix_kernel, layer=layer, chunk=chunk),
        grid=(batch, nt),
        in_specs=[row(d), pl.BlockSpec((None, 1, d), lambda b, j: (layer, 0, 0)),
                  _const_spec(wi.shape, lambda b, j: (0, 0)),
                  pl.BlockSpec((None, CONV_HALO, conv_c), lambda b, j: (layer, 0, 0)),
                  cvec, cvec, cvec,
                  pl.BlockSpec((depth, hd), lambda b, j: (0, 0)),
                  pl.BlockSpec((None, 1, dk), lambda b, j: (layer, 0, 0)),
                  const2(lev), const2(sel), const2(tri)],
        out_specs=[row(conv_c), row(att), row(att), row(att), row(hd),
                   pl.BlockSpec((None, CONV_HALO, conv_c), lambda b, j: (b, 0, 0)),
                   pl.BlockSpec((None, HG_HEADS, dk, dk), lambda b, j: (b, 0, 0, 0)),
                   window, window],
        out_shape=[jax.ShapeDtypeStruct((n, conv_c), BF16),
                   jax.ShapeDtypeStruct((n, att), F32),
                   jax.ShapeDtypeStruct((n, att), F32),
                   jax.ShapeDtypeStruct((n, att), F32),
                   jax.ShapeDtypeStruct((n, hd), BF16),
                   jax.ShapeDtypeStruct((batch, CONV_HALO, conv_c), F32),
                   jax.ShapeDtypeStruct((batch, HG_HEADS, dk, dk), F32),
                   jax.ShapeDtypeStruct((batch, att, keep), F32),
                   jax.ShapeDtypeStruct((batch, att, keep), F32)],
        scratch_shapes=[pltpu.VMEM((1, CONV_HALO + tm + SUBLANES, conv_c), F32),
                        pltpu.VMEM((HG_HEADS, dk, dk), F32)],
        compiler_params=_params(("parallel", "arbitrary")),
        name="proj_mix",
    )(x, ln, wi, dww, dwb, lng, lnb, lb_raw, norm_g, lev, sel, tri)


def _pad_rows(x, rows):
    return jnp.pad(x, ((0, 0), (0, rows - x.shape[1]), (0, 0)))


def _key_minor(cache):
    depth, b, n_past, h, hd = cache.shape
    return jnp.transpose(cache, (0, 1, 3, 4, 2)).reshape(depth, b, h * hd, n_past)


def kernel(x_prompt, x_sample, state_conv, cache_k_win, cache_v_win, state_hgrn, ln_ffn1, w_ffn1_gate, w_ffn1_up, w_ffn1_down, ln_mix, w_in, conv_dw_w, conv_dw_b, conv_ln_g, conv_ln_b, hg_lower_bounds, hg_norm_g, w_out, ln_ffn2, w_ffn2_gate, w_ffn2_up, w_ffn2_down, ln_final):
    bp, seq, d = x_prompt.shape
    bs, t_new, _ = x_sample.shape
    depth = w_in.shape[0]
    conv_dim = conv_dw_w.shape[2]
    att_dim = ATT_HEADS * ATT_HDIM
    hg_dim = hg_lower_bounds.shape[1]
    widths = (2 * conv_dim, att_dim, att_dim, att_dim, 4 * hg_dim)
    n_past = cache_k_win.shape[2]
    t_pad = SUBLANES
    keep = min(ATT_SPAN, seq)

    early_f32 = (w_ffn1_gate, w_ffn1_up, w_ffn1_down, w_in, w_out)
    ffn2_f32 = (w_ffn2_gate, w_ffn2_up, w_ffn2_down)
    early_w = tuple(w[0].astype(BF16) for w in early_f32)
    ffn2_w = tuple(w[0].astype(BF16) for w in ffn2_f32)
    vec3 = lambda a: a.reshape(depth, 1, a.shape[-1])
    ln1, lnm, ln2 = vec3(ln_ffn1), vec3(ln_mix), vec3(ln_ffn2)
    dwb, clg, clb, hgn = vec3(conv_dw_b), vec3(conv_ln_g), vec3(conv_ln_b), vec3(hg_norm_g)
    lnf = ln_final.reshape(1, d)
    dww = jnp.pad(conv_dw_w, ((0, 0), (0, CONV_HALO - CONV_WIDTH), (0, 0)))
    cache_k, cache_v = _key_minor(cache_k_win), _key_minor(cache_v_win)
    halo_s = jnp.pad(state_conv, ((0, 0), (0, 0), (CONV_HALO - (CONV_WIDTH - 1), 0), (0, 0)))

    bias_p = jnp.asarray(_prompt_bias_tables())
    bias_c, bias_n = (jnp.asarray(a) for a in _sample_bias_tables(n_past, t_new, t_pad))
    chunk_p = 128
    tabs_p = tuple(jnp.asarray(a, dt) for a, dt in zip(_hgrn_tables(chunk_p), (jnp.int32, BF16, BF16)))
    tabs_s = tuple(jnp.asarray(a, dt) for a, dt in zip(_hgrn_tables(t_pad), (jnp.int32, BF16, BF16)))

    hp = x_prompt.reshape(bp * seq, d)
    hs = x_sample.reshape(bs * t_new, d)
    outs = [[] for _ in range(8)]
    for l in range(depth):
        final = l == depth - 1
        nxt1 = early_f32 if not final else ()
        nxt2 = ffn2_f32 if not final else ()
        wg1, wu1, wd1, wi, wo = early_w
        x1, x1s, early_next = _ffn_half_step(hp, hs, l, ln1, wg1, wu1, wd1, nxt1, l + 1, tm=512, f_chunk=1408)
        yc, q, k, v, yh, tail, sp, kt, vt = _proj_mix(x1, l, lnm, wi, dww, dwb, clg, clb, hg_lower_bounds, hgn,
                                                      tabs_p, batch=bp, tm=512, chunk=chunk_p, widths=widths,
                                                      keep=keep)
        seq3 = lambda a: a.reshape(bp, seq, a.shape[-1])
        ya = _attn_prompt(seq3(q), seq3(k), seq3(v), bias_p, span=ATT_SPAN)
        window = lambda a: a.reshape(bp, ATT_HEADS, ATT_HDIM, keep).transpose(0, 3, 1, 2)
        outs[0].append(tail[:, -(CONV_WIDTH - 1):])
        outs[2].append(window(kt))
        outs[3].append(window(vt))
        outs[6].append(sp)

        zcs, qs, ks, vs, zhs = _proj_in(x1s, l, lnm, wi, tm=bs * t_new, widths=widths)
        new3 = lambda a: _pad_rows(a.reshape(bs, t_new, a.shape[-1]), t_pad)
        ycs, us = _conv_group(new3(zcs), halo_s[l], l, dww, dwb, clg, clb, nb=bs, tt=t_pad, out_dtype=F32)
        yas = _attn_sample(new3(qs), new3(ks), new3(vs), cache_k, cache_v, l, bias_c, bias_n, nb=2)
        yhs, ss = _hgrn_sample(new3(zhs), hg_lower_bounds, hgn, state_hgrn, l, tabs_s, nb=8, t_valid=t_new)
        unpad = lambda a: a[:, :t_new].reshape(bs * t_new, a.shape[-1])
        hp, hs, ffn2_next = _dense_out((x1, yc, ya.reshape(bp * seq, att_dim), yh),
                                       (x1s, unpad(ycs), unpad(yas), unpad(yhs)),
                                       l, wo, ln2, *ffn2_w, lnf, nxt2, l + 1, tm=512, f_chunk=1408, final=final)
        early_w, ffn2_w = early_next, ffn2_next
        outs[1].append(jnp.concatenate([state_conv[l][:, t_new:], us[:, :t_new]], axis=1))
        outs[4].append(ks.reshape(bs, t_new, ATT_HEADS, ATT_HDIM))
        outs[5].append(vs.reshape(bs, t_new, ATT_HEADS, ATT_HDIM))
        outs[7].append(ss)

    y_prompt = hp.reshape(bp, seq, d)
    y_sample = hs.reshape(bs, t_new, d)
    st = [jnp.stack(o) for o in outs]
    return (y_prompt, y_sample, st[0], st[1], st[2], st[3], st[4], st[5], st[6], st[7])
```

```python
import functools
import math

import numpy as np
import jax
import jax.numpy as jnp
from jax import lax
from jax.experimental import pallas as pl
from jax.experimental.pallas import tpu as pltpu

F32 = jnp.float32
BF16 = jnp.bfloat16

EPS = 1e-6
NEG_BIG = -1e30
CONV_WIDTH = 31
CONV_HALO = 32
ATT_HEADS = 4
ATT_HDIM = 64
ATT_BLK = 128
DILATIONS = (1, 4, 16)
ATT_SPAN = 2048
ATT_UNROLL = 16
HG_HEADS = 4
LANES = 128
SUBLANES = 8
BF16_ROWS = 16
VMEM_LIMIT = 56 * 1024 * 1024


def _params(sem, vmem=VMEM_LIMIT):
    return pltpu.CompilerParams(dimension_semantics=sem, vmem_limit_bytes=vmem)


def _const_spec(shape, index):
    return pl.BlockSpec(shape, index, pipeline_mode=pl.Buffered(1))


def _rms(x, g):
    ms = jnp.mean(x * x, axis=-1, keepdims=True)
    return x * lax.rsqrt(ms + EPS) * g


def _silu(x):
    return x * jax.nn.sigmoid(x)


def _dot(a, b):
    return jnp.dot(a, b, preferred_element_type=F32)


def _dot_nt(a, b):
    return lax.dot_general(a, b, (((1,), (1,)), ((), ())), preferred_element_type=F32)


def _dot_tn(a, b):
    return lax.dot_general(a, b, (((0,), (0,)), ((), ())), preferred_element_type=F32)


def _swiglu(h, wg_ref, wu_ref, wd_ref, f_chunk):
    ffn = wg_ref.shape[1]
    acc = None
    for c0 in range(0, ffn, f_chunk):
        g = _dot(h, wg_ref[:, c0:c0 + f_chunk])
        u = _dot(h, wu_ref[:, c0:c0 + f_chunk])
        a = (_silu(g) * u).astype(BF16)
        y = _dot(a, wd_ref[c0:c0 + f_chunk, :])
        acc = y if acc is None else acc + y
    return acc


def _main_then_extra(n_main, main, extra):
    i = pl.program_id(0)
    pl.when(i < n_main)(main)
    pl.when(i == n_main)(extra)


def _cast_specs(cast, n_steps):
    in_specs, out_specs, shapes = [], [], []
    for w, layer in cast:
        _, r, c = w.shape
        rows = -(-r // n_steps)
        rows = -(-rows // BF16_ROWS) * BF16_ROWS
        while r % rows:
            rows += BF16_ROWS
        last = r // rows - 1
        in_specs.append(pl.BlockSpec((None, rows, c),
                                     lambda i, last=last, layer=layer: (layer, jnp.minimum(i, last), 0)))
        out_specs.append(pl.BlockSpec((rows, c), lambda i, last=last: (jnp.minimum(i, last), 0)))
        shapes.append(jax.ShapeDtypeStruct((r, c), BF16))
    return in_specs, out_specs, shapes


def _cast_slabs(src_refs, dst_refs):
    for src, dst in zip(src_refs, dst_refs):
        dst[...] = src[...].astype(dst.dtype)


def _ffn_kernel(*refs, f_chunk, n_main, n_cast):
    xp_ref, xs_ref, ln_ref, wg_ref, wu_ref, wd_ref = refs[:6]
    cast_src = refs[6:6 + n_cast]
    op_ref, os_ref = refs[6 + n_cast:8 + n_cast]
    cast_dst = refs[8 + n_cast:]

    def run(x_ref, o_ref):
        _cast_slabs(cast_src, cast_dst)
        x = x_ref[...]
        h = _rms(x, ln_ref[...]).astype(BF16)
        o_ref[...] = x + 0.5 * _swiglu(h, wg_ref, wu_ref, wd_ref, f_chunk)

    _main_then_extra(n_main, lambda: run(xp_ref, op_ref), lambda: run(xs_ref, os_ref))


def _ffn_half_step(xp, xs, layer, ln, wg, wu, wd, cast, *, tm, f_chunk):
    n, d = xp.shape
    ns = xs.shape[0]
    ffn = wg.shape[1]
    n_main = n // tm
    row = pl.BlockSpec((tm, d), lambda i: (jnp.minimum(i, n_main - 1), 0))
    extra = pl.BlockSpec((ns, d), lambda i: (0, 0))
    c_in, c_out, c_shapes = _cast_specs(cast, n_main + 1)
    res = pl.pallas_call(
        functools.partial(_ffn_kernel, f_chunk=f_chunk, n_main=n_main, n_cast=len(cast)),
        grid=(n_main + 1,),
        in_specs=[row, extra, pl.BlockSpec((None, 1, d), lambda i: (layer, 0, 0)),
                  _const_spec((d, ffn), lambda i: (0, 0)),
                  _const_spec((d, ffn), lambda i: (0, 0)),
                  _const_spec((ffn, d), lambda i: (0, 0))] + c_in,
        out_specs=[row, extra] + c_out,
        out_shape=[jax.ShapeDtypeStruct((n, d), F32), jax.ShapeDtypeStruct((ns, d), F32)] + c_shapes,
        compiler_params=_params(("arbitrary",)),
        name="ffn_half_step",
    )(xp, xs, ln, wg, wu, wd, *[w for w, _ in cast])
    return res[0], res[1], tuple(res[2:])


def _proj_in_kernel(x_ref, ln_ref, wi_ref, zc_ref, q_ref, k_ref, v_ref, zh_ref):
    h = _rms(x_ref[...], ln_ref[...]).astype(BF16)
    c0 = 0
    for ref in (zc_ref, q_ref, k_ref, v_ref, zh_ref):
        w = ref.shape[1]
        ref[...] = _dot(h, wi_ref[:, c0:c0 + w])
        c0 += w


def _proj_in(x, layer, ln, wi, *, tm, widths):
    n, d = x.shape
    row = lambda w: pl.BlockSpec((tm, w), lambda i: (i, 0))
    return pl.pallas_call(
        _proj_in_kernel,
        grid=(n // tm,),
        in_specs=[row(d), pl.BlockSpec((None, 1, d), lambda i: (layer, 0, 0)),
                  _const_spec(wi.shape, lambda i: (0, 0))],
        out_specs=[row(w) for w in widths],
        out_shape=[jax.ShapeDtypeStruct((n, w), F32) for w in widths],
        compiler_params=_params(("parallel",)),
        name="proj_in",
    )(x, ln, wi)


def _dense_out_kernel(*refs, f_chunk, final, n_main, n_cast):
    (xp_ref, ycp_ref, yap_ref, yhp_ref, xs_ref, ycs_ref, yas_ref, yhs_ref,
     wo_ref, ln2_ref, wg_ref, wu_ref, wd_ref, lnf_ref) = refs[:14]
    cast_src = refs[14:14 + n_cast]
    op_ref, os_ref = refs[14 + n_cast:16 + n_cast]
    cast_dst = refs[16 + n_cast:]

    def run(x_ref, yc_ref, ya_ref, yh_ref, o_ref):
        _cast_slabs(cast_src, cast_dst)
        y = jnp.concatenate([r[...].astype(BF16) for r in (yc_ref, ya_ref, yh_ref)], axis=-1)
        x2 = x_ref[...] + _dot(y, wo_ref[...])
        h = _rms(x2, ln2_ref[...]).astype(BF16)
        x3 = x2 + 0.5 * _swiglu(h, wg_ref, wu_ref, wd_ref, f_chunk)
        if final:
            x3 = _rms(x3, lnf_ref[...])
        o_ref[...] = x3

    _main_then_extra(n_main, lambda: run(xp_ref, ycp_ref, yap_ref, yhp_ref, op_ref),
                     lambda: run(xs_ref, ycs_ref, yas_ref, yhs_ref, os_ref))


def _dense_out(prompt, sample, layer, wo, ln2, wg, wu, wd, lnf, cast, *, tm, f_chunk, final):
    n, d = prompt[0].shape
    ns = sample[0].shape[0]
    ffn = wg.shape[1]
    n_main = n // tm
    row = lambda a: pl.BlockSpec((tm, a.shape[1]), lambda i: (jnp.minimum(i, n_main - 1), 0))
    extra = lambda a: pl.BlockSpec((ns, a.shape[1]), lambda i: (0, 0))
    vec = pl.BlockSpec((None, 1, d), lambda i: (layer, 0, 0))
    c_in, c_out, c_shapes = _cast_specs(cast, n_main + 1)
    res = pl.pallas_call(
        functools.partial(_dense_out_kernel, f_chunk=f_chunk, final=final, n_main=n_main, n_cast=len(cast)),
        grid=(n_main + 1,),
        in_specs=[row(a) for a in prompt] + [extra(a) for a in sample] + [
                  _const_spec((d, d), lambda i: (0, 0)),
                  vec,
                  _const_spec((d, ffn), lambda i: (0, 0)),
                  _const_spec((d, ffn), lambda i: (0, 0)),
                  _const_spec((ffn, d), lambda i: (0, 0)),
                  pl.BlockSpec((1, d), lambda i: (0, 0))] + c_in,
        out_specs=[row(prompt[0]), extra(sample[0])] + c_out,
        out_shape=[jax.ShapeDtypeStruct((n, d), F32), jax.ShapeDtypeStruct((ns, d), F32)] + c_shapes,
        compiler_params=_params(("arbitrary",)),
        name="dense_out",
    )(*prompt, *sample, wo, ln2, wg, wu, wd, lnf, *[w for w, _ in cast])
    return res[0], res[1], tuple(res[2:])


def _conv_tile(z, ext_ref, b, dww_ref, dwb, lng, lnb):
    tt, c2 = z.shape
    c = c2 // 2
    lo = CONV_HALO - (CONV_WIDTH - 1)
    u = z[:, :c] * jax.nn.sigmoid(z[:, c:])
    ext_ref[b, CONV_HALO:CONV_HALO + tt, :] = u
    y = None
    for r in range(SUBLANES):
        z_r = None
        for o in range(r, CONV_HALO + 1, SUBLANES):
            if o < lo:
                continue
            term = ext_ref[b, o - r:o - r + tt + SUBLANES, :] * dww_ref[o - lo:o - lo + 1, :]
            z_r = term if z_r is None else z_r + term
        part = z_r[r:r + tt, :]
        y = part if y is None else y + part
    y = y + dwb
    yc = y - jnp.mean(y, axis=-1, keepdims=True)
    yn = yc * lax.rsqrt(jnp.mean(yc * yc, axis=-1, keepdims=True) + EPS)
    carry = ext_ref[b, tt:tt + CONV_HALO, :]
    ext_ref[b, 0:CONV_HALO, :] = carry
    return _silu(yn * lng + lnb), u


def _conv_kernel(zc_ref, halo_ref, dww_ref, dwb_ref, lng_ref, lnb_ref, y_ref, tail_ref, ext_ref, *, nb, tt, tail):
    c = y_ref.shape[-1]

    def one_batch(b):
        @pl.when(pl.program_id(1) == 0)
        def _():
            ext_ref[b, 0:CONV_HALO, :] = halo_ref[b]
            ext_ref[b, CONV_HALO + tt:CONV_HALO + tt + SUBLANES, :] = jnp.zeros((SUBLANES, c), F32)

        y, u = _conv_tile(zc_ref[b], ext_ref, b, dww_ref, dwb_ref[...], lng_ref[...], lnb_ref[...])
        y_ref[b] = y.astype(y_ref.dtype)
        tail_ref[b] = u[tt - tail:, :]

    if nb == 1:
        one_batch(0)
    else:
        def body(b, carry):
            one_batch(b)
            return carry
        lax.fori_loop(0, nb, body, 0)


def _conv_group(zc, halo, layer, dww, dwb, lng, lnb, *, nb, tt, out_dtype):
    b, t, c2 = zc.shape
    c = c2 // 2
    tail = min(CONV_HALO, tt)
    vec = pl.BlockSpec((None, 1, c), lambda i, j: (layer, 0, 0))
    return pl.pallas_call(
        functools.partial(_conv_kernel, nb=nb, tt=tt, tail=tail),
        grid=(b // nb, t // tt),
        in_specs=[pl.BlockSpec((nb, tt, c2), lambda i, j: (i, j, 0)),
                  pl.BlockSpec((nb, CONV_HALO, c), lambda i, j: (i, 0, 0)),
                  pl.BlockSpec((None, CONV_HALO, c), lambda i, j: (layer, 0, 0)),
                  vec, vec, vec],
        out_specs=[pl.BlockSpec((nb, tt, c), lambda i, j: (i, j, 0)),
                   pl.BlockSpec((nb, tail, c), lambda i, j: (i, 0, 0))],
        out_shape=[jax.ShapeDtypeStruct((b, t, c), out_dtype),
                   jax.ShapeDtypeStruct((b, tail, c), F32)],
        scratch_shapes=[pltpu.VMEM((nb, CONV_HALO + tt + SUBLANES, c), F32)],
        compiler_params=_params(("parallel", "arbitrary")),
        name="conv_group",
    )(zc, halo, dww, dwb, lng, lnb)


def _alibi_slopes():
    return [2.0 ** (-8.0 * (h + 1) / ATT_HEADS) for h in range(ATT_HEADS)]


def _prompt_bias_tables():
    qi = np.arange(ATT_BLK)[:, None]
    ki = np.arange(2 * ATT_BLK)[None, :]
    j = qi - ki + ATT_BLK
    tabs = []
    for dil in DILATIONS:
        for first in (False, True):
            valid = (j >= 0) & (j <= ATT_BLK) & ((not first) | (ki >= ATT_BLK))
            rows = []
            for s in _alibi_slopes():
                bias = (-np.float32(s)) * (j * dil).astype(np.float32)
                rows.append(np.where(valid, bias, np.float32(NEG_BIG)).astype(np.float32))
            tabs.append(np.concatenate(rows, axis=0))
    return np.stack(tabs, axis=0)


def _head_pair_split(x, lo_mask):
    zero = jnp.zeros_like(x)
    return jnp.concatenate([jnp.where(lo_mask, x, zero), jnp.where(lo_mask, zero, x)], axis=0)


def _attn_prompt_kernel(bias_ref, q_ref, kp_ref, kc_ref, vp_ref, vc_ref, o_ref,
                        qs_ref, ks_ref, vs_ref, os_ref, ls_ref, *, span):
    j = pl.program_id(1)
    nslab = qs_ref.shape[0]
    scale = ATT_HDIM ** -0.5
    for sl in range(nslab):
        cols = slice(sl * LANES, (sl + 1) * LANES)
        qs_ref[sl] = q_ref[:, cols] * scale
        ks_ref[sl, 0:span, :] = kp_ref[:, cols]
        ks_ref[sl, span:2 * span, :] = kc_ref[:, cols]
        vs_ref[sl, 0:span, :] = vp_ref[:, cols]
        vs_ref[sl, span:2 * span, :] = vc_ref[:, cols]

    lane = lax.broadcasted_iota(jnp.int32, (ATT_BLK, LANES), 1)
    lo_mask = lane < ATT_HDIM
    n_blocks = span // ATT_BLK

    order = sorted(range(len(DILATIONS)), key=lambda b: -DILATIONS[b])
    for pos, br in enumerate(order):
        dil = DILATIONS[br]

        def rows(start, size, dil=dil):
            if dil == 1:
                return pl.ds(start if isinstance(start, int) else pl.multiple_of(start, ATT_BLK), size)
            return pl.ds(start, size, stride=dil)

        def scores(i, dil=dil, rows=rows):
            q0 = (i // dil) * (ATT_BLK * dil) + i % dil
            k0 = span + q0 - ATT_BLK * dil
            s_parts, v_parts = [], []
            for sl in range(nslab):
                ql = qs_ref[sl, rows(q0, ATT_BLK), :]
                kl = ks_ref[sl, rows(k0, 2 * ATT_BLK), :]
                v_parts.append(vs_ref[sl, rows(k0, 2 * ATT_BLK), :].astype(BF16))
                lhs = _head_pair_split(ql, lo_mask).astype(BF16)
                s_parts.append(_dot_nt(lhs, kl.astype(BF16)))
            return jnp.concatenate(s_parts, axis=0), jnp.concatenate(v_parts, axis=1)

        def finish(i, s, v, br=br, dil=dil, pos=pos, rows=rows):
            q0 = (i // dil) * (ATT_BLK * dil) + i % dil
            first = jnp.logical_and(j == 0, i // dil == 0).astype(jnp.int32)
            bias = bias_ref[2 * br + first]
            s = jnp.where(bias > 0.5 * NEG_BIG, s + bias, NEG_BIG)
            m = jnp.max(s, axis=-1, keepdims=True)
            p = jnp.exp(s - m)
            l = jnp.sum(p, axis=-1, keepdims=True)
            pv = _dot(p.astype(BF16), v)
            lse_all = m + jnp.log(l)
            for sl in range(nslab):
                cols = slice(sl * LANES, (sl + 1) * LANES)
                ra = slice((2 * sl) * ATT_BLK, (2 * sl + 1) * ATT_BLK)
                rb = slice((2 * sl + 1) * ATT_BLK, (2 * sl + 2) * ATT_BLK)
                pair = lambda x: jnp.where(lo_mask, jnp.broadcast_to(x[ra], (ATT_BLK, LANES)),
                                           jnp.broadcast_to(x[rb], (ATT_BLK, LANES)))
                o_new = jnp.where(lo_mask, pv[ra, cols], pv[rb, cols]) / pair(l)
                lse_new = pair(lse_all)
                out_rows = rows(q0, ATT_BLK)
                if pos == 0:
                    o_tot, lse_tot = o_new, lse_new
                else:
                    o_old = os_ref[sl, out_rows, :]
                    lse_old = ls_ref[sl, out_rows, :]
                    mx = jnp.maximum(lse_old, lse_new)
                    e_old = jnp.exp(lse_old - mx)
                    e_new = jnp.exp(lse_new - mx)
                    den = e_old + e_new
                    o_tot = (e_old * o_old + e_new * o_new) / den
                    lse_tot = mx + jnp.log(den)
                os_ref[sl, out_rows, :] = o_tot
                ls_ref[sl, out_rows, :] = lse_tot

        def trip(t, carry, scores=scores, finish=finish):
            base = t * ATT_UNROLL
            ahead = scores(base)
            for u in range(ATT_UNROLL):
                s, v = ahead
                if u + 1 < ATT_UNROLL:
                    ahead = scores(base + u + 1)
                finish(base + u, s, v)
            return carry

        if n_blocks == ATT_UNROLL:
            trip(0, 0)
        else:
            lax.fori_loop(0, n_blocks // ATT_UNROLL, trip, 0)

    o_ref[...] = jnp.concatenate([os_ref[sl] for sl in range(nslab)], axis=-1).astype(o_ref.dtype)


def _attn_prompt(q, k, v, bias_tabs, *, span):
    b, t, w = q.shape
    nslab = w // LANES
    cur = pl.BlockSpec((None, span, w), lambda i, j: (i, j, 0))
    prev = pl.BlockSpec((None, span, w), lambda i, j: (i, jnp.maximum(j - 1, 0), 0))
    slab = lambda rows: pltpu.VMEM((nslab, rows, LANES), F32)
    return pl.pallas_call(
        functools.partial(_attn_prompt_kernel, span=span),
        grid=(b, t // span),
        in_specs=[_const_spec(bias_tabs.shape, lambda i, j: (0, 0, 0)), cur, prev, cur, prev, cur],
        out_specs=cur,
        out_shape=jax.ShapeDtypeStruct((b, t, w), BF16),
        scratch_shapes=[slab(span), slab(2 * span), slab(2 * span), slab(span), slab(span)],
        compiler_params=_params(("parallel", "arbitrary")),
        name="attn_prompt",
    )(bias_tabs, q, k, k, v, v)


def _sample_bias_tables(n_past, t_new, t_pad):
    slopes = _alibi_slopes()
    n_keys = n_past + t_pad
    tabs = np.full((len(DILATIONS), ATT_HEADS * t_pad, n_keys), NEG_BIG, np.float32)
    for br, dil in enumerate(DILATIONS):
        for h, s in enumerate(slopes):
            for t in range(t_pad):
                tq = min(t, t_new - 1)
                for jj in range(ATT_BLK + 1):
                    idx = n_past + tq - jj * dil
                    if idx < 0:
                        continue
                    tabs[br, h * t_pad + t, idx] = -np.float32(s) * np.float32(jj * dil)
    return tabs[:, :, :n_past], tabs[:, :, n_past:]


def _attn_sample_kernel(bc_ref, bn_ref, q_ref, kn_ref, vn_ref, kc_ref, vc_ref, o_ref):
    for b in range(q_ref.shape[0]):
        _attn_sample_one(bc_ref, bn_ref, q_ref.at[b], kn_ref.at[b], vn_ref.at[b], kc_ref.at[b], vc_ref.at[b],
                         o_ref.at[b])


def _attn_sample_one(bc_ref, bn_ref, q_ref, kn_ref, vn_ref, kc_ref, vc_ref, o_ref):
    t_pad, w = q_ref.shape
    scale = ATT_HDIM ** -0.5
    q = q_ref[...] * scale
    lane = lax.broadcasted_iota(jnp.int32, (t_pad, w), 1)
    head_of_lane = lane // ATT_HDIM
    zero = jnp.zeros_like(q)
    lhs = jnp.concatenate([jnp.where(head_of_lane == h, q, zero) for h in range(ATT_HEADS)], axis=0).astype(BF16)
    s_c = _dot(lhs, kc_ref[...].astype(BF16))
    s_n = _dot_nt(lhs, kn_ref[...].astype(BF16))
    vc = vc_ref[...].astype(BF16)
    vn = vn_ref[...].astype(BF16)
    pcs, pns, ls, lses = [], [], [], []
    for br in range(len(DILATIONS)):
        bc = bc_ref[br]
        bn = bn_ref[br]
        sc = jnp.where(bc > 0.5 * NEG_BIG, s_c + bc, NEG_BIG)
        sn = jnp.where(bn > 0.5 * NEG_BIG, s_n + bn, NEG_BIG)
        m = jnp.maximum(jnp.max(sc, axis=-1, keepdims=True), jnp.max(sn, axis=-1, keepdims=True))
        pc = jnp.exp(sc - m)
        pn = jnp.exp(sn - m)
        l = jnp.sum(pc, axis=-1, keepdims=True) + jnp.sum(pn, axis=-1, keepdims=True)
        pcs.append(pc.astype(BF16))
        pns.append(pn.astype(BF16))
        ls.append(l)
        lses.append(m + jnp.log(l))
    nq = ATT_HEADS * t_pad
    pv = _dot_nt(jnp.concatenate(pcs, axis=0), vc) + _dot(jnp.concatenate(pns, axis=0), vn)
    outs = [pv[br * nq:(br + 1) * nq] / ls[br] for br in range(len(DILATIONS))]
    mx = functools.reduce(jnp.maximum, lses)
    es = [jnp.exp(x - mx) for x in lses]
    den = functools.reduce(lambda a, b: a + b, es)
    o = functools.reduce(lambda a, b: a + b, [e * x for e, x in zip(es, outs)]) / den
    res = zero
    for h in range(ATT_HEADS):
        res = jnp.where(head_of_lane == h, o[h * t_pad:(h + 1) * t_pad, :], res)
    o_ref[...] = res.astype(o_ref.dtype)


def _attn_sample(q, k, v, cache_k, cache_v, layer, bias_c, bias_n, *, nb):
    b, t_pad, w = q.shape
    n_past = cache_k.shape[3]
    new = pl.BlockSpec((nb, t_pad, w), lambda i: (i, 0, 0))
    cache = pl.BlockSpec((None, nb, w, n_past), lambda i: (layer, i, 0, 0))
    return pl.pallas_call(
        _attn_sample_kernel,
        grid=(b // nb,),
        in_specs=[_const_spec(bias_c.shape, lambda i: (0, 0, 0)),
                  _const_spec(bias_n.shape, lambda i: (0, 0, 0)),
                  new, new, new, cache, cache],
        out_specs=new,
        out_shape=jax.ShapeDtypeStruct((b, t_pad, w), F32),
        compiler_params=_params(("parallel",)),
        name="attn_sample",
    )(bias_c, bias_n, q, k, v, cache_k, cache_v)


def _hgrn_tables(chunk):
    sub = min(chunk, LANES)
    r = np.arange(sub)[:, None]
    c = np.arange(sub)[None, :]
    x = np.bitwise_xor(r, c)
    hb = np.zeros_like(x)
    for bit in range(1, 16):
        hb = np.where(x >> bit > 0, bit, hb)
    lev = np.where(c > r, -1, np.where(r // SUBLANES == c // SUBLANES, 0, hb)).astype(np.int32)
    lev = np.tile(lev, (chunk // sub, 1))
    sel = np.zeros((SUBLANES * LANES, sub), np.float32)
    for s in range(SUBLANES):
        sel[s * LANES:(s + 1) * LANES, s::SUBLANES] = 1.0
    rr = np.arange(chunk)
    tri = (rr[None, :] <= rr[:, None]).astype(np.float32)
    return lev, sel, tri


def _split3(x):
    hi = x.astype(BF16)
    r1 = x - hi.astype(F32)
    mid = r1.astype(BF16)
    lo = (r1 - mid.astype(F32)).astype(BF16)
    return hi, mid, lo


def _group_row(x, s):
    c, l = x.shape
    x3 = x.reshape(c // SUBLANES, SUBLANES, l)
    return jnp.broadcast_to(x3[:, s:s + 1, :], x3.shape).reshape(c, l)


def _block_last(x, m):
    c, l = x.shape
    x3 = x.reshape(c // m, m, l)
    return jnp.broadcast_to(x3[:, m - 1:m, :], x3.shape).reshape(c, l)


def _hgrn_lower_bound(raw, layer):
    e = jnp.exp(raw - jnp.max(raw, axis=0, keepdims=True))
    sm = e / jnp.sum(e, axis=0, keepdims=True)
    lb = jnp.zeros_like(sm[0:1])
    for i in range(1, layer + 1):
        lb = lb + sm[i:i + 1]
    return lb


def _hgrn_chunk_gates(zq, zf, zi, lb, tri, live):
    dk = zq.shape[1]
    q = _silu(zq)
    f = lb + (1.0 - lb) * jax.nn.sigmoid(zf)
    g = jnp.log2(f)
    k = 1.0 - f
    if live is not None:
        g = jnp.where(live, g, 0.0)
        k = jnp.where(live, k, 0.0)
    b3 = _dot(tri, jnp.concatenate(_split3(g), axis=1))
    b = b3[:, :dk] + b3[:, dk:2 * dk] + b3[:, 2 * dk:]
    return q, k, b, zi.astype(BF16)


def _hgrn_chunk_local(q, k, b, vb, lev, sel):
    chunk, dk = q.shape
    sub = lev.shape[1]

    vals = []
    for s in range(SUBLANES):
        decay = jnp.exp2(jnp.minimum(b - _group_row(b, s), 0.0))
        vals.append((q * decay * _group_row(k, s)).astype(BF16))
    a = jnp.where(lev == 0, _dot(jnp.concatenate(vals, axis=1), sel), 0.0)
    subs = [slice(r0, r0 + sub) for r0 in range(0, chunk, sub)]
    for bit in range(3, int(math.log2(sub))):
        m = 1 << bit
        b_end = _block_last(b, m)
        b_prev = jnp.concatenate([jnp.zeros((m, dk), F32), b_end[:chunk - m]], axis=0)
        qd = (q * jnp.exp2(b - b_prev)).astype(BF16)
        kd = (k * jnp.exp2(b_end - b)).astype(BF16)
        a_l = jnp.concatenate([_dot_nt(qd[rs], kd[rs]) for rs in subs], axis=0)
        a = jnp.where(lev == bit, a_l, a)
    ab = a.astype(BF16)
    o_parts = []
    for i, rs in enumerate(subs):
        if i == 0:
            o_parts.append(_dot(ab[rs], vb[rs]))
            continue
        b_piv = b[rs.start - 1:rs.start, :]
        qd = (q[rs] * jnp.exp2(b[rs] - b_piv)).astype(BF16)
        kd = (k[:rs.start] * jnp.exp2(b_piv - b[:rs.start])).astype(BF16)
        a_row = jnp.concatenate([_dot_nt(qd, kd).astype(BF16), ab[rs]], axis=1)
        o_parts.append(_dot(a_row, vb[:rs.stop]))
    o = jnp.concatenate(o_parts, axis=0) if len(o_parts) > 1 else o_parts[0]
    q_in = (q * jnp.exp2(b)).astype(BF16)
    b_last = b[chunk - 1:chunk, :]
    kd = (k * jnp.exp2(b_last - b)).astype(BF16)
    return o, q_in, _dot_tn(vb, kd), jnp.exp2(b_last)


def _hgrn_chunk_finish(o_local, q_in, st, ng, zg):
    o = o_local + _dot_nt(q_in, st.astype(BF16))
    on = o * lax.rsqrt(jnp.mean(o * o, axis=-1, keepdims=True) + EPS) * ng
    return on * _silu(zg)


def _hgrn_sample_kernel(zh_ref, lb_ref, ng_ref, lev_ref, sel_ref, tri_ref, s0_ref, o_ref, sf_ref,
                        *, nb, layer, t_valid):
    t_pad = zh_ref.shape[1]
    hd = o_ref.shape[2]
    dk = hd // HG_HEADS
    lev = lev_ref[...]
    sel = sel_ref[...]
    tri = tri_ref[...]
    ng = ng_ref[...]
    live = lax.broadcasted_iota(jnp.int32, (t_pad, dk), 0) < t_valid
    lbs = [_hgrn_lower_bound(lb_ref[:, h * dk:(h + 1) * dk], layer) for h in range(HG_HEADS)]

    def one_batch(bi, carry):
        part = lambda h, p: zh_ref[bi, :, (p * HG_HEADS + h) * dk:(p * HG_HEADS + h + 1) * dk]
        heads = range(HG_HEADS)
        gates = [_hgrn_chunk_gates(part(h, 0), part(h, 1), part(h, 2), lbs[h], tri, live) for h in heads]
        states = [s0_ref[bi, h].T for h in heads]
        local = [_hgrn_chunk_local(*gates[h], lev, sel) for h in heads]
        for h in heads:
            o_local, q_in, s_own, d_all = local[h]
            o_ref[bi, :, h * dk:(h + 1) * dk] = _hgrn_chunk_finish(o_local, q_in, states[h], ng, part(h, 3))
            sf_ref[bi, h] = (states[h] * d_all + s_own).T
        return carry

    lax.fori_loop(0, nb, one_batch, 0, unroll=2)


def _hgrn_sample(zh, lb_raw, norm_g, s0, layer, tables, *, nb, t_valid):
    b, t_pad, w4 = zh.shape
    hd = w4 // 4
    dk = hd // HG_HEADS
    depth = lb_raw.shape[0]
    lev, sel, tri = tables
    return pl.pallas_call(
        functools.partial(_hgrn_sample_kernel, nb=nb, layer=layer, t_valid=t_valid),
        grid=(b // nb,),
        in_specs=[pl.BlockSpec((nb, t_pad, w4), lambda i: (i, 0, 0)),
                  pl.BlockSpec((depth, hd), lambda i: (0, 0)),
                  pl.BlockSpec((None, 1, dk), lambda i: (layer, 0, 0)),
                  _const_spec(lev.shape, lambda i: (0, 0)),
                  _const_spec(sel.shape, lambda i: (0, 0)),
                  _const_spec(tri.shape, lambda i: (0, 0)),
                  pl.BlockSpec((None, nb, HG_HEADS, dk, dk), lambda i: (layer, i, 0, 0, 0))],
        out_specs=[pl.BlockSpec((nb, t_pad, hd), lambda i: (i, 0, 0)),
                   pl.BlockSpec((nb, HG_HEADS, dk, dk), lambda i: (i, 0, 0, 0))],
        out_shape=[jax.ShapeDtypeStruct((b, t_pad, hd), F32),
                   jax.ShapeDtypeStruct((b, HG_HEADS, dk, dk), F32)],
        compiler_params=_params(("parallel",)),
        name="hgrn_sample",
    )(zh, lb_raw, norm_g, lev, sel, tri, s0)


def _proj_mix_kernel(x_ref, ln_ref, wi_ref, dww_ref, dwb_ref, lng_ref, lnb_ref, lb_ref, ng_ref,
                     lev_ref, sel_ref, tri_ref,
                     yc_ref, q_ref, k_ref, v_ref, yh_ref, tail_ref, sf_ref, kt_ref, vt_ref,
                     ext_ref, st_ref, *, layer, chunk):
    tstep = pl.program_id(1)
    tt = x_ref.shape[0]
    conv_c = yc_ref.shape[1]
    att = q_ref.shape[1]
    hd = yh_ref.shape[1]
    dk = hd // HG_HEADS
    n_chunks = tt // chunk

    @pl.when(tstep == 0)
    def _():
        ext_ref[0, 0:CONV_HALO, :] = jnp.zeros((CONV_HALO, conv_c), F32)
        ext_ref[0, CONV_HALO + tt:CONV_HALO + tt + SUBLANES, :] = jnp.zeros((SUBLANES, conv_c), F32)
        st_ref[...] = jnp.zeros_like(st_ref)

    h = _rms(x_ref[...], ln_ref[...]).astype(BF16)
    c0 = 2 * conv_c
    zc = _dot(h, wi_ref[:, 0:c0])
    zh = _dot(h, wi_ref[:, c0 + 3 * att:c0 + 3 * att + 4 * hd])
    yc, u = _conv_tile(zc, ext_ref, 0, dww_ref, dwb_ref[...], lng_ref[...], lnb_ref[...])
    yc_ref[...] = yc.astype(yc_ref.dtype)
    tail_ref[...] = u[tt - CONV_HALO:, :]
    for i, (ref, t_ref) in enumerate(((q_ref, None), (k_ref, kt_ref), (v_ref, vt_ref))):
        z = _dot(h, wi_ref[:, c0 + i * att:c0 + (i + 1) * att])
        ref[...] = z
        if t_ref is not None:
            t_ref[...] = z.T

    lev = lev_ref[...]
    sel = sel_ref[...]
    tri = tri_ref[...]
    ng = ng_ref[...]
    lbs = [_hgrn_lower_bound(lb_ref[:, hh * dk:(hh + 1) * dk], layer) for hh in range(HG_HEADS)]

    def part(u_idx, p):
        hh, c = divmod(u_idx, n_chunks)
        return zh[c * chunk:(c + 1) * chunk, (p * HG_HEADS + hh) * dk:(p * HG_HEADS + hh + 1) * dk]

    def gates(u_idx):
        return _hgrn_chunk_gates(part(u_idx, 0), part(u_idx, 1), part(u_idx, 2), lbs[u_idx // n_chunks], tri, None)

    ahead = gates(0)
    st = None
    for u_idx in range(HG_HEADS * n_chunks):
        hh, c = divmod(u_idx, n_chunks)
        cur = ahead
        if u_idx + 1 < HG_HEADS * n_chunks:
            ahead = gates(u_idx + 1)
        if c == 0:
            st = st_ref[hh]
        o_local, q_in, s_own, d_all = _hgrn_chunk_local(*cur, lev, sel)
        o = _hgrn_chunk_finish(o_local, q_in, st, ng, part(u_idx, 3))
        yh_ref[c * chunk:(c + 1) * chunk, hh * dk:(hh + 1) * dk] = o.astype(yh_ref.dtype)
        st = st * d_all + s_own
        if c == n_chunks - 1:
            st_ref[hh] = st

    @pl.when(tstep == pl.num_programs(1) - 1)
    def _():
        for hh in range(HG_HEADS):
            sf_ref[hh] = st_ref[hh].T


def _proj_mix(x, layer, ln, wi, dww, dwb, lng, lnb, lb_raw, norm_g, tables, *, batch, tm, chunk, widths, keep):
    n, d = x.shape
    conv_c = widths[0] // 2
    att = widths[1]
    hd = widths[4] // 4
    dk = hd // HG_HEADS
    depth = lb_raw.shape[0]
    nt = n // batch // tm
    lev, sel, tri = tables
    row = lambda w: pl.BlockSpec((tm, w), lambda b, j: (b * nt + j, 0))
    cvec = pl.BlockSpec((None, 1, conv_c), lambda b, j: (layer, 0, 0))
    const2 = lambda a: _const_spec(a.shape, lambda b, j: (0, 0))
    first_kept = nt - keep // tm
    window = pl.BlockSpec((None, att, tm), lambda b, j: (b, 0, jnp.maximum(j - first_kept, 0)))
    return pl.pallas_call(
        functools.partial(_proj_mix_kernel, layer=layer, chunk=chunk),
        grid=(batch, nt),
        in_specs=[row(d), pl.BlockSpec((None, 1, d), lambda b, j: (layer, 0, 0)),
                  _const_spec(wi.shape, lambda b, j: (0, 0)),
                  pl.BlockSpec((None, CONV_HALO, conv_c), lambda b, j: (layer, 0, 0)),
                  cvec, cvec, cvec,
                  pl.BlockSpec((depth, hd), lambda b, j: (0, 0)),
                  pl.BlockSpec((None, 1, dk), lambda b, j: (layer, 0, 0)),
                  const2(lev), const2(sel), const2(tri)],
        out_specs=[row(conv_c), row(att), row(att), row(att), row(hd),
                   pl.BlockSpec((None, CONV_HALO, conv_c), lambda b, j: (b, 0, 0)),
                   pl.BlockSpec((None, HG_HEADS, dk, dk), lambda b, j: (b, 0, 0, 0)),
                   window, window],
        out_shape=[jax.ShapeDtypeStruct((n, conv_c), BF16),
                   jax.ShapeDtypeStruct((n, att), F32),
                   jax.ShapeDtypeStruct((n, att), F32),
                   jax.ShapeDtypeStruct((n, att), F32),
                   jax.ShapeDtypeStruct((n, hd), BF16),
                   jax.ShapeDtypeStruct((batch, CONV_HALO, conv_c), F32),
                   jax.ShapeDtypeStruct((batch, HG_HEADS, dk, dk), F32),
                   jax.ShapeDtypeStruct((batch, att, keep), F32),
                   jax.ShapeDtypeStruct((batch, att, keep), F32)],
        scratch_shapes=[pltpu.VMEM((1, CONV_HALO + tm + SUBLANES, conv_c), F32),
                        pltpu.VMEM((HG_HEADS, dk, dk), F32)],
        compiler_params=_params(("parallel", "arbitrary")),
        name="proj_mix",
    )(x, ln, wi, dww, dwb, lng, lnb, lb_raw, norm_g, lev, sel, tri)


def _pad_rows(x, rows):
    return jnp.pad(x, ((0, 0), (0, rows - x.shape[1]), (0, 0)))


def _key_minor(cache):
    depth, b, n_past, h, hd = cache.shape
    return jnp.transpose(cache, (0, 1, 3, 4, 2)).reshape(depth, b, h * hd, n_past)


def kernel(x_prompt, x_sample, state_conv, cache_k_win, cache_v_win, state_hgrn, ln_ffn1, w_ffn1_gate, w_ffn1_up, w_ffn1_down, ln_mix, w_in, conv_dw_w, conv_dw_b, conv_ln_g, conv_ln_b, hg_lower_bounds, hg_norm_g, w_out, ln_ffn2, w_ffn2_gate, w_ffn2_up, w_ffn2_down, ln_final):
    bp, seq, d = x_prompt.shape
    bs, t_new, _ = x_sample.shape
    depth = w_in.shape[0]
    conv_dim = conv_dw_w.shape[2]
    att_dim = ATT_HEADS * ATT_HDIM
    hg_dim = hg_lower_bounds.shape[1]
    widths = (2 * conv_dim, att_dim, att_dim, att_dim, 4 * hg_dim)
    n_past = cache_k_win.shape[2]
    t_pad = SUBLANES
    keep = min(ATT_SPAN, seq)

    ffn1_f32 = (w_ffn1_gate, w_ffn1_up, w_ffn1_down)
    late_f32 = (w_in, w_out, w_ffn2_gate, w_ffn2_up, w_ffn2_down)
    ffn1_w = tuple(w[0].astype(BF16) for w in ffn1_f32)
    vec3 = lambda a: a.reshape(depth, 1, a.shape[-1])
    ln1, lnm, ln2 = vec3(ln_ffn1), vec3(ln_mix), vec3(ln_ffn2)
    dwb, clg, clb, hgn = vec3(conv_dw_b), vec3(conv_ln_g), vec3(conv_ln_b), vec3(hg_norm_g)
    lnf = ln_final.reshape(1, d)
    dww = jnp.pad(conv_dw_w, ((0, 0), (0, CONV_HALO - CONV_WIDTH), (0, 0)))
    cache_k, cache_v = _key_minor(cache_k_win), _key_minor(cache_v_win)
    halo_s = jnp.pad(state_conv, ((0, 0), (0, 0), (CONV_HALO - (CONV_WIDTH - 1), 0), (0, 0)))

    bias_p = jnp.asarray(_prompt_bias_tables())
    bias_c, bias_n = (jnp.asarray(a) for a in _sample_bias_tables(n_past, t_new, t_pad))
    chunk_p = 128
    tabs_p = tuple(jnp.asarray(a, dt) for a, dt in zip(_hgrn_tables(chunk_p), (jnp.int32, BF16, BF16)))
    tabs_s = tuple(jnp.asarray(a, dt) for a, dt in zip(_hgrn_tables(t_pad), (jnp.int32, BF16, BF16)))

    hp = x_prompt.reshape(bp * seq, d)
    hs = x_sample.reshape(bs * t_new, d)
    outs = [[] for _ in range(8)]
    for l in range(depth):
        final = l == depth - 1
        x1, x1s, (wi, wo, *ffn2_w) = _ffn_half_step(hp, hs, l, ln1, *ffn1_w, [(w, l) for w in late_f32],
                                                    tm=512, f_chunk=1408)
        yc, q, k, v, yh, tail, sp, kt, vt = _proj_mix(x1, l, lnm, wi, dww, dwb, clg, clb, hg_lower_bounds, hgn,
                                                      tabs_p, batch=bp, tm=512, chunk=chunk_p, widths=widths,
                                                      keep=keep)
        seq3 = lambda a: a.reshape(bp, seq, a.shape[-1])
        ya = _attn_prompt(seq3(q), seq3(k), seq3(v), bias_p, span=ATT_SPAN)
        window = lambda a: a.reshape(bp, ATT_HEADS, ATT_HDIM, keep).transpose(0, 3, 1, 2)
        outs[0].append(tail[:, -(CONV_WIDTH - 1):])
        outs[2].append(window(kt))
        outs[3].append(window(vt))
        outs[6].append(sp)

        zcs, qs, ks, vs, zhs = _proj_in(x1s, l, lnm, wi, tm=bs * t_new, widths=widths)
        new3 = lambda a: _pad_rows(a.reshape(bs, t_new, a.shape[-1]), t_pad)
        ycs, us = _conv_group(new3(zcs), halo_s[l], l, dww, dwb, clg, clb, nb=bs, tt=t_pad, out_dtype=F32)
        yas = _attn_sample(new3(qs), new3(ks), new3(vs), cache_k, cache_v, l, bias_c, bias_n, nb=2)
        yhs, ss = _hgrn_sample(new3(zhs), hg_lower_bounds, hgn, state_hgrn, l, tabs_s, nb=8, t_valid=t_new)
        unpad = lambda a: a[:, :t_new].reshape(bs * t_new, a.shape[-1])
        nxt = [(w, l + 1) for w in ffn1_f32] if not final else []
        hp, hs, ffn1_w = _dense_out((x1, yc, ya.reshape(bp * seq, att_dim), yh),
                                    (x1s, unpad(ycs), unpad(yas), unpad(yhs)),
                                    l, wo, ln2, *ffn2_w, lnf, nxt, tm=512, f_chunk=1408, final=final)
        outs[1].append(jnp.concatenate([state_conv[l][:, t_new:], us[:, :t_new]], axis=1))
        outs[4].append(ks.reshape(bs, t_new, ATT_HEADS, ATT_HDIM))
        outs[5].append(vs.reshape(bs, t_new, ATT_HEADS, ATT_HDIM))
        outs[7].append(ss)

    y_prompt = hp.reshape(bp, seq, d)
    y_sample = hs.reshape(bs, t_new, d)
    st = [jnp.stack(o) for o in outs]
    return (y_prompt, y_sample, st[0], st[1], st[2], st[3], st[4], st[5], st[6], st[7])
```

```python
import functools
import math

import numpy as np
import jax
import jax.numpy as jnp
from jax import lax
from jax.experimental import pallas as pl
from jax.experimental.pallas import tpu as pltpu

F32 = jnp.float32
BF16 = jnp.bfloat16

EPS = 1e-6
NEG_BIG = -1e30
CONV_WIDTH = 31
CONV_HALO = 32
ATT_HEADS = 4
ATT_HDIM = 64
ATT_BLK = 128
DILATIONS = (1, 4, 16)
ATT_SPAN = 2048
ATT_UNROLL = 16
HG_HEADS = 4
LANES = 128
SUBLANES = 8
BF16_ROWS = 16
VMEM_LIMIT = 56 * 1024 * 1024


def _params(sem, vmem=VMEM_LIMIT):
    return pltpu.CompilerParams(dimension_semantics=sem, vmem_limit_bytes=vmem)


def _const_spec(shape, index):
    return pl.BlockSpec(shape, index, pipeline_mode=pl.Buffered(1))


def _rms(x, g):
    ms = jnp.mean(x * x, axis=-1, keepdims=True)
    return x * lax.rsqrt(ms + EPS) * g


def _silu(x):
    return x * jax.nn.sigmoid(x)


def _dot(a, b):
    return jnp.dot(a, b, preferred_element_type=F32)


def _dot_nt(a, b):
    return lax.dot_general(a, b, (((1,), (1,)), ((), ())), preferred_element_type=F32)


def _dot_tn(a, b):
    return lax.dot_general(a, b, (((0,), (0,)), ((), ())), preferred_element_type=F32)


def _swiglu(h, wg_ref, wu_ref, wd_ref, f_chunk):
    ffn = wg_ref.shape[1]
    acc = None
    for c0 in range(0, ffn, f_chunk):
        g = _dot(h, wg_ref[:, c0:c0 + f_chunk])
        u = _dot(h, wu_ref[:, c0:c0 + f_chunk])
        a = (_silu(g) * u).astype(BF16)
        y = _dot(a, wd_ref[c0:c0 + f_chunk, :])
        acc = y if acc is None else acc + y
    return acc


def _main_then_extra(n_main, main, extra):
    i = pl.program_id(0)
    pl.when(i < n_main)(main)
    pl.when(i == n_main)(extra)


def _cast_specs(cast, n_steps):
    in_specs, out_specs, shapes = [], [], []
    for w, layer in cast:
        _, r, c = w.shape
        rows = -(-r // n_steps)
        rows = -(-rows // BF16_ROWS) * BF16_ROWS
        while r % rows:
            rows += BF16_ROWS
        last = r // rows - 1
        in_specs.append(pl.BlockSpec((None, rows, c),
                                     lambda i, last=last, layer=layer: (layer, jnp.minimum(i, last), 0)))
        out_specs.append(pl.BlockSpec((rows, c), lambda i, last=last: (jnp.minimum(i, last), 0)))
        shapes.append(jax.ShapeDtypeStruct((r, c), BF16))
    return in_specs, out_specs, shapes


def _cast_slabs(src_refs, dst_refs):
    for src, dst in zip(src_refs, dst_refs):
        dst[...] = src[...].astype(dst.dtype)


def _ffn_kernel(*refs, f_chunk, n_main, n_cast):
    xp_ref, xs_ref, ln_ref, wg_ref, wu_ref, wd_ref = refs[:6]
    cast_src = refs[6:6 + n_cast]
    op_ref, os_ref = refs[6 + n_cast:8 + n_cast]
    cast_dst = refs[8 + n_cast:]

    def run(x_ref, o_ref):
        _cast_slabs(cast_src, cast_dst)
        x = x_ref[...]
        h = _rms(x, ln_ref[...]).astype(BF16)
        o_ref[...] = x + 0.5 * _swiglu(h, wg_ref, wu_ref, wd_ref, f_chunk)

    _main_then_extra(n_main, lambda: run(xp_ref, op_ref), lambda: run(xs_ref, os_ref))


def _ffn_half_step(xp, xs, layer, ln, wg, wu, wd, cast, *, tm, f_chunk):
    n, d = xp.shape
    ns = xs.shape[0]
    ffn = wg.shape[1]
    n_main = n // tm
    row = pl.BlockSpec((tm, d), lambda i: (jnp.minimum(i, n_main - 1), 0))
    extra = pl.BlockSpec((ns, d), lambda i: (0, 0))
    c_in, c_out, c_shapes = _cast_specs(cast, n_main + 1)
    res = pl.pallas_call(
        functools.partial(_ffn_kernel, f_chunk=f_chunk, n_main=n_main, n_cast=len(cast)),
        grid=(n_main + 1,),
        in_specs=[row, extra, pl.BlockSpec((None, 1, d), lambda i: (layer, 0, 0)),
                  _const_spec((d, ffn), lambda i: (0, 0)),
                  _const_spec((d, ffn), lambda i: (0, 0)),
                  _const_spec((ffn, d), lambda i: (0, 0))] + c_in,
        out_specs=[row, extra] + c_out,
        out_shape=[jax.ShapeDtypeStruct((n, d), F32), jax.ShapeDtypeStruct((ns, d), F32)] + c_shapes,
        compiler_params=_params(("arbitrary",)),
        name="ffn_half_step",
    )(xp, xs, ln, wg, wu, wd, *[w for w, _ in cast])
    return res[0], res[1], tuple(res[2:])


def _proj_in_kernel(x_ref, ln_ref, wi_ref, zc_ref, q_ref, k_ref, v_ref, zh_ref):
    h = _rms(x_ref[...], ln_ref[...]).astype(BF16)
    c0 = 0
    for ref in (zc_ref, q_ref, k_ref, v_ref, zh_ref):
        w = ref.shape[1]
        ref[...] = _dot(h, wi_ref[:, c0:c0 + w])
        c0 += w


def _proj_in(x, layer, ln, wi, *, tm, widths):
    n, d = x.shape
    row = lambda w: pl.BlockSpec((tm, w), lambda i: (i, 0))
    return pl.pallas_call(
        _proj_in_kernel,
        grid=(n // tm,),
        in_specs=[row(d), pl.BlockSpec((None, 1, d), lambda i: (layer, 0, 0)),
                  _const_spec(wi.shape, lambda i: (0, 0))],
        out_specs=[row(w) for w in widths],
        out_shape=[jax.ShapeDtypeStruct((n, w), F32) for w in widths],
        compiler_params=_params(("parallel",)),
        name="proj_in",
    )(x, ln, wi)


def _dense_out_kernel(*refs, f_chunk, final, n_main, n_cast):
    (xp_ref, ycp_ref, yap_ref, yhp_ref, xs_ref, ycs_ref, yas_ref, yhs_ref,
     wo_ref, ln2_ref, wg_ref, wu_ref, wd_ref, lnf_ref) = refs[:14]
    cast_src = refs[14:14 + n_cast]
    op_ref, os_ref = refs[14 + n_cast:16 + n_cast]
    cast_dst = refs[16 + n_cast:]

    def run(x_ref, yc_ref, ya_ref, yh_ref, o_ref):
        _cast_slabs(cast_src, cast_dst)
        y = jnp.concatenate([r[...].astype(BF16) for r in (yc_ref, ya_ref, yh_ref)], axis=-1)
        x2 = x_ref[...] + _dot(y, wo_ref[...])
        h = _rms(x2, ln2_ref[...]).astype(BF16)
        x3 = x2 + 0.5 * _swiglu(h, wg_ref, wu_ref, wd_ref, f_chunk)
        if final:
            x3 = _rms(x3, lnf_ref[...])
        o_ref[...] = x3

    _main_then_extra(n_main, lambda: run(xp_ref, ycp_ref, yap_ref, yhp_ref, op_ref),
                     lambda: run(xs_ref, ycs_ref, yas_ref, yhs_ref, os_ref))


def _dense_out(prompt, sample, layer, wo, ln2, wg, wu, wd, lnf, cast, *, tm, f_chunk, final):
    n, d = prompt[0].shape
    ns = sample[0].shape[0]
    ffn = wg.shape[1]
    n_main = n // tm
    row = lambda a: pl.BlockSpec((tm, a.shape[1]), lambda i: (jnp.minimum(i, n_main - 1), 0))
    extra = lambda a: pl.BlockSpec((ns, a.shape[1]), lambda i: (0, 0))
    vec = pl.BlockSpec((None, 1, d), lambda i: (layer, 0, 0))
    c_in, c_out, c_shapes = _cast_specs(cast, n_main + 1)
    res = pl.pallas_call(
        functools.partial(_dense_out_kernel, f_chunk=f_chunk, final=final, n_main=n_main, n_cast=len(cast)),
        grid=(n_main + 1,),
        in_specs=[row(a) for a in prompt] + [extra(a) for a in sample] + [
                  _const_spec((d, d), lambda i: (0, 0)),
                  vec,
                  _const_spec((d, ffn), lambda i: (0, 0)),
                  _const_spec((d, ffn), lambda i: (0, 0)),
                  _const_spec((ffn, d), lambda i: (0, 0)),
                  pl.BlockSpec((1, d), lambda i: (0, 0))] + c_in,
        out_specs=[row(prompt[0]), extra(sample[0])] + c_out,
        out_shape=[jax.ShapeDtypeStruct((n, d), F32), jax.ShapeDtypeStruct((ns, d), F32)] + c_shapes,
        compiler_params=_params(("arbitrary",)),
        name="dense_out",
    )(*prompt, *sample, wo, ln2, wg, wu, wd, lnf, *[w for w, _ in cast])
    return res[0], res[1], tuple(res[2:])


def _conv_tile(z, ext_ref, b, dww_ref, dwb, lng, lnb):
    tt, c2 = z.shape
    c = c2 // 2
    lo = CONV_HALO - (CONV_WIDTH - 1)
    u = z[:, :c] * jax.nn.sigmoid(z[:, c:])
    ext_ref[b, CONV_HALO:CONV_HALO + tt, :] = u
    y = None
    for r in range(SUBLANES):
        z_r = None
        for o in range(r, CONV_HALO + 1, SUBLANES):
            if o < lo:
                continue
            term = ext_ref[b, o - r:o - r + tt + SUBLANES, :] * dww_ref[o - lo:o - lo + 1, :]
            z_r = term if z_r is None else z_r + term
        part = z_r[r:r + tt, :]
        y = part if y is None else y + part
    y = y + dwb
    yc = y - jnp.mean(y, axis=-1, keepdims=True)
    yn = yc * lax.rsqrt(jnp.mean(yc * yc, axis=-1, keepdims=True) + EPS)
    carry = ext_ref[b, tt:tt + CONV_HALO, :]
    ext_ref[b, 0:CONV_HALO, :] = carry
    return _silu(yn * lng + lnb), u


def _conv_kernel(zc_ref, halo_ref, dww_ref, dwb_ref, lng_ref, lnb_ref, y_ref, tail_ref, ext_ref, *, nb, tt, tail):
    c = y_ref.shape[-1]

    def one_batch(b):
        @pl.when(pl.program_id(1) == 0)
        def _():
            ext_ref[b, 0:CONV_HALO, :] = halo_ref[b]
            ext_ref[b, CONV_HALO + tt:CONV_HALO + tt + SUBLANES, :] = jnp.zeros((SUBLANES, c), F32)

        y, u = _conv_tile(zc_ref[b], ext_ref, b, dww_ref, dwb_ref[...], lng_ref[...], lnb_ref[...])
        y_ref[b] = y.astype(y_ref.dtype)
        tail_ref[b] = u[tt - tail:, :]

    if nb == 1:
        one_batch(0)
    else:
        def body(b, carry):
            one_batch(b)
            return carry
        lax.fori_loop(0, nb, body, 0)


def _conv_group(zc, halo, layer, dww, dwb, lng, lnb, *, nb, tt, out_dtype):
    b, t, c2 = zc.shape
    c = c2 // 2
    tail = min(CONV_HALO, tt)
    vec = pl.BlockSpec((None, 1, c), lambda i, j: (layer, 0, 0))
    return pl.pallas_call(
        functools.partial(_conv_kernel, nb=nb, tt=tt, tail=tail),
        grid=(b // nb, t // tt),
        in_specs=[pl.BlockSpec((nb, tt, c2), lambda i, j: (i, j, 0)),
                  pl.BlockSpec((nb, CONV_HALO, c), lambda i, j: (i, 0, 0)),
                  pl.BlockSpec((None, CONV_HALO, c), lambda i, j: (layer, 0, 0)),
                  vec, vec, vec],
        out_specs=[pl.BlockSpec((nb, tt, c), lambda i, j: (i, j, 0)),
                   pl.BlockSpec((nb, tail, c), lambda i, j: (i, 0, 0))],
        out_shape=[jax.ShapeDtypeStruct((b, t, c), out_dtype),
                   jax.ShapeDtypeStruct((b, tail, c), F32)],
        scratch_shapes=[pltpu.VMEM((nb, CONV_HALO + tt + SUBLANES, c), F32)],
        compiler_params=_params(("parallel", "arbitrary")),
        name="conv_group",
    )(zc, halo, dww, dwb, lng, lnb)


def _alibi_slopes():
    return [2.0 ** (-8.0 * (h + 1) / ATT_HEADS) for h in range(ATT_HEADS)]


def _prompt_bias_tables():
    qi = np.arange(ATT_BLK)[:, None]
    ki = np.arange(2 * ATT_BLK)[None, :]
    j = qi - ki + ATT_BLK
    tabs = []
    for dil in DILATIONS:
        for first in (False, True):
            valid = (j >= 0) & (j <= ATT_BLK) & ((not first) | (ki >= ATT_BLK))
            rows = []
            for s in _alibi_slopes():
                bias = (-np.float32(s)) * (j * dil).astype(np.float32)
                rows.append(np.where(valid, bias, np.float32(NEG_BIG)).astype(np.float32))
            tabs.append(np.concatenate(rows, axis=0))
    return np.stack(tabs, axis=0)


def _head_pair_split(x, lo_mask):
    zero = jnp.zeros_like(x)
    return jnp.concatenate([jnp.where(lo_mask, x, zero), jnp.where(lo_mask, zero, x)], axis=0)


def _attn_prompt_kernel(bias_ref, q_ref, kp_ref, kc_ref, vp_ref, vc_ref, o_ref,
                        qs_ref, ks_ref, vs_ref, os_ref, ls_ref, *, span):
    j = pl.program_id(1)
    nslab = qs_ref.shape[0]
    scale = ATT_HDIM ** -0.5
    for sl in range(nslab):
        cols = slice(sl * LANES, (sl + 1) * LANES)
        qs_ref[sl] = q_ref[:, cols] * scale
        ks_ref[sl, 0:span, :] = kp_ref[:, cols]
        ks_ref[sl, span:2 * span, :] = kc_ref[:, cols]
        vs_ref[sl, 0:span, :] = vp_ref[:, cols]
        vs_ref[sl, span:2 * span, :] = vc_ref[:, cols]

    lane = lax.broadcasted_iota(jnp.int32, (ATT_BLK, LANES), 1)
    lo_mask = lane < ATT_HDIM
    n_blocks = span // ATT_BLK

    order = sorted(range(len(DILATIONS)), key=lambda b: -DILATIONS[b])
    for pos, br in enumerate(order):
        dil = DILATIONS[br]

        def rows(start, size, dil=dil):
            if dil == 1:
                return pl.ds(start if isinstance(start, int) else pl.multiple_of(start, ATT_BLK), size)
            return pl.ds(start, size, stride=dil)

        def scores(i, dil=dil, rows=rows):
            q0 = (i // dil) * (ATT_BLK * dil) + i % dil
            k0 = span + q0 - ATT_BLK * dil
            s_parts, v_parts = [], []
            for sl in range(nslab):
                ql = qs_ref[sl, rows(q0, ATT_BLK), :]
                kl = ks_ref[sl, rows(k0, 2 * ATT_BLK), :]
                v_parts.append(vs_ref[sl, rows(k0, 2 * ATT_BLK), :].astype(BF16))
                lhs = _head_pair_split(ql, lo_mask).astype(BF16)
                s_parts.append(_dot_nt(lhs, kl.astype(BF16)))
            return jnp.concatenate(s_parts, axis=0), jnp.concatenate(v_parts, axis=1)

        def finish(i, s, v, br=br, dil=dil, pos=pos, rows=rows):
            q0 = (i // dil) * (ATT_BLK * dil) + i % dil
            first = jnp.logical_and(j == 0, i // dil == 0).astype(jnp.int32)
            bias = bias_ref[2 * br + first]
            s = jnp.where(bias > 0.5 * NEG_BIG, s + bias, NEG_BIG)
            m = jnp.max(s, axis=-1, keepdims=True)
            p = jnp.exp(s - m)
            l = jnp.sum(p, axis=-1, keepdims=True)
            pv = _dot(p.astype(BF16), v)
            lse_all = m + jnp.log(l)
            for sl in range(nslab):
                cols = slice(sl * LANES, (sl + 1) * LANES)
                ra = slice((2 * sl) * ATT_BLK, (2 * sl + 1) * ATT_BLK)
                rb = slice((2 * sl + 1) * ATT_BLK, (2 * sl + 2) * ATT_BLK)
                pair = lambda x: jnp.where(lo_mask, jnp.broadcast_to(x[ra], (ATT_BLK, LANES)),
                                           jnp.broadcast_to(x[rb], (ATT_BLK, LANES)))
                o_new = jnp.where(lo_mask, pv[ra, cols], pv[rb, cols]) / pair(l)
                lse_new = pair(lse_all)
                out_rows = rows(q0, ATT_BLK)
                if pos == 0:
                    o_tot, lse_tot = o_new, lse_new
                else:
                    o_old = os_ref[sl, out_rows, :]
                    lse_old = ls_ref[sl, out_rows, :]
                    mx = jnp.maximum(lse_old, lse_new)
                    e_old = jnp.exp(lse_old - mx)
                    e_new = jnp.exp(lse_new - mx)
                    den = e_old + e_new
                    o_tot = (e_old * o_old + e_new * o_new) / den
                    lse_tot = mx + jnp.log(den)
                os_ref[sl, out_rows, :] = o_tot
                ls_ref[sl, out_rows, :] = lse_tot

        def trip(t, carry, scores=scores, finish=finish):
            base = t * ATT_UNROLL
            ahead = scores(base)
            for u in range(ATT_UNROLL):
                s, v = ahead
                if u + 1 < ATT_UNROLL:
                    ahead = scores(base + u + 1)
                finish(base + u, s, v)
            return carry

        if n_blocks == ATT_UNROLL:
            trip(0, 0)
        else:
            lax.fori_loop(0, n_blocks // ATT_UNROLL, trip, 0)

    o_ref[...] = jnp.concatenate([os_ref[sl] for sl in range(nslab)], axis=-1).astype(o_ref.dtype)


def _attn_prompt(q, k, v, bias_tabs, *, span):
    b, t, w = q.shape
    nslab = w // LANES
    cur = pl.BlockSpec((None, span, w), lambda i, j: (i, j, 0))
    prev = pl.BlockSpec((None, span, w), lambda i, j: (i, jnp.maximum(j - 1, 0), 0))
    slab = lambda rows: pltpu.VMEM((nslab, rows, LANES), F32)
    return pl.pallas_call(
        functools.partial(_attn_prompt_kernel, span=span),
        grid=(b, t // span),
        in_specs=[_const_spec(bias_tabs.shape, lambda i, j: (0, 0, 0)), cur, prev, cur, prev, cur],
        out_specs=cur,
        out_shape=jax.ShapeDtypeStruct((b, t, w), BF16),
        scratch_shapes=[slab(span), slab(2 * span), slab(2 * span), slab(span), slab(span)],
        compiler_params=_params(("parallel", "arbitrary")),
        name="attn_prompt",
    )(bias_tabs, q, k, k, v, v)


def _sample_bias_tables(n_past, t_new, t_pad):
    slopes = _alibi_slopes()
    n_keys = n_past + t_pad
    tabs = np.full((len(DILATIONS), ATT_HEADS * t_pad, n_keys), NEG_BIG, np.float32)
    for br, dil in enumerate(DILATIONS):
        for h, s in enumerate(slopes):
            for t in range(t_pad):
                tq = min(t, t_new - 1)
                for jj in range(ATT_BLK + 1):
                    idx = n_past + tq - jj * dil
                    if idx < 0:
                        continue
                    tabs[br, h * t_pad + t, idx] = -np.float32(s) * np.float32(jj * dil)
    return tabs[:, :, :n_past], tabs[:, :, n_past:]


def _attn_sample_kernel(bc_ref, bn_ref, q_ref, kn_ref, vn_ref, kc_ref, vc_ref, o_ref):
    for b in range(q_ref.shape[0]):
        _attn_sample_one(bc_ref, bn_ref, q_ref.at[b], kn_ref.at[b], vn_ref.at[b], kc_ref.at[b], vc_ref.at[b],
                         o_ref.at[b])


def _attn_sample_one(bc_ref, bn_ref, q_ref, kn_ref, vn_ref, kc_ref, vc_ref, o_ref):
    t_pad, w = q_ref.shape
    scale = ATT_HDIM ** -0.5
    q = q_ref[...] * scale
    lane = lax.broadcasted_iota(jnp.int32, (t_pad, w), 1)
    head_of_lane = lane // ATT_HDIM
    zero = jnp.zeros_like(q)
    lhs = jnp.concatenate([jnp.where(head_of_lane == h, q, zero) for h in range(ATT_HEADS)], axis=0).astype(BF16)
    s_c = _dot(lhs, kc_ref[...].astype(BF16))
    s_n = _dot_nt(lhs, kn_ref[...].astype(BF16))
    vc = vc_ref[...].astype(BF16)
    vn = vn_ref[...].astype(BF16)
    pcs, pns, ls, lses = [], [], [], []
    for br in range(len(DILATIONS)):
        bc = bc_ref[br]
        bn = bn_ref[br]
        sc = jnp.where(bc > 0.5 * NEG_BIG, s_c + bc, NEG_BIG)
        sn = jnp.where(bn > 0.5 * NEG_BIG, s_n + bn, NEG_BIG)
        m = jnp.maximum(jnp.max(sc, axis=-1, keepdims=True), jnp.max(sn, axis=-1, keepdims=True))
        pc = jnp.exp(sc - m)
        pn = jnp.exp(sn - m)
        l = jnp.sum(pc, axis=-1, keepdims=True) + jnp.sum(pn, axis=-1, keepdims=True)
        pcs.append(pc.astype(BF16))
        pns.append(pn.astype(BF16))
        ls.append(l)
        lses.append(m + jnp.log(l))
    nq = ATT_HEADS * t_pad
    pv = _dot_nt(jnp.concatenate(pcs, axis=0), vc) + _dot(jnp.concatenate(pns, axis=0), vn)
    outs = [pv[br * nq:(br + 1) * nq] / ls[br] for br in range(len(DILATIONS))]
    mx = functools.reduce(jnp.maximum, lses)
    es = [jnp.exp(x - mx) for x in lses]
    den = functools.reduce(lambda a, b: a + b, es)
    o = functools.reduce(lambda a, b: a + b, [e * x for e, x in zip(es, outs)]) / den
    res = zero
    for h in range(ATT_HEADS):
        res = jnp.where(head_of_lane == h, o[h * t_pad:(h + 1) * t_pad, :], res)
    o_ref[...] = res.astype(o_ref.dtype)


def _attn_sample(q, k, v, cache_k, cache_v, layer, bias_c, bias_n, *, nb):
    b, t_pad, w = q.shape
    n_past = cache_k.shape[3]
    new = pl.BlockSpec((nb, t_pad, w), lambda i: (i, 0, 0))
    cache = pl.BlockSpec((None, nb, w, n_past), lambda i: (layer, i, 0, 0))
    return pl.pallas_call(
        _attn_sample_kernel,
        grid=(b // nb,),
        in_specs=[_const_spec(bias_c.shape, lambda i: (0, 0, 0)),
                  _const_spec(bias_n.shape, lambda i: (0, 0, 0)),
                  new, new, new, cache, cache],
        out_specs=new,
        out_shape=jax.ShapeDtypeStruct((b, t_pad, w), F32),
        compiler_params=_params(("parallel",)),
        name="attn_sample",
    )(bias_c, bias_n, q, k, v, cache_k, cache_v)


def _hgrn_tables(chunk):
    sub = min(chunk, LANES)
    r = np.arange(sub)[:, None]
    c = np.arange(sub)[None, :]
    x = np.bitwise_xor(r, c)
    hb = np.zeros_like(x)
    for bit in range(1, 16):
        hb = np.where(x >> bit > 0, bit, hb)
    lev = np.where(c > r, -1, np.where(r // SUBLANES == c // SUBLANES, 0, hb)).astype(np.int32)
    lev = np.tile(lev, (chunk // sub, 1))
    sel = np.zeros((SUBLANES * LANES, sub), np.float32)
    for s in range(SUBLANES):
        sel[s * LANES:(s + 1) * LANES, s::SUBLANES] = 1.0
    rr = np.arange(chunk)
    tri = (rr[None, :] <= rr[:, None]).astype(np.float32)
    return lev, sel, tri


def _split3(x):
    hi = x.astype(BF16)
    r1 = x - hi.astype(F32)
    mid = r1.astype(BF16)
    lo = (r1 - mid.astype(F32)).astype(BF16)
    return hi, mid, lo


def _group_row(x, s):
    c, l = x.shape
    x3 = x.reshape(c // SUBLANES, SUBLANES, l)
    return jnp.broadcast_to(x3[:, s:s + 1, :], x3.shape).reshape(c, l)


def _block_last(x, m):
    c, l = x.shape
    x3 = x.reshape(c // m, m, l)
    return jnp.broadcast_to(x3[:, m - 1:m, :], x3.shape).reshape(c, l)


def _hgrn_lower_bound(raw, layer):
    e = jnp.exp(raw - jnp.max(raw, axis=0, keepdims=True))
    sm = e / jnp.sum(e, axis=0, keepdims=True)
    lb = jnp.zeros_like(sm[0:1])
    for i in range(1, layer + 1):
        lb = lb + sm[i:i + 1]
    return lb


def _hgrn_chunk_gates(zq, zf, zi, lb, tri, live):
    dk = zq.shape[1]
    q = _silu(zq)
    f = lb + (1.0 - lb) * jax.nn.sigmoid(zf)
    g = jnp.log2(f)
    k = 1.0 - f
    if live is not None:
        g = jnp.where(live, g, 0.0)
        k = jnp.where(live, k, 0.0)
    b3 = _dot(tri, jnp.concatenate(_split3(g), axis=1))
    b = b3[:, :dk] + b3[:, dk:2 * dk] + b3[:, 2 * dk:]
    return q, k, b, zi.astype(BF16)


def _hgrn_chunk_local(q, k, b, vb, lev, sel):
    chunk, dk = q.shape
    sub = lev.shape[1]

    vals = []
    for s in range(SUBLANES):
        decay = jnp.exp2(jnp.minimum(b - _group_row(b, s), 0.0))
        vals.append((q * decay * _group_row(k, s)).astype(BF16))
    a = jnp.where(lev == 0, _dot(jnp.concatenate(vals, axis=1), sel), 0.0)
    subs = [slice(r0, r0 + sub) for r0 in range(0, chunk, sub)]
    for bit in range(3, int(math.log2(sub))):
        m = 1 << bit
        b_end = _block_last(b, m)
        b_prev = jnp.concatenate([jnp.zeros((m, dk), F32), b_end[:chunk - m]], axis=0)
        qd = (q * jnp.exp2(b - b_prev)).astype(BF16)
        kd = (k * jnp.exp2(b_end - b)).astype(BF16)
        a_l = jnp.concatenate([_dot_nt(qd[rs], kd[rs]) for rs in subs], axis=0)
        a = jnp.where(lev == bit, a_l, a)
    ab = a.astype(BF16)
    o_parts = []
    for i, rs in enumerate(subs):
        if i == 0:
            o_parts.append(_dot(ab[rs], vb[rs]))
            continue
        b_piv = b[rs.start - 1:rs.start, :]
        qd = (q[rs] * jnp.exp2(b[rs] - b_piv)).astype(BF16)
        kd = (k[:rs.start] * jnp.exp2(b_piv - b[:rs.start])).astype(BF16)
        a_row = jnp.concatenate([_dot_nt(qd, kd).astype(BF16), ab[rs]], axis=1)
        o_parts.append(_dot(a_row, vb[:rs.stop]))
    o = jnp.concatenate(o_parts, axis=0) if len(o_parts) > 1 else o_parts[0]
    q_in = (q * jnp.exp2(b)).astype(BF16)
    b_last = b[chunk - 1:chunk, :]
    kd = (k * jnp.exp2(b_last - b)).astype(BF16)
    return o, q_in, _dot_tn(vb, kd), jnp.exp2(b_last)


def _hgrn_chunk_finish(o_local, q_in, st, ng, zg):
    o = o_local + _dot_nt(q_in, st.astype(BF16))
    on = o * lax.rsqrt(jnp.mean(o * o, axis=-1, keepdims=True) + EPS) * ng
    return on * _silu(zg)


def _hgrn_sample_kernel(zh_ref, lb_ref, ng_ref, lev_ref, sel_ref, tri_ref, s0_ref, o_ref, sf_ref,
                        *, nb, layer, t_valid):
    t_pad = zh_ref.shape[1]
    hd = o_ref.shape[2]
    dk = hd // HG_HEADS
    lev = lev_ref[...]
    sel = sel_ref[...]
    tri = tri_ref[...]
    ng = ng_ref[...]
    live = lax.broadcasted_iota(jnp.int32, (t_pad, dk), 0) < t_valid
    lbs = [_hgrn_lower_bound(lb_ref[:, h * dk:(h + 1) * dk], layer) for h in range(HG_HEADS)]

    def one_batch(bi, carry):
        part = lambda h, p: zh_ref[bi, :, (p * HG_HEADS + h) * dk:(p * HG_HEADS + h + 1) * dk]
        heads = range(HG_HEADS)
        gates = [_hgrn_chunk_gates(part(h, 0), part(h, 1), part(h, 2), lbs[h], tri, live) for h in heads]
        states = [s0_ref[bi, h].T for h in heads]
        local = [_hgrn_chunk_local(*gates[h], lev, sel) for h in heads]
        for h in heads:
            o_local, q_in, s_own, d_all = local[h]
            o_ref[bi, :, h * dk:(h + 1) * dk] = _hgrn_chunk_finish(o_local, q_in, states[h], ng, part(h, 3))
            sf_ref[bi, h] = (states[h] * d_all + s_own).T
        return carry

    lax.fori_loop(0, nb, one_batch, 0, unroll=2)


def _hgrn_sample(zh, lb_raw, norm_g, s0, layer, tables, *, nb, t_valid):
    b, t_pad, w4 = zh.shape
    hd = w4 // 4
    dk = hd // HG_HEADS
    depth = lb_raw.shape[0]
    lev, sel, tri = tables
    return pl.pallas_call(
        functools.partial(_hgrn_sample_kernel, nb=nb, layer=layer, t_valid=t_valid),
        grid=(b // nb,),
        in_specs=[pl.BlockSpec((nb, t_pad, w4), lambda i: (i, 0, 0)),
                  pl.BlockSpec((depth, hd), lambda i: (0, 0)),
                  pl.BlockSpec((None, 1, dk), lambda i: (layer, 0, 0)),
                  _const_spec(lev.shape, lambda i: (0, 0)),
                  _const_spec(sel.shape, lambda i: (0, 0)),
                  _const_spec(tri.shape, lambda i: (0, 0)),
                  pl.BlockSpec((None, nb, HG_HEADS, dk, dk), lambda i: (layer, i, 0, 0, 0))],
        out_specs=[pl.BlockSpec((nb, t_pad, hd), lambda i: (i, 0, 0)),
                   pl.BlockSpec((nb, HG_HEADS, dk, dk), lambda i: (i, 0, 0, 0))],
        out_shape=[jax.ShapeDtypeStruct((b, t_pad, hd), F32),
                   jax.ShapeDtypeStruct((b, HG_HEADS, dk, dk), F32)],
        compiler_params=_params(("parallel",)),
        name="hgrn_sample",
    )(zh, lb_raw, norm_g, lev, sel, tri, s0)


def _proj_mix_kernel(x_ref, ln_ref, wi_ref, dww_ref, dwb_ref, lng_ref, lnb_ref, lb_ref, ng_ref,
                     lev_ref, sel_ref, tri_ref,
                     yc_ref, q_ref, k_ref, v_ref, yh_ref, tail_ref, sf_ref, kt_ref, vt_ref,
                     ext_ref, st_ref, *, layer, chunk):
    tstep = pl.program_id(1)
    tt = x_ref.shape[0]
    conv_c = yc_ref.shape[1]
    att = q_ref.shape[1]
    hd = yh_ref.shape[1]
    dk = hd // HG_HEADS
    n_chunks = tt // chunk

    @pl.when(tstep == 0)
    def _():
        ext_ref[0, 0:CONV_HALO, :] = jnp.zeros((CONV_HALO, conv_c), F32)
        ext_ref[0, CONV_HALO + tt:CONV_HALO + tt + SUBLANES, :] = jnp.zeros((SUBLANES, conv_c), F32)
        st_ref[...] = jnp.zeros_like(st_ref)

    h = _rms(x_ref[...], ln_ref[...]).astype(BF16)
    c0 = 2 * conv_c
    zc = _dot(h, wi_ref[:, 0:c0])
    zh = _dot(h, wi_ref[:, c0 + 3 * att:c0 + 3 * att + 4 * hd])
    yc, u = _conv_tile(zc, ext_ref, 0, dww_ref, dwb_ref[...], lng_ref[...], lnb_ref[...])
    yc_ref[...] = yc.astype(yc_ref.dtype)
    tail_ref[...] = u[tt - CONV_HALO:, :]
    for i, (ref, t_ref) in enumerate(((q_ref, None), (k_ref, kt_ref), (v_ref, vt_ref))):
        z = _dot(h, wi_ref[:, c0 + i * att:c0 + (i + 1) * att])
        ref[...] = z
        if t_ref is not None:
            t_ref[...] = z.T

    lev = lev_ref[...]
    sel = sel_ref[...]
    tri = tri_ref[...]
    ng = ng_ref[...]
    lbs = [_hgrn_lower_bound(lb_ref[:, hh * dk:(hh + 1) * dk], layer) for hh in range(HG_HEADS)]

    def part(u_idx, p):
        hh, c = divmod(u_idx, n_chunks)
        return zh[c * chunk:(c + 1) * chunk, (p * HG_HEADS + hh) * dk:(p * HG_HEADS + hh + 1) * dk]

    def gates(u_idx):
        return _hgrn_chunk_gates(part(u_idx, 0), part(u_idx, 1), part(u_idx, 2), lbs[u_idx // n_chunks], tri, None)

    ahead = gates(0)
    st = None
    for u_idx in range(HG_HEADS * n_chunks):
        hh, c = divmod(u_idx, n_chunks)
        cur = ahead
        if u_idx + 1 < HG_HEADS * n_chunks:
            ahead = gates(u_idx + 1)
        if c == 0:
            st = st_ref[hh]
        o_local, q_in, s_own, d_all = _hgrn_chunk_local(*cur, lev, sel)
        o = _hgrn_chunk_finish(o_local, q_in, st, ng, part(u_idx, 3))
        yh_ref[c * chunk:(c + 1) * chunk, hh * dk:(hh + 1) * dk] = o.astype(yh_ref.dtype)
        st = st * d_all + s_own
        if c == n_chunks - 1:
            st_ref[hh] = st

    @pl.when(tstep == pl.num_programs(1) - 1)
    def _():
        for hh in range(HG_HEADS):
            sf_ref[hh] = st_ref[hh].T


def _proj_mix(x, layer, ln, wi, dww, dwb, lng, lnb, lb_raw, norm_g, tables, *, batch, tm, chunk, widths, keep):
    n, d = x.shape
    conv_c = widths[0] // 2
    att = widths[1]
    hd = widths[4] // 4
    dk = hd // HG_HEADS
    depth = lb_raw.shape[0]
    nt = n // batch // tm
    lev, sel, tri = tables
    row = lambda w: pl.BlockSpec((tm, w), lambda b, j: (b * nt + j, 0))
    cvec = pl.BlockSpec((None, 1, conv_c), lambda b, j: (layer, 0, 0))
    const2 = lambda a: _const_spec(a.shape, lambda b, j: (0, 0))
    first_kept = nt - keep // tm
    window = pl.BlockSpec((None, att, tm), lambda b, j: (b, 0, jnp.maximum(j - first_kept, 0)))
    return pl.pallas_call(
        functools.partial(_proj_mix_kernel, layer=layer, chunk=chunk),
        grid=(batch, nt),
        in_specs=[row(d), pl.BlockSpec((None, 1, d), lambda b, j: (layer, 0, 0)),
                  _const_spec(wi.shape, lambda b, j: (0, 0)),
                  pl.BlockSpec((None, CONV_HALO, conv_c), lambda b, j: (layer, 0, 0)),
                  cvec, cvec, cvec,
                  pl.BlockSpec((depth, hd), lambda b, j: (0, 0)),
                  pl.BlockSpec((None, 1, dk), lambda b, j: (layer, 0, 0)),
                  const2(lev), const2(sel), const2(tri)],
        out_specs=[row(conv_c), row(att), row(att), row(att), row(hd),
                   pl.BlockSpec((None, CONV_HALO, conv_c), lambda b, j: (b, 0, 0)),
                   pl.BlockSpec((None, HG_HEADS, dk, dk), lambda b, j: (b, 0, 0, 0)),
                   window, window],
        out_shape=[jax.ShapeDtypeStruct((n, conv_c), BF16),
                   jax.ShapeDtypeStruct((n, att), F32),
                   jax.ShapeDtypeStruct((n, att), F32),
                   jax.ShapeDtypeStruct((n, att), F32),
                   jax.ShapeDtypeStruct((n, hd), BF16),
                   jax.ShapeDtypeStruct((batch, CONV_HALO, conv_c), F32),
                   jax.ShapeDtypeStruct((batch, HG_HEADS, dk, dk), F32),
                   jax.ShapeDtypeStruct((batch, att, keep), F32),
                   jax.ShapeDtypeStruct((batch, att, keep), F32)],
        scratch_shapes=[pltpu.VMEM((1, CONV_HALO + tm + SUBLANES, conv_c), F32),
                        pltpu.VMEM((HG_HEADS, dk, dk), F32)],
        compiler_params=_params(("parallel", "arbitrary")),
        name="proj_mix",
    )(x, ln, wi, dww, dwb, lng, lnb, lb_raw, norm_g, lev, sel, tri)


def _pad_rows(x, rows):
    return jnp.pad(x, ((0, 0), (0, rows - x.shape[1]), (0, 0)))


def _key_minor(cache):
    depth, b, n_past, h, hd = cache.shape
    return jnp.transpose(cache, (0, 1, 3, 4, 2)).reshape(depth, b, h * hd, n_past)


def kernel(x_prompt, x_sample, state_conv, cache_k_win, cache_v_win, state_hgrn, ln_ffn1, w_ffn1_gate, w_ffn1_up, w_ffn1_down, ln_mix, w_in, conv_dw_w, conv_dw_b, conv_ln_g, conv_ln_b, hg_lower_bounds, hg_norm_g, w_out, ln_ffn2, w_ffn2_gate, w_ffn2_up, w_ffn2_down, ln_final):
    bp, seq, d = x_prompt.shape
    bs, t_new, _ = x_sample.shape
    depth = w_in.shape[0]
    conv_dim = conv_dw_w.shape[2]
    att_dim = ATT_HEADS * ATT_HDIM
    hg_dim = hg_lower_bounds.shape[1]
    widths = (2 * conv_dim, att_dim, att_dim, att_dim, 4 * hg_dim)
    n_past = cache_k_win.shape[2]
    t_pad = SUBLANES
    keep = min(ATT_SPAN, seq)

    ffn1_f32 = (w_ffn1_gate, w_ffn1_up, w_ffn1_down)
    late_f32 = (w_in, w_out, w_ffn2_gate, w_ffn2_up, w_ffn2_down)
    ffn1_w = tuple(w[0].astype(BF16) for w in ffn1_f32)
    vec3 = lambda a: a.reshape(depth, 1, a.shape[-1])
    ln1, lnm, ln2 = vec3(ln_ffn1), vec3(ln_mix), vec3(ln_ffn2)
    dwb, clg, clb, hgn = vec3(conv_dw_b), vec3(conv_ln_g), vec3(conv_ln_b), vec3(hg_norm_g)
    lnf = ln_final.reshape(1, d)
    dww = jnp.pad(conv_dw_w, ((0, 0), (0, CONV_HALO - CONV_WIDTH), (0, 0)))
    cache_k, cache_v = _key_minor(cache_k_win), _key_minor(cache_v_win)
    halo_s = jnp.pad(state_conv, ((0, 0), (0, 0), (CONV_HALO - (CONV_WIDTH - 1), 0), (0, 0)))

    bias_p = jnp.asarray(_prompt_bias_tables())
    bias_c, bias_n = (jnp.asarray(a) for a in _sample_bias_tables(n_past, t_new, t_pad))
    chunk_p = 128
    tabs_p = tuple(jnp.asarray(a, dt) for a, dt in zip(_hgrn_tables(chunk_p), (jnp.int32, BF16, BF16)))
    tabs_s = tuple(jnp.asarray(a, dt) for a, dt in zip(_hgrn_tables(t_pad), (jnp.int32, BF16, BF16)))

    hp = x_prompt.reshape(bp * seq, d)
    hs = x_sample.reshape(bs * t_new, d)
    outs = [[] for _ in range(8)]
    for l in range(depth):
        final = l == depth - 1
        x1, x1s, (wi, wo, *ffn2_w) = _ffn_half_step(hp, hs, l, ln1, *ffn1_w, [(w, l) for w in late_f32],
                                                    tm=512, f_chunk=1408)
        yc, q, k, v, yh, tail, sp, kt, vt = _proj_mix(x1, l, lnm, wi, dww, dwb, clg, clb, hg_lower_bounds, hgn,
                                                      tabs_p, batch=bp, tm=1024, chunk=chunk_p, widths=widths,
                                                      keep=keep)
        seq3 = lambda a: a.reshape(bp, seq, a.shape[-1])
        ya = _attn_prompt(seq3(q), seq3(k), seq3(v), bias_p, span=ATT_SPAN)
        window = lambda a: a.reshape(bp, ATT_HEADS, ATT_HDIM, keep).transpose(0, 3, 1, 2)
        outs[0].append(tail[:, -(CONV_WIDTH - 1):])
        outs[2].append(window(kt))
        outs[3].append(window(vt))
        outs[6].append(sp)

        zcs, qs, ks, vs, zhs = _proj_in(x1s, l, lnm, wi, tm=bs * t_new, widths=widths)
        new3 = lambda a: _pad_rows(a.reshape(bs, t_new, a.shape[-1]), t_pad)
        ycs, us = _conv_group(new3(zcs), halo_s[l], l, dww, dwb, clg, clb, nb=bs, tt=t_pad, out_dtype=F32)
        yas = _attn_sample(new3(qs), new3(ks), new3(vs), cache_k, cache_v, l, bias_c, bias_n, nb=2)
        yhs, ss = _hgrn_sample(new3(zhs), hg_lower_bounds, hgn, state_hgrn, l, tabs_s, nb=8, t_valid=t_new)
        unpad = lambda a: a[:, :t_new].reshape(bs * t_new, a.shape[-1])
        nxt = [(w, l + 1) for w in ffn1_f32] if not final else []
        hp, hs, ffn1_w = _dense_out((x1, yc, ya.reshape(bp * seq, att_dim), yh),
                                    (x1s, unpad(ycs), unpad(yas), unpad(yhs)),
                                    l, wo, ln2, *ffn2_w, lnf, nxt, tm=512, f_chunk=1408, final=final)
        outs[1].append(jnp.concatenate([state_conv[l][:, t_new:], us[:, :t_new]], axis=1))
        outs[4].append(ks.reshape(bs, t_new, ATT_HEADS, ATT_HDIM))
        outs[5].append(vs.reshape(bs, t_new, ATT_HEADS, ATT_HDIM))
        outs[7].append(ss)

    y_prompt = hp.reshape(bp, seq, d)
    y_sample = hs.reshape(bs, t_new, d)
    st = [jnp.stack(o) for o in outs]
    return (y_prompt, y_sample, st[0], st[1], st[2], st[3], st[4], st[5], st[6], st[7])
```

```python
import functools
import math

import numpy as np
import jax
import jax.numpy as jnp
from jax import lax
from jax.experimental import pallas as pl
from jax.experimental.pallas import tpu as pltpu

F32 = jnp.float32
BF16 = jnp.bfloat16

EPS = 1e-6
NEG_BIG = -1e30
CONV_WIDTH = 31
CONV_HALO = 32
ATT_HEADS = 4
ATT_HDIM = 64
ATT_BLK = 128
DILATIONS = (1, 4, 16)
ATT_SPAN = 2048
ATT_UNROLL = 16
HG_HEADS = 4
LANES = 128
SUBLANES = 8
BF16_ROWS = 16
VMEM_LIMIT = 56 * 1024 * 1024


def _params(sem, vmem=VMEM_LIMIT):
    return pltpu.CompilerParams(dimension_semantics=sem, vmem_limit_bytes=vmem)


def _const_spec(shape, index):
    return pl.BlockSpec(shape, index, pipeline_mode=pl.Buffered(1))


def _rms(x, g):
    ms = jnp.mean(x * x, axis=-1, keepdims=True)
    return x * lax.rsqrt(ms + EPS) * g


def _silu(x):
    return x * jax.nn.sigmoid(x)


def _dot(a, b):
    return jnp.dot(a, b, preferred_element_type=F32)


def _dot_nt(a, b):
    return lax.dot_general(a, b, (((1,), (1,)), ((), ())), preferred_element_type=F32)


def _dot_tn(a, b):
    return lax.dot_general(a, b, (((0,), (0,)), ((), ())), preferred_element_type=F32)


def _swiglu(h, wg_ref, wu_ref, wd_ref, f_chunk):
    ffn = wg_ref.shape[1]
    acc = None
    for c0 in range(0, ffn, f_chunk):
        g = _dot(h, wg_ref[:, c0:c0 + f_chunk])
        u = _dot(h, wu_ref[:, c0:c0 + f_chunk])
        a = (_silu(g) * u).astype(BF16)
        y = _dot(a, wd_ref[c0:c0 + f_chunk, :])
        acc = y if acc is None else acc + y
    return acc


def _main_then_extra(n_main, main, extra):
    i = pl.program_id(0)
    pl.when(i < n_main)(main)
    pl.when(i == n_main)(extra)


def _cast_specs(cast, n_steps):
    in_specs, out_specs, shapes = [], [], []
    for w, layer in cast:
        _, r, c = w.shape
        rows = -(-r // n_steps)
        rows = -(-rows // BF16_ROWS) * BF16_ROWS
        while r % rows:
            rows += BF16_ROWS
        last = r // rows - 1
        in_specs.append(pl.BlockSpec((None, rows, c),
                                     lambda i, last=last, layer=layer: (layer, jnp.minimum(i, last), 0)))
        out_specs.append(pl.BlockSpec((rows, c), lambda i, last=last: (jnp.minimum(i, last), 0)))
        shapes.append(jax.ShapeDtypeStruct((r, c), BF16))
    return in_specs, out_specs, shapes


def _cast_slabs(src_refs, dst_refs):
    for src, dst in zip(src_refs, dst_refs):
        dst[...] = src[...].astype(dst.dtype)


def _ffn_kernel(*refs, f_chunk, n_main, n_cast):
    xp_ref, xs_ref, ln_ref, wg_ref, wu_ref, wd_ref = refs[:6]
    cast_src = refs[6:6 + n_cast]
    op_ref, os_ref = refs[6 + n_cast:8 + n_cast]
    cast_dst = refs[8 + n_cast:]

    def run(x_ref, o_ref):
        _cast_slabs(cast_src, cast_dst)
        x = x_ref[...]
        h = _rms(x, ln_ref[...]).astype(BF16)
        o_ref[...] = x + 0.5 * _swiglu(h, wg_ref, wu_ref, wd_ref, f_chunk)

    _main_then_extra(n_main, lambda: run(xp_ref, op_ref), lambda: run(xs_ref, os_ref))


def _ffn_half_step(xp, xs, layer, ln, wg, wu, wd, cast, *, tm, f_chunk):
    n, d = xp.shape
    ns = xs.shape[0]
    ffn = wg.shape[1]
    n_main = n // tm
    row = pl.BlockSpec((tm, d), lambda i: (jnp.minimum(i, n_main - 1), 0))
    extra = pl.BlockSpec((ns, d), lambda i: (0, 0))
    c_in, c_out, c_shapes = _cast_specs(cast, n_main + 1)
    res = pl.pallas_call(
        functools.partial(_ffn_kernel, f_chunk=f_chunk, n_main=n_main, n_cast=len(cast)),
        grid=(n_main + 1,),
        in_specs=[row, extra, pl.BlockSpec((None, 1, d), lambda i: (layer, 0, 0)),
                  _const_spec((d, ffn), lambda i: (0, 0)),
                  _const_spec((d, ffn), lambda i: (0, 0)),
                  _const_spec((ffn, d), lambda i: (0, 0))] + c_in,
        out_specs=[row, extra] + c_out,
        out_shape=[jax.ShapeDtypeStruct((n, d), F32), jax.ShapeDtypeStruct((ns, d), F32)] + c_shapes,
        compiler_params=_params(("arbitrary",)),
        name="ffn_half_step",
    )(xp, xs, ln, wg, wu, wd, *[w for w, _ in cast])
    return res[0], res[1], tuple(res[2:])


def _proj_in_kernel(x_ref, ln_ref, wi_ref, zc_ref, q_ref, k_ref, v_ref, zh_ref):
    h = _rms(x_ref[...], ln_ref[...]).astype(BF16)
    c0 = 0
    for ref in (zc_ref, q_ref, k_ref, v_ref, zh_ref):
        w = ref.shape[1]
        ref[...] = _dot(h, wi_ref[:, c0:c0 + w])
        c0 += w


def _proj_in(x, layer, ln, wi, *, tm, widths):
    n, d = x.shape
    row = lambda w: pl.BlockSpec((tm, w), lambda i: (i, 0))
    return pl.pallas_call(
        _proj_in_kernel,
        grid=(n // tm,),
        in_specs=[row(d), pl.BlockSpec((None, 1, d), lambda i: (layer, 0, 0)),
                  _const_spec(wi.shape, lambda i: (0, 0))],
        out_specs=[row(w) for w in widths],
        out_shape=[jax.ShapeDtypeStruct((n, w), F32) for w in widths],
        compiler_params=_params(("parallel",)),
        name="proj_in",
    )(x, ln, wi)


def _dense_out_kernel(*refs, f_chunk, final, n_main, n_cast):
    (xp_ref, ycp_ref, yap_ref, yhp_ref, xs_ref, ycs_ref, yas_ref, yhs_ref,
     wo_ref, ln2_ref, wg_ref, wu_ref, wd_ref, lnf_ref) = refs[:14]
    cast_src = refs[14:14 + n_cast]
    op_ref, os_ref = refs[14 + n_cast:16 + n_cast]
    cast_dst = refs[16 + n_cast:]

    def run(x_ref, yc_ref, ya_ref, yh_ref, o_ref):
        _cast_slabs(cast_src, cast_dst)
        y = jnp.concatenate([r[...].astype(BF16) for r in (yc_ref, ya_ref, yh_ref)], axis=-1)
        x2 = x_ref[...] + _dot(y, wo_ref[...])
        h = _rms(x2, ln2_ref[...]).astype(BF16)
        x3 = x2 + 0.5 * _swiglu(h, wg_ref, wu_ref, wd_ref, f_chunk)
        if final:
            x3 = _rms(x3, lnf_ref[...])
        o_ref[...] = x3

    _main_then_extra(n_main, lambda: run(xp_ref, ycp_ref, yap_ref, yhp_ref, op_ref),
                     lambda: run(xs_ref, ycs_ref, yas_ref, yhs_ref, os_ref))


def _dense_out(prompt, sample, layer, wo, ln2, wg, wu, wd, lnf, cast, *, tm, f_chunk, final):
    n, d = prompt[0].shape
    ns = sample[0].shape[0]
    ffn = wg.shape[1]
    n_main = n // tm
    row = lambda a: pl.BlockSpec((tm, a.shape[1]), lambda i: (jnp.minimum(i, n_main - 1), 0))
    extra = lambda a: pl.BlockSpec((ns, a.shape[1]), lambda i: (0, 0))
    vec = pl.BlockSpec((None, 1, d), lambda i: (layer, 0, 0))
    c_in, c_out, c_shapes = _cast_specs(cast, n_main + 1)
    res = pl.pallas_call(
        functools.partial(_dense_out_kernel, f_chunk=f_chunk, final=final, n_main=n_main, n_cast=len(cast)),
        grid=(n_main + 1,),
        in_specs=[row(a) for a in prompt] + [extra(a) for a in sample] + [
                  _const_spec((d, d), lambda i: (0, 0)),
                  vec,
                  _const_spec((d, ffn), lambda i: (0, 0)),
                  _const_spec((d, ffn), lambda i: (0, 0)),
                  _const_spec((ffn, d), lambda i: (0, 0)),
                  pl.BlockSpec((1, d), lambda i: (0, 0))] + c_in,
        out_specs=[row(prompt[0]), extra(sample[0])] + c_out,
        out_shape=[jax.ShapeDtypeStruct((n, d), F32), jax.ShapeDtypeStruct((ns, d), F32)] + c_shapes,
        compiler_params=_params(("arbitrary",)),
        name="dense_out",
    )(*prompt, *sample, wo, ln2, wg, wu, wd, lnf, *[w for w, _ in cast])
    return res[0], res[1], tuple(res[2:])


def _conv_tile(z, ext_ref, b, dww_ref, dwb, lng, lnb):
    tt, c2 = z.shape
    c = c2 // 2
    lo = CONV_HALO - (CONV_WIDTH - 1)
    u = z[:, :c] * jax.nn.sigmoid(z[:, c:])
    ext_ref[b, CONV_HALO:CONV_HALO + tt, :] = u
    y = None
    for r in range(SUBLANES):
        z_r = None
        for o in range(r, CONV_HALO + 1, SUBLANES):
            if o < lo:
                continue
            term = ext_ref[b, o - r:o - r + tt + SUBLANES, :] * dww_ref[o - lo:o - lo + 1, :]
            z_r = term if z_r is None else z_r + term
        part = z_r[r:r + tt, :]
        y = part if y is None else y + part
    y = y + dwb
    yc = y - jnp.mean(y, axis=-1, keepdims=True)
    yn = yc * lax.rsqrt(jnp.mean(yc * yc, axis=-1, keepdims=True) + EPS)
    carry = ext_ref[b, tt:tt + CONV_HALO, :]
    ext_ref[b, 0:CONV_HALO, :] = carry
    return _silu(yn * lng + lnb), u


def _conv_kernel(zc_ref, halo_ref, dww_ref, dwb_ref, lng_ref, lnb_ref, y_ref, tail_ref, ext_ref, *, nb, tt, tail):
    c = y_ref.shape[-1]

    def one_batch(b):
        @pl.when(pl.program_id(1) == 0)
        def _():
            ext_ref[b, 0:CONV_HALO, :] = halo_ref[b]
            ext_ref[b, CONV_HALO + tt:CONV_HALO + tt + SUBLANES, :] = jnp.zeros((SUBLANES, c), F32)

        y, u = _conv_tile(zc_ref[b], ext_ref, b, dww_ref, dwb_ref[...], lng_ref[...], lnb_ref[...])
        y_ref[b] = y.astype(y_ref.dtype)
        tail_ref[b] = u[tt - tail:, :]

    if nb == 1:
        one_batch(0)
    else:
        def body(b, carry):
            one_batch(b)
            return carry
        lax.fori_loop(0, nb, body, 0)


def _conv_group(zc, halo, layer, dww, dwb, lng, lnb, *, nb, tt, out_dtype):
    b, t, c2 = zc.shape
    c = c2 // 2
    tail = min(CONV_HALO, tt)
    vec = pl.BlockSpec((None, 1, c), lambda i, j: (layer, 0, 0))
    return pl.pallas_call(
        functools.partial(_conv_kernel, nb=nb, tt=tt, tail=tail),
        grid=(b // nb, t // tt),
        in_specs=[pl.BlockSpec((nb, tt, c2), lambda i, j: (i, j, 0)),
                  pl.BlockSpec((nb, CONV_HALO, c), lambda i, j: (i, 0, 0)),
                  pl.BlockSpec((None, CONV_HALO, c), lambda i, j: (layer, 0, 0)),
                  vec, vec, vec],
        out_specs=[pl.BlockSpec((nb, tt, c), lambda i, j: (i, j, 0)),
                   pl.BlockSpec((nb, tail, c), lambda i, j: (i, 0, 0))],
        out_shape=[jax.ShapeDtypeStruct((b, t, c), out_dtype),
                   jax.ShapeDtypeStruct((b, tail, c), F32)],
        scratch_shapes=[pltpu.VMEM((nb, CONV_HALO + tt + SUBLANES, c), F32)],
        compiler_params=_params(("parallel", "arbitrary")),
        name="conv_group",
    )(zc, halo, dww, dwb, lng, lnb)


def _alibi_slopes():
    return [2.0 ** (-8.0 * (h + 1) / ATT_HEADS) for h in range(ATT_HEADS)]


def _prompt_bias_tables():
    qi = np.arange(ATT_BLK)[:, None]
    ki = np.arange(2 * ATT_BLK)[None, :]
    j = qi - ki + ATT_BLK
    tabs = []
    for dil in DILATIONS:
        for first in (False, True):
            valid = (j >= 0) & (j <= ATT_BLK) & ((not first) | (ki >= ATT_BLK))
            rows = []
            for s in _alibi_slopes():
                bias = (-np.float32(s)) * (j * dil).astype(np.float32)
                rows.append(np.where(valid, bias, np.float32(NEG_BIG)).astype(np.float32))
            tabs.append(np.concatenate(rows, axis=0))
    return np.stack(tabs, axis=0)


def _head_pair_split(x, lo_mask):
    zero = jnp.zeros_like(x)
    return jnp.concatenate([jnp.where(lo_mask, x, zero), jnp.where(lo_mask, zero, x)], axis=0)


def _attn_prompt_kernel(bias_ref, q_ref, kc_ref, vc_ref, o_ref,
                        qs_ref, ks_ref, vs_ref, os_ref, ls_ref, *, span):
    j = pl.program_id(1)
    nslab = qs_ref.shape[0]
    scale = ATT_HDIM ** -0.5
    for sl in range(nslab):
        cols = slice(sl * LANES, (sl + 1) * LANES)
        qs_ref[sl] = q_ref[:, cols] * scale
        for dst, src in ((ks_ref, kc_ref), (vs_ref, vc_ref)):
            @pl.when(j == 0)
            def _(dst=dst, sl=sl):
                dst[sl, 0:span, :] = jnp.zeros((span, LANES), F32)

            @pl.when(j > 0)
            def _(dst=dst, sl=sl):
                dst[sl, 0:span, :] = dst[sl, span:2 * span, :]

            dst[sl, span:2 * span, :] = src[:, cols]

    lane = lax.broadcasted_iota(jnp.int32, (ATT_BLK, LANES), 1)
    lo_mask = lane < ATT_HDIM
    n_blocks = span // ATT_BLK

    order = sorted(range(len(DILATIONS)), key=lambda b: -DILATIONS[b])
    for pos, br in enumerate(order):
        dil = DILATIONS[br]

        def rows(start, size, dil=dil):
            if dil == 1:
                return pl.ds(start if isinstance(start, int) else pl.multiple_of(start, ATT_BLK), size)
            return pl.ds(start, size, stride=dil)

        def scores(i, dil=dil, rows=rows):
            q0 = (i // dil) * (ATT_BLK * dil) + i % dil
            k0 = span + q0 - ATT_BLK * dil
            s_parts, v_parts = [], []
            for sl in range(nslab):
                ql = qs_ref[sl, rows(q0, ATT_BLK), :]
                kl = ks_ref[sl, rows(k0, 2 * ATT_BLK), :]
                v_parts.append(vs_ref[sl, rows(k0, 2 * ATT_BLK), :].astype(BF16))
                lhs = _head_pair_split(ql, lo_mask).astype(BF16)
                s_parts.append(_dot_nt(lhs, kl.astype(BF16)))
            return jnp.concatenate(s_parts, axis=0), jnp.concatenate(v_parts, axis=1)

        def finish(i, s, v, br=br, dil=dil, pos=pos, rows=rows):
            q0 = (i // dil) * (ATT_BLK * dil) + i % dil
            first = jnp.logical_and(j == 0, i // dil == 0).astype(jnp.int32)
            bias = bias_ref[2 * br + first]
            s = jnp.where(bias > 0.5 * NEG_BIG, s + bias, NEG_BIG)
            m = jnp.max(s, axis=-1, keepdims=True)
            p = jnp.exp(s - m)
            l = jnp.sum(p, axis=-1, keepdims=True)
            pv = _dot(p.astype(BF16), v)
            lse_all = m + jnp.log(l)
            for sl in range(nslab):
                cols = slice(sl * LANES, (sl + 1) * LANES)
                ra = slice((2 * sl) * ATT_BLK, (2 * sl + 1) * ATT_BLK)
                rb = slice((2 * sl + 1) * ATT_BLK, (2 * sl + 2) * ATT_BLK)
                pair = lambda x: jnp.where(lo_mask, jnp.broadcast_to(x[ra], (ATT_BLK, LANES)),
                                           jnp.broadcast_to(x[rb], (ATT_BLK, LANES)))
                o_new = jnp.where(lo_mask, pv[ra, cols], pv[rb, cols]) / pair(l)
                lse_new = pair(lse_all)
                out_rows = rows(q0, ATT_BLK)
                if pos == 0:
                    o_tot, lse_tot = o_new, lse_new
                else:
                    o_old = os_ref[sl, out_rows, :]
                    lse_old = ls_ref[sl, out_rows, :]
                    mx = jnp.maximum(lse_old, lse_new)
                    e_old = jnp.exp(lse_old - mx)
                    e_new = jnp.exp(lse_new - mx)
                    den = e_old + e_new
                    o_tot = (e_old * o_old + e_new * o_new) / den
                    lse_tot = mx + jnp.log(den)
                os_ref[sl, out_rows, :] = o_tot
                ls_ref[sl, out_rows, :] = lse_tot

        def trip(t, carry, scores=scores, finish=finish):
            base = t * ATT_UNROLL
            ahead = scores(base)
            for u in range(ATT_UNROLL):
                s, v = ahead
                if u + 1 < ATT_UNROLL:
                    ahead = scores(base + u + 1)
                finish(base + u, s, v)
            return carry

        if n_blocks == ATT_UNROLL:
            trip(0, 0)
        else:
            lax.fori_loop(0, n_blocks // ATT_UNROLL, trip, 0)

    o_ref[...] = jnp.concatenate([os_ref[sl] for sl in range(nslab)], axis=-1).astype(o_ref.dtype)


def _attn_prompt(q, k, v, bias_tabs, *, span):
    b, t, w = q.shape
    nslab = w // LANES
    cur = pl.BlockSpec((None, span, w), lambda i, j: (i, j, 0))
    slab = lambda rows: pltpu.VMEM((nslab, rows, LANES), F32)
    return pl.pallas_call(
        functools.partial(_attn_prompt_kernel, span=span),
        grid=(b, t // span),
        in_specs=[_const_spec(bias_tabs.shape, lambda i, j: (0, 0, 0)), cur, cur, cur],
        out_specs=cur,
        out_shape=jax.ShapeDtypeStruct((b, t, w), BF16),
        scratch_shapes=[slab(span), slab(2 * span), slab(2 * span), slab(span), slab(span)],
        compiler_params=_params(("arbitrary", "arbitrary")),
        name="attn_prompt",
    )(bias_tabs, q, k, v)


def _sample_bias_tables(n_past, t_new, t_pad):
    slopes = _alibi_slopes()
    n_keys = n_past + t_pad
    tabs = np.full((len(DILATIONS), ATT_HEADS * t_pad, n_keys), NEG_BIG, np.float32)
    for br, dil in enumerate(DILATIONS):
        for h, s in enumerate(slopes):
            for t in range(t_pad):
                tq = min(t, t_new - 1)
                for jj in range(ATT_BLK + 1):
                    idx = n_past + tq - jj * dil
                    if idx < 0:
                        continue
                    tabs[br, h * t_pad + t, idx] = -np.float32(s) * np.float32(jj * dil)
    return tabs[:, :, :n_past], tabs[:, :, n_past:]


def _attn_sample_kernel(bc_ref, bn_ref, q_ref, kn_ref, vn_ref, kc_ref, vc_ref, o_ref):
    for b in range(q_ref.shape[0]):
        _attn_sample_one(bc_ref, bn_ref, q_ref.at[b], kn_ref.at[b], vn_ref.at[b], kc_ref.at[b], vc_ref.at[b],
                         o_ref.at[b])


def _attn_sample_one(bc_ref, bn_ref, q_ref, kn_ref, vn_ref, kc_ref, vc_ref, o_ref):
    t_pad, w = q_ref.shape
    scale = ATT_HDIM ** -0.5
    q = q_ref[...] * scale
    lane = lax.broadcasted_iota(jnp.int32, (t_pad, w), 1)
    head_of_lane = lane // ATT_HDIM
    zero = jnp.zeros_like(q)
    lhs = jnp.concatenate([jnp.where(head_of_lane == h, q, zero) for h in range(ATT_HEADS)], axis=0).astype(BF16)
    s_c = _dot(lhs, kc_ref[...].astype(BF16))
    s_n = _dot_nt(lhs, kn_ref[...].astype(BF16))
    vc = vc_ref[...].astype(BF16)
    vn = vn_ref[...].astype(BF16)
    pcs, pns, ls, lses = [], [], [], []
    for br in range(len(DILATIONS)):
        bc = bc_ref[br]
        bn = bn_ref[br]
        sc = jnp.where(bc > 0.5 * NEG_BIG, s_c + bc, NEG_BIG)
        sn = jnp.where(bn > 0.5 * NEG_BIG, s_n + bn, NEG_BIG)
        m = jnp.maximum(jnp.max(sc, axis=-1, keepdims=True), jnp.max(sn, axis=-1, keepdims=True))
        pc = jnp.exp(sc - m)
        pn = jnp.exp(sn - m)
        l = jnp.sum(pc, axis=-1, keepdims=True) + jnp.sum(pn, axis=-1, keepdims=True)
        pcs.append(pc.astype(BF16))
        pns.append(pn.astype(BF16))
        ls.append(l)
        lses.append(m + jnp.log(l))
    nq = ATT_HEADS * t_pad
    pv = _dot_nt(jnp.concatenate(pcs, axis=0), vc) + _dot(jnp.concatenate(pns, axis=0), vn)
    outs = [pv[br * nq:(br + 1) * nq] / ls[br] for br in range(len(DILATIONS))]
    mx = functools.reduce(jnp.maximum, lses)
    es = [jnp.exp(x - mx) for x in lses]
    den = functools.reduce(lambda a, b: a + b, es)
    o = functools.reduce(lambda a, b: a + b, [e * x for e, x in zip(es, outs)]) / den
    res = zero
    for h in range(ATT_HEADS):
        res = jnp.where(head_of_lane == h, o[h * t_pad:(h + 1) * t_pad, :], res)
    o_ref[...] = res.astype(o_ref.dtype)


def _attn_sample(q, k, v, cache_k, cache_v, layer, bias_c, bias_n, *, nb):
    b, t_pad, w = q.shape
    n_past = cache_k.shape[3]
    new = pl.BlockSpec((nb, t_pad, w), lambda i: (i, 0, 0))
    cache = pl.BlockSpec((None, nb, w, n_past), lambda i: (layer, i, 0, 0))
    return pl.pallas_call(
        _attn_sample_kernel,
        grid=(b // nb,),
        in_specs=[_const_spec(bias_c.shape, lambda i: (0, 0, 0)),
                  _const_spec(bias_n.shape, lambda i: (0, 0, 0)),
                  new, new, new, cache, cache],
        out_specs=new,
        out_shape=jax.ShapeDtypeStruct((b, t_pad, w), F32),
        compiler_params=_params(("parallel",)),
        name="attn_sample",
    )(bias_c, bias_n, q, k, v, cache_k, cache_v)


def _hgrn_tables(chunk):
    sub = min(chunk, LANES)
    r = np.arange(sub)[:, None]
    c = np.arange(sub)[None, :]
    x = np.bitwise_xor(r, c)
    hb = np.zeros_like(x)
    for bit in range(1, 16):
        hb = np.where(x >> bit > 0, bit, hb)
    lev = np.where(c > r, -1, np.where(r // SUBLANES == c // SUBLANES, 0, hb)).astype(np.int32)
    lev = np.tile(lev, (chunk // sub, 1))
    sel = np.zeros((SUBLANES * LANES, sub), np.float32)
    for s in range(SUBLANES):
        sel[s * LANES:(s + 1) * LANES, s::SUBLANES] = 1.0
    rr = np.arange(chunk)
    tri = (rr[None, :] <= rr[:, None]).astype(np.float32)
    return lev, sel, tri


def _split3(x):
    hi = x.astype(BF16)
    r1 = x - hi.astype(F32)
    mid = r1.astype(BF16)
    lo = (r1 - mid.astype(F32)).astype(BF16)
    return hi, mid, lo


def _group_row(x, s):
    c, l = x.shape
    x3 = x.reshape(c // SUBLANES, SUBLANES, l)
    return jnp.broadcast_to(x3[:, s:s + 1, :], x3.shape).reshape(c, l)


def _block_last(x, m):
    c, l = x.shape
    x3 = x.reshape(c // m, m, l)
    return jnp.broadcast_to(x3[:, m - 1:m, :], x3.shape).reshape(c, l)


def _hgrn_lower_bound(raw, layer):
    e = jnp.exp(raw - jnp.max(raw, axis=0, keepdims=True))
    sm = e / jnp.sum(e, axis=0, keepdims=True)
    lb = jnp.zeros_like(sm[0:1])
    for i in range(1, layer + 1):
        lb = lb + sm[i:i + 1]
    return lb


def _hgrn_chunk_gates(zq, zf, zi, lb, tri, live):
    dk = zq.shape[1]
    q = _silu(zq)
    f = lb + (1.0 - lb) * jax.nn.sigmoid(zf)
    g = jnp.log2(f)
    k = 1.0 - f
    if live is not None:
        g = jnp.where(live, g, 0.0)
        k = jnp.where(live, k, 0.0)
    b3 = _dot(tri, jnp.concatenate(_split3(g), axis=1))
    b = b3[:, :dk] + b3[:, dk:2 * dk] + b3[:, 2 * dk:]
    return q, k, b, zi.astype(BF16)


def _hgrn_chunk_local(q, k, b, vb, lev, sel):
    chunk, dk = q.shape
    sub = lev.shape[1]

    vals = []
    for s in range(SUBLANES):
        decay = jnp.exp2(jnp.minimum(b - _group_row(b, s), 0.0))
        vals.append((q * decay * _group_row(k, s)).astype(BF16))
    a = jnp.where(lev == 0, _dot(jnp.concatenate(vals, axis=1), sel), 0.0)
    subs = [slice(r0, r0 + sub) for r0 in range(0, chunk, sub)]
    for bit in range(3, int(math.log2(sub))):
        m = 1 << bit
        b_end = _block_last(b, m)
        b_prev = jnp.concatenate([jnp.zeros((m, dk), F32), b_end[:chunk - m]], axis=0)
        qd = (q * jnp.exp2(b - b_prev)).astype(BF16)
        kd = (k * jnp.exp2(b_end - b)).astype(BF16)
        a_l = jnp.concatenate([_dot_nt(qd[rs], kd[rs]) for rs in subs], axis=0)
        a = jnp.where(lev == bit, a_l, a)
    ab = a.astype(BF16)
    o_parts = []
    for i, rs in enumerate(subs):
        if i == 0:
            o_parts.append(_dot(ab[rs], vb[rs]))
            continue
        b_piv = b[rs.start - 1:rs.start, :]
        qd = (q[rs] * jnp.exp2(b[rs] - b_piv)).astype(BF16)
        kd = (k[:rs.start] * jnp.exp2(b_piv - b[:rs.start])).astype(BF16)
        a_row = jnp.concatenate([_dot_nt(qd, kd).astype(BF16), ab[rs]], axis=1)
        o_parts.append(_dot(a_row, vb[:rs.stop]))
    o = jnp.concatenate(o_parts, axis=0) if len(o_parts) > 1 else o_parts[0]
    q_in = (q * jnp.exp2(b)).astype(BF16)
    b_last = b[chunk - 1:chunk, :]
    kd = (k * jnp.exp2(b_last - b)).astype(BF16)
    return o, q_in, _dot_tn(vb, kd), jnp.exp2(b_last)


def _hgrn_chunk_finish(o_local, q_in, st, ng, zg):
    o = o_local + _dot_nt(q_in, st.astype(BF16))
    on = o * lax.rsqrt(jnp.mean(o * o, axis=-1, keepdims=True) + EPS) * ng
    return on * _silu(zg)


def _hgrn_sample_kernel(zh_ref, lb_ref, ng_ref, lev_ref, sel_ref, tri_ref, s0_ref, o_ref, sf_ref,
                        *, nb, layer, t_valid):
    t_pad = zh_ref.shape[1]
    hd = o_ref.shape[2]
    dk = hd // HG_HEADS
    lev = lev_ref[...]
    sel = sel_ref[...]
    tri = tri_ref[...]
    ng = ng_ref[...]
    live = lax.broadcasted_iota(jnp.int32, (t_pad, dk), 0) < t_valid
    lbs = [_hgrn_lower_bound(lb_ref[:, h * dk:(h + 1) * dk], layer) for h in range(HG_HEADS)]

    def one_batch(bi, carry):
        part = lambda h, p: zh_ref[bi, :, (p * HG_HEADS + h) * dk:(p * HG_HEADS + h + 1) * dk]
        heads = range(HG_HEADS)
        gates = [_hgrn_chunk_gates(part(h, 0), part(h, 1), part(h, 2), lbs[h], tri, live) for h in heads]
        states = [s0_ref[bi, h].T for h in heads]
        local = [_hgrn_chunk_local(*gates[h], lev, sel) for h in heads]
        for h in heads:
            o_local, q_in, s_own, d_all = local[h]
            o_ref[bi, :, h * dk:(h + 1) * dk] = _hgrn_chunk_finish(o_local, q_in, states[h], ng, part(h, 3))
            sf_ref[bi, h] = (states[h] * d_all + s_own).T
        return carry

    lax.fori_loop(0, nb, one_batch, 0, unroll=2)


def _hgrn_sample(zh, lb_raw, norm_g, s0, layer, tables, *, nb, t_valid):
    b, t_pad, w4 = zh.shape
    hd = w4 // 4
    dk = hd // HG_HEADS
    depth = lb_raw.shape[0]
    lev, sel, tri = tables
    return pl.pallas_call(
        functools.partial(_hgrn_sample_kernel, nb=nb, layer=layer, t_valid=t_valid),
        grid=(b // nb,),
        in_specs=[pl.BlockSpec((nb, t_pad, w4), lambda i: (i, 0, 0)),
                  pl.BlockSpec((depth, hd), lambda i: (0, 0)),
                  pl.BlockSpec((None, 1, dk), lambda i: (layer, 0, 0)),
                  _const_spec(lev.shape, lambda i: (0, 0)),
                  _const_spec(sel.shape, lambda i: (0, 0)),
                  _const_spec(tri.shape, lambda i: (0, 0)),
                  pl.BlockSpec((None, nb, HG_HEADS, dk, dk), lambda i: (layer, i, 0, 0, 0))],
        out_specs=[pl.BlockSpec((nb, t_pad, hd), lambda i: (i, 0, 0)),
                   pl.BlockSpec((nb, HG_HEADS, dk, dk), lambda i: (i, 0, 0, 0))],
        out_shape=[jax.ShapeDtypeStruct((b, t_pad, hd), F32),
                   jax.ShapeDtypeStruct((b, HG_HEADS, dk, dk), F32)],
        compiler_params=_params(("parallel",)),
        name="hgrn_sample",
    )(zh, lb_raw, norm_g, lev, sel, tri, s0)


def _proj_mix_kernel(x_ref, ln_ref, wi_ref, dww_ref, dwb_ref, lng_ref, lnb_ref, lb_ref, ng_ref,
                     lev_ref, sel_ref, tri_ref,
                     yc_ref, q_ref, k_ref, v_ref, yh_ref, tail_ref, sf_ref, kt_ref, vt_ref,
                     ext_ref, st_ref, *, layer, chunk):
    tstep = pl.program_id(1)
    tt = x_ref.shape[0]
    conv_c = yc_ref.shape[1]
    att = q_ref.shape[1]
    hd = yh_ref.shape[1]
    dk = hd // HG_HEADS
    n_chunks = tt // chunk

    @pl.when(tstep == 0)
    def _():
        ext_ref[0, 0:CONV_HALO, :] = jnp.zeros((CONV_HALO, conv_c), F32)
        ext_ref[0, CONV_HALO + tt:CONV_HALO + tt + SUBLANES, :] = jnp.zeros((SUBLANES, conv_c), F32)
        st_ref[...] = jnp.zeros_like(st_ref)

    h = _rms(x_ref[...], ln_ref[...]).astype(BF16)
    c0 = 2 * conv_c
    zc = _dot(h, wi_ref[:, 0:c0])
    zh = _dot(h, wi_ref[:, c0 + 3 * att:c0 + 3 * att + 4 * hd])
    yc, u = _conv_tile(zc, ext_ref, 0, dww_ref, dwb_ref[...], lng_ref[...], lnb_ref[...])
    yc_ref[...] = yc.astype(yc_ref.dtype)
    tail_ref[...] = u[tt - CONV_HALO:, :]
    for i, (ref, t_ref) in enumerate(((q_ref, None), (k_ref, kt_ref), (v_ref, vt_ref))):
        z = _dot(h, wi_ref[:, c0 + i * att:c0 + (i + 1) * att])
        ref[...] = z
        if t_ref is not None:
            t_ref[...] = z.T

    lev = lev_ref[...]
    sel = sel_ref[...]
    tri = tri_ref[...]
    ng = ng_ref[...]
    lbs = [_hgrn_lower_bound(lb_ref[:, hh * dk:(hh + 1) * dk], layer) for hh in range(HG_HEADS)]

    def part(u_idx, p):
        hh, c = divmod(u_idx, n_chunks)
        return zh[c * chunk:(c + 1) * chunk, (p * HG_HEADS + hh) * dk:(p * HG_HEADS + hh + 1) * dk]

    def gates(u_idx):
        return _hgrn_chunk_gates(part(u_idx, 0), part(u_idx, 1), part(u_idx, 2), lbs[u_idx // n_chunks], tri, None)

    ahead = gates(0)
    st = None
    for u_idx in range(HG_HEADS * n_chunks):
        hh, c = divmod(u_idx, n_chunks)
        cur = ahead
        if u_idx + 1 < HG_HEADS * n_chunks:
            ahead = gates(u_idx + 1)
        if c == 0:
            st = st_ref[hh]
        o_local, q_in, s_own, d_all = _hgrn_chunk_local(*cur, lev, sel)
        o = _hgrn_chunk_finish(o_local, q_in, st, ng, part(u_idx, 3))
        yh_ref[c * chunk:(c + 1) * chunk, hh * dk:(hh + 1) * dk] = o.astype(yh_ref.dtype)
        st = st * d_all + s_own
        if c == n_chunks - 1:
            st_ref[hh] = st

    @pl.when(tstep == pl.num_programs(1) - 1)
    def _():
        for hh in range(HG_HEADS):
            sf_ref[hh] = st_ref[hh].T


def _proj_mix(x, layer, ln, wi, dww, dwb, lng, lnb, lb_raw, norm_g, tables, *, batch, tm, chunk, widths, keep):
    n, d = x.shape
    conv_c = widths[0] // 2
    att = widths[1]
    hd = widths[4] // 4
    dk = hd // HG_HEADS
    depth = lb_raw.shape[0]
    nt = n // batch // tm
    lev, sel, tri = tables
    row = lambda w: pl.BlockSpec((tm, w), lambda b, j: (b * nt + j, 0))
    cvec = pl.BlockSpec((None, 1, conv_c), lambda b, j: (layer, 0, 0))
    const2 = lambda a: _const_spec(a.shape, lambda b, j: (0, 0))
    first_kept = nt - keep // tm
    window = pl.BlockSpec((None, att, tm), lambda b, j: (b, 0, jnp.maximum(j - first_kept, 0)))
    return pl.pallas_call(
        functools.partial(_proj_mix_kernel, layer=layer, chunk=chunk),
        grid=(batch, nt),
        in_specs=[row(d), pl.BlockSpec((None, 1, d), lambda b, j: (layer, 0, 0)),
                  _const_spec(wi.shape, lambda b, j: (0, 0)),
                  pl.BlockSpec((None, CONV_HALO, conv_c), lambda b, j: (layer, 0, 0)),
                  cvec, cvec, cvec,
                  pl.BlockSpec((depth, hd), lambda b, j: (0, 0)),
                  pl.BlockSpec((None, 1, dk), lambda b, j: (layer, 0, 0)),
                  const2(lev), const2(sel), const2(tri)],
        out_specs=[row(conv_c), row(att), row(att), row(att), row(hd),
                   pl.BlockSpec((None, CONV_HALO, conv_c), lambda b, j: (b, 0, 0)),
                   pl.BlockSpec((None, HG_HEADS, dk, dk), lambda b, j: (b, 0, 0, 0)),
                   window, window],
        out_shape=[jax.ShapeDtypeStruct((n, conv_c), BF16),
                   jax.ShapeDtypeStruct((n, att), F32),
                   jax.ShapeDtypeStruct((n, att), F32),
                   jax.ShapeDtypeStruct((n, att), F32),
                   jax.ShapeDtypeStruct((n, hd), BF16),
                   jax.ShapeDtypeStruct((batch, CONV_HALO, conv_c), F32),
                   jax.ShapeDtypeStruct((batch, HG_HEADS, dk, dk), F32),
                   jax.ShapeDtypeStruct((batch, att, keep), F32),
                   jax.ShapeDtypeStruct((batch, att, keep), F32)],
        scratch_shapes=[pltpu.VMEM((1, CONV_HALO + tm + SUBLANES, conv_c), F32),
                        pltpu.VMEM((HG_HEADS, dk, dk), F32)],
        compiler_params=_params(("parallel", "arbitrary")),
        name="proj_mix",
    )(x, ln, wi, dww, dwb, lng, lnb, lb_raw, norm_g, lev, sel, tri)


def _pad_rows(x, rows):
    return jnp.pad(x, ((0, 0), (0, rows - x.shape[1]), (0, 0)))


def _key_minor(cache):
    depth, b, n_past, h, hd = cache.shape
    return jnp.transpose(cache, (0, 1, 3, 4, 2)).reshape(depth, b, h * hd, n_past)


def kernel(x_prompt, x_sample, state_conv, cache_k_win, cache_v_win, state_hgrn, ln_ffn1, w_ffn1_gate, w_ffn1_up, w_ffn1_down, ln_mix, w_in, conv_dw_w, conv_dw_b, conv_ln_g, conv_ln_b, hg_lower_bounds, hg_norm_g, w_out, ln_ffn2, w_ffn2_gate, w_ffn2_up, w_ffn2_down, ln_final):
    bp, seq, d = x_prompt.shape
    bs, t_new, _ = x_sample.shape
    depth = w_in.shape[0]
    conv_dim = conv_dw_w.shape[2]
    att_dim = ATT_HEADS * ATT_HDIM
    hg_dim = hg_lower_bounds.shape[1]
    widths = (2 * conv_dim, att_dim, att_dim, att_dim, 4 * hg_dim)
    n_past = cache_k_win.shape[2]
    t_pad = SUBLANES
    keep = min(ATT_SPAN, seq)

    ffn1_f32 = (w_ffn1_gate, w_ffn1_up, w_ffn1_down)
    late_f32 = (w_in, w_out, w_ffn2_gate, w_ffn2_up, w_ffn2_down)
    ffn1_w = tuple(w[0].astype(BF16) for w in ffn1_f32)
    vec3 = lambda a: a.reshape(depth, 1, a.shape[-1])
    ln1, lnm, ln2 = vec3(ln_ffn1), vec3(ln_mix), vec3(ln_ffn2)
    dwb, clg, clb, hgn = vec3(conv_dw_b), vec3(conv_ln_g), vec3(conv_ln_b), vec3(hg_norm_g)
    lnf = ln_final.reshape(1, d)
    dww = jnp.pad(conv_dw_w, ((0, 0), (0, CONV_HALO - CONV_WIDTH), (0, 0)))
    cache_k, cache_v = _key_minor(cache_k_win), _key_minor(cache_v_win)
    halo_s = jnp.pad(state_conv, ((0, 0), (0, 0), (CONV_HALO - (CONV_WIDTH - 1), 0), (0, 0)))

    bias_p = jnp.asarray(_prompt_bias_tables())
    bias_c, bias_n = (jnp.asarray(a) for a in _sample_bias_tables(n_past, t_new, t_pad))
    chunk_p = 128
    tabs_p = tuple(jnp.asarray(a, dt) for a, dt in zip(_hgrn_tables(chunk_p), (jnp.int32, BF16, BF16)))
    tabs_s = tuple(jnp.asarray(a, dt) for a, dt in zip(_hgrn_tables(t_pad), (jnp.int32, BF16, BF16)))

    hp = x_prompt.reshape(bp * seq, d)
    hs = x_sample.reshape(bs * t_new, d)
    outs = [[] for _ in range(8)]
    for l in range(depth):
        final = l == depth - 1
        x1, x1s, (wi, wo, *ffn2_w) = _ffn_half_step(hp, hs, l, ln1, *ffn1_w, [(w, l) for w in late_f32],
                                                    tm=512, f_chunk=1408)
        yc, q, k, v, yh, tail, sp, kt, vt = _proj_mix(x1, l, lnm, wi, dww, dwb, clg, clb, hg_lower_bounds, hgn,
                                                      tabs_p, batch=bp, tm=512, chunk=chunk_p, widths=widths,
                                                      keep=keep)
        seq3 = lambda a: a.reshape(bp, seq, a.shape[-1])
        ya = _attn_prompt(seq3(q), seq3(k), seq3(v), bias_p, span=ATT_SPAN)
        window = lambda a: a.reshape(bp, ATT_HEADS, ATT_HDIM, keep).transpose(0, 3, 1, 2)
        outs[0].append(tail[:, -(CONV_WIDTH - 1):])
        outs[2].append(window(kt))
        outs[3].append(window(vt))
        outs[6].append(sp)

        zcs, qs, ks, vs, zhs = _proj_in(x1s, l, lnm, wi, tm=bs * t_new, widths=widths)
        new3 = lambda a: _pad_rows(a.reshape(bs, t_new, a.shape[-1]), t_pad)
        ycs, us = _conv_group(new3(zcs), halo_s[l], l, dww, dwb, clg, clb, nb=bs, tt=t_pad, out_dtype=F32)
        yas = _attn_sample(new3(qs), new3(ks), new3(vs), cache_k, cache_v, l, bias_c, bias_n, nb=2)
        yhs, ss = _hgrn_sample(new3(zhs), hg_lower_bounds, hgn, state_hgrn, l, tabs_s, nb=8, t_valid=t_new)
        unpad = lambda a: a[:, :t_new].reshape(bs * t_new, a.shape[-1])
        nxt = [(w, l + 1) for w in ffn1_f32] if not final else []
        hp, hs, ffn1_w = _dense_out((x1, yc, ya.reshape(bp * seq, att_dim), yh),
                                    (x1s, unpad(ycs), unpad(yas), unpad(yhs)),
                                    l, wo, ln2, *ffn2_w, lnf, nxt, tm=512, f_chunk=1408, final=final)
        outs[1].append(jnp.concatenate([state_conv[l][:, t_new:], us[:, :t_new]], axis=1))
        outs[4].append(ks.reshape(bs, t_new, ATT_HEADS, ATT_HDIM))
        outs[5].append(vs.reshape(bs, t_new, ATT_HEADS, ATT_HDIM))
        outs[7].append(ss)

    y_prompt = hp.reshape(bp, seq, d)
    y_sample = hs.reshape(bs, t_new, d)
    st = [jnp.stack(o) for o in outs]
    return (y_prompt, y_sample, st[0], st[1], st[2], st[3], st[4], st[5], st[6], st[7])
```
